```python
import functools
import jax, jax.numpy as jnp
from jax import lax
import numpy as np

D_MODEL = 1024
BATCH = 16
SEQ = 2048
DEPTH = 1
DEC_BATCH = 128
DEC_SEQ = 1
PAST_LEN = 16384
PAGE_SIZE = 128

MLA_HEADS = 8
Q_LORA = 256
KV_LORA = 128
QK_NOPE = 128
QK_ROPE = 64
V_HEAD = 128
ROPE_THETA = 10000.0
ATTN_SCALE = (QK_NOPE + QK_ROPE) ** -0.5
Q_BLOCK = 128
GLA_HEADS = 4
GLA_DK = 128
GLA_DV = 256
GLA_RANK = 16
GLA_TAU = 16.0
GLA_CHUNK = 64
N_MEM = 256
MEM_HEADS = 4
MEM_HEAD_DIM = 128
N_EXPERTS = 32
TOP_K = 4
D_FF = 1024
SWIGLU_LIMIT = 7.0
SWIGLU_ALPHA = 1.702
EPS = 1e-6

MLA_V_WIDTH = MLA_HEADS * V_HEAD
GLA_K_WIDTH = GLA_HEADS * GLA_DK
GLA_V_WIDTH = GLA_HEADS * GLA_DV
MEM_WIDTH = MEM_HEADS * MEM_HEAD_DIM
IN_SPLITS = (Q_LORA, KV_LORA, QK_ROPE, GLA_K_WIDTH, GLA_K_WIDTH, GLA_V_WIDTH, GLA_V_WIDTH, GLA_RANK, D_MODEL, D_MODEL)
D_IN = Q_LORA + KV_LORA + QK_ROPE + 2 * GLA_K_WIDTH + 2 * GLA_V_WIDTH + GLA_RANK + 2 * D_MODEL

kernel_name = "hybrid_mla_gla_moe_decode_step"


def rmsnorm(x, g):
    xf = x.astype(jnp.float32)
    var = jnp.mean(xf * xf, axis=-1, keepdims=True)
    return (xf * lax.rsqrt(var + EPS)).astype(x.dtype) * g


def rope(x, pos):
    half = QK_ROPE // 2
    inv = ROPE_THETA ** (-jnp.arange(half, dtype=jnp.float32) * 2.0 / QK_ROPE)
    ang = pos[:, None] * inv[None, :]
    ang = ang.reshape((ang.shape[0],) + (1,) * (x.ndim - 3) + (half,))
    cos, sin = jnp.cos(ang), jnp.sin(ang)
    xf = x.astype(jnp.float32)
    x1, x2 = xf[..., :half], xf[..., half:]
    return jnp.concatenate([x1 * cos - x2 * sin, x1 * sin + x2 * cos], axis=-1).astype(x.dtype)


def mla_prompt(q_lat, q_rope, c_kv, k_rope):
    B, S, H, _ = q_lat.shape
    nb = S // Q_BLOCK
    ql = jnp.swapaxes(q_lat.reshape(B, nb, Q_BLOCK, H, KV_LORA), 0, 1)
    qr = jnp.swapaxes(q_rope.reshape(B, nb, Q_BLOCK, H, QK_ROPE), 0, 1)
    kpos = jnp.arange(S)

    def block(args):
        i, qlb, qrb = args
        s = (jnp.einsum('bqhr,bkr->bhqk', qlb, c_kv) + jnp.einsum('bqhp,bkp->bhqk', qrb, k_rope)).astype(jnp.float32) * ATTN_SCALE
        qpos = i * Q_BLOCK + jnp.arange(Q_BLOCK)
        s = jnp.where(kpos[None, :] <= qpos[:, None], s, -jnp.inf)
        p = jax.nn.softmax(s, axis=-1).astype(c_kv.dtype)
        return jnp.einsum('bhqk,bkr->bqhr', p, c_kv)

    o = lax.map(block, (jnp.arange(nb), ql, qr))
    return jnp.swapaxes(o, 0, 1).reshape(B, S, H, KV_LORA)


def mla_sample(q_lat, q_rope, c_kv, k_rope, cache_lat, cache_rope, page_table):
    Bd, S = q_lat.shape[:2]
    past_len = page_table.shape[1] * cache_lat.shape[1]
    past_lat = cache_lat[page_table].reshape(Bd, past_len, KV_LORA)
    past_rope = cache_rope[page_table].reshape(Bd, past_len, QK_ROPE)
    keys_lat = jnp.concatenate([past_lat, c_kv.astype(past_lat.dtype)], axis=1)
    keys_rope = jnp.concatenate([past_rope, k_rope.astype(past_rope.dtype)], axis=1)
    s = (jnp.einsum('bqhr,bkr->bhqk', q_lat, keys_lat) + jnp.einsum('bqhp,bkp->bhqk', q_rope, keys_rope)).astype(jnp.float32) * ATTN_SCALE
    kpos = jnp.arange(past_len + S)
    qpos = past_len + jnp.arange(S)
    s = jnp.where(kpos[None, :] <= qpos[:, None], s, -jnp.inf)
    p = jax.nn.softmax(s, axis=-1).astype(keys_lat.dtype)
    return jnp.einsum('bhqk,bkr->bqhr', p, keys_lat)


def gla_prompt(q, k, v, la):
    B, S, H, DK = q.shape
    DV = v.shape[-1]
    C = GLA_CHUNK
    n = S // C
    f32 = jnp.float32

    def to_chunks(t):
        return t.astype(f32).reshape(B, n, C, H, t.shape[-1]).transpose(1, 0, 3, 2, 4)

    qc, kc, vc, lac = to_chunks(q), to_chunks(k), to_chunks(v), to_chunks(la)
    b = jnp.cumsum(lac, axis=3)
    b_last = b[:, :, :, -1:, :]
    q_in = qc * jnp.exp(b)
    k_in = kc * jnp.exp(-b)
    k_out = kc * jnp.exp(b_last - b)
    mask = jnp.arange(C)[:, None] >= jnp.arange(C)[None, :]
    a = jnp.where(mask, jnp.einsum('nbhtd,nbhsd->nbhts', q_in, k_in), 0.0)
    intra = jnp.einsum('nbhts,nbhsv->nbhtv', a, vc)
    kv = jnp.einsum('nbhsd,nbhsv->nbhdv', k_out, vc)
    decay = jnp.exp(b_last[:, :, :, 0, :])

    def step(st, inp):
        d_c, kv_c = inp
        return d_c[..., None] * st + kv_c, st

    s_final, s_prev = lax.scan(step, jnp.zeros((B, H, DK, DV), f32), (decay, kv))
    inter = jnp.einsum('nbhtd,nbhdv->nbhtv', q_in, s_prev)
    o = (intra + inter).transpose(1, 0, 3, 2, 4).reshape(B, S, H, DV)
    return o.astype(q.dtype), s_final.astype(q.dtype)


def gla_sample(q, k, v, la, state):
    f32 = jnp.float32
    xs = (jnp.swapaxes(q.astype(f32), 0, 1), jnp.swapaxes(k.astype(f32), 0, 1),
          jnp.swapaxes(v.astype(f32), 0, 1), jnp.swapaxes(la.astype(f32), 0, 1))

    def step(st, inp):
        qt, kt, vt, lat = inp
        st = jnp.exp(lat)[..., None] * st + kt[..., :, None] * vt[..., None, :]
        return st, jnp.einsum('bhd,bhdv->bhv', qt, st)

    s_final, o = lax.scan(step, state.astype(f32), xs)
    return jnp.swapaxes(o, 0, 1).astype(q.dtype), s_final.astype(state.dtype)


def memory_kv(mem, g_mem, w_mk, w_mv):
    B, M, _ = mem.shape
    mn = rmsnorm(mem, g_mem)
    return ((mn @ w_mk).reshape(B, M, MEM_HEADS, MEM_HEAD_DIM),
            (mn @ w_mv).reshape(B, M, MEM_HEADS, MEM_HEAD_DIM))


def cross_attend(xn, mem_k, mem_v, w_mq, w_mo):
    B, S, _ = xn.shape
    q = (xn @ w_mq).reshape(B, S, MEM_HEADS, MEM_HEAD_DIM)
    s = jnp.einsum('bshd,bmhd->bhsm', q, mem_k).astype(jnp.float32) * MEM_HEAD_DIM ** -0.5
    p = jax.nn.softmax(s, axis=-1).astype(mem_v.dtype)
    o = jnp.einsum('bhsm,bmhd->bshd', p, mem_v).reshape(B, S, MEM_WIDTH)
    return o @ w_mo


def moe(xn, w_router, b_router, w_gate_up, b_gate_up, w_down, b_down):
    B, S, D = xn.shape
    xt = xn.reshape(B * S, D)
    logits = (xt @ w_router + b_router).astype(jnp.float32)
    top_val, top_idx = lax.top_k(logits, TOP_K)
    w_top = jax.nn.softmax(top_val, axis=-1)
    combine = jnp.einsum('tk,tke->te', w_top, jax.nn.one_hot(top_idx, N_EXPERTS, dtype=jnp.float32)).astype(xt.dtype)
    out = jnp.zeros_like(xt)
    for e in range(N_EXPERTS):
        a = xt @ w_gate_up[e] + b_gate_up[e]
        gate = jnp.minimum(a[:, :D_FF], SWIGLU_LIMIT)
        up = jnp.clip(a[:, D_FF:], -SWIGLU_LIMIT, SWIGLU_LIMIT)
        h = (up + 1.0) * gate * jax.nn.sigmoid(SWIGLU_ALPHA * gate)
        out = out + combine[:, e:e + 1] * (h @ w_down[e] + b_down[e])
    return out.reshape(B, S, D)


def decoder_layer(x, pos, mem_k, mem_v, attend, recur,
                  g_attn, w_in, g_q, g_kv, w_uq, w_uk, w_uv, w_a2, b_a, g_gla,
                  w_o_mla, w_o_gla, w_out, g_cross, w_mq, w_mo,
                  g_ffn, w_router, b_router, w_gate_up, b_gate_up, w_down, b_down):
    B, S, _ = x.shape
    xn = rmsnorm(x, g_attn)
    split_idx = np.cumsum(np.array(IN_SPLITS))[:-1].tolist()
    cq, ckv, kr, gq, gk, gv, gg, ga, z_mla, z_gla = jnp.split(xn @ w_in, split_idx, axis=-1)
    c_q = rmsnorm(cq, g_q)
    q = (c_q @ w_uq).reshape(B, S, MLA_HEADS, QK_NOPE + QK_ROPE)
    q_nope = q[..., :QK_NOPE]
    q_rope = rope(q[..., QK_NOPE:], pos)
    q_lat = jnp.einsum('bshn,rhn->bshr', q_nope, w_uk)
    c_kv = rmsnorm(ckv, g_kv)
    k_rope = rope(kr, pos)
    o_lat = attend(q_lat, q_rope, c_kv, k_rope)
    o_mla = jnp.einsum('bshr,rhv->bshv', o_lat, w_uv).reshape(B, S, MLA_V_WIDTH) @ w_o_mla
    qg = gq.reshape(B, S, GLA_HEADS, GLA_DK) * GLA_DK ** -0.5
    kg = gk.reshape(B, S, GLA_HEADS, GLA_DK)
    vg = gv.reshape(B, S, GLA_HEADS, GLA_DV)
    la = (jax.nn.log_sigmoid((ga @ w_a2 + b_a).astype(jnp.float32)) / GLA_TAU).reshape(B, S, GLA_HEADS, GLA_DK)
    o_g, gla_state = recur(qg, kg, vg, la)
    o_g = rmsnorm(o_g, g_gla) * jax.nn.silu(gg.reshape(B, S, GLA_HEADS, GLA_DV))
    o_gla = o_g.reshape(B, S, GLA_V_WIDTH) @ w_o_gla
    merged = jax.nn.sigmoid(z_mla) * o_mla + jax.nn.sigmoid(z_gla) * o_gla
    h = x + merged @ w_out
    h = h + cross_attend(rmsnorm(h, g_cross), mem_k, mem_v, w_mq, w_mo)
    h = h + moe(rmsnorm(h, g_ffn), w_router, b_router, w_gate_up, b_gate_up, w_down, b_down)
    return h, c_kv, k_rope, gla_state


def setup_inputs(seed: int = 0) -> dict:
    key = jax.random.key(seed)
    ks = jax.random.split(key, 40)
    f32 = jnp.float32

    def nrm(k, shape, scale=1.0):
        return jax.random.normal(k, shape, f32) * scale

    def gain(k, shape):
        return 1.0 + 0.02 * jax.random.normal(k, shape, f32)

    n_pages = PAST_LEN // PAGE_SIZE
    n_used = DEC_BATCH * n_pages
    n_pool = (n_used * 5) // 4
    page_table = jax.random.permutation(ks[0], n_pool)[:n_used].reshape(DEC_BATCH, n_pages).astype(jnp.int32)
    L = DEPTH
    return {
        "x_prompt": nrm(ks[1], (BATCH, SEQ, D_MODEL)),
        "x_sample": nrm(ks[2], (DEC_BATCH, DEC_SEQ, D_MODEL)),
        "cache_kv_latent": nrm(ks[3], (L, n_pool, PAGE_SIZE, KV_LORA)),
        "cache_k_rope": nrm(ks[4], (L, n_pool, PAGE_SIZE, QK_ROPE)),
        "state_gla": nrm(ks[5], (L, DEC_BATCH, GLA_HEADS, GLA_DK, GLA_DV)),
        "cache_mem_k": nrm(ks[6], (L, DEC_BATCH, N_MEM, MEM_HEADS, MEM_HEAD_DIM)),
        "cache_mem_v": nrm(ks[7], (L, DEC_BATCH, N_MEM, MEM_HEADS, MEM_HEAD_DIM)),
        "page_table": page_table,
        "mem_prompt": nrm(ks[8], (BATCH, N_MEM, D_MODEL)),
        "g_attn": gain(ks[9], (L, D_MODEL)),
        "w_in": nrm(ks[10], (L, D_MODEL, D_IN), D_MODEL ** -0.5),
        "g_q": gain(ks[11], (L, Q_LORA)),
        "g_kv": gain(ks[12], (L, KV_LORA)),
        "w_uq": nrm(ks[13], (L, Q_LORA, MLA_HEADS * (QK_NOPE + QK_ROPE)), Q_LORA ** -0.5),
        "w_uk": nrm(ks[14], (L, KV_LORA, MLA_HEADS, QK_NOPE), KV_LORA ** -0.5),
        "w_uv": nrm(ks[15], (L, KV_LORA, MLA_HEADS, V_HEAD), KV_LORA ** -0.5),
        "w_a2": nrm(ks[16], (L, GLA_RANK, GLA_K_WIDTH), GLA_RANK ** -0.5),
        "b_a": nrm(ks[17], (L, GLA_K_WIDTH), 0.01),
        "g_gla": gain(ks[18], (L, GLA_DV)),
        "w_o_mla": nrm(ks[19], (L, MLA_V_WIDTH, D_MODEL), MLA_V_WIDTH ** -0.5),
        "w_o_gla": nrm(ks[20], (L, GLA_V_WIDTH, D_MODEL), GLA_V_WIDTH ** -0.5),
        "w_out": nrm(ks[21], (L, D_MODEL, D_MODEL), D_MODEL ** -0.5),
        "g_mem": gain(ks[22], (L, D_MODEL)),
        "w_mk": nrm(ks[23], (L, D_MODEL, MEM_WIDTH), D_MODEL ** -0.5),
        "w_mv": nrm(ks[24], (L, D_MODEL, MEM_WIDTH), D_MODEL ** -0.5),
        "g_cross": gain(ks[25], (L, D_MODEL)),
        "w_mq": nrm(ks[26], (L, D_MODEL, MEM_WIDTH), D_MODEL ** -0.5),
        "w_mo": nrm(ks[27], (L, MEM_WIDTH, D_MODEL), MEM_WIDTH ** -0.5),
        "g_ffn": gain(ks[28], (L, D_MODEL)),
        "w_router": nrm(ks[29], (L, D_MODEL, N_EXPERTS), D_MODEL ** -0.5),
        "b_router": nrm(ks[30], (L, N_EXPERTS), 0.01),
        "w_gate_up": nrm(ks[31], (L, N_EXPERTS, D_MODEL, 2 * D_FF), D_MODEL ** -0.5),
        "b_gate_up": nrm(ks[32], (L, N_EXPERTS, 2 * D_FF), 0.01),
        "w_down": nrm(ks[33], (L, N_EXPERTS, D_FF, D_MODEL), D_FF ** -0.5),
        "b_down": nrm(ks[34], (L, N_EXPERTS, D_MODEL), 0.01),
        "g_final": gain(ks[35], (D_MODEL,)),
    }


def reference(x_prompt, x_sample, cache_kv_latent, cache_k_rope, state_gla, cache_mem_k, cache_mem_v,
              page_table, mem_prompt, g_attn, w_in, g_q, g_kv, w_uq, w_uk, w_uv, w_a2, b_a, g_gla,
              w_o_mla, w_o_gla, w_out, g_mem, w_mk, w_mv, g_cross, w_mq, w_mo, g_ffn,
              w_router, b_router, w_gate_up, b_gate_up, w_down, b_down, g_final):
    pos_p = jnp.arange(x_prompt.shape[1], dtype=jnp.float32)
    past_len = page_table.shape[1] * cache_kv_latent.shape[2]
    pos_s = past_len + jnp.arange(x_sample.shape[1], dtype=jnp.float32)
    h_p, h_s = x_prompt, x_sample
    lat_p, rope_p, st_p, mk_p, mv_p = [], [], [], [], []
    lat_s, rope_s, st_s = [], [], []
    for l in range(DEPTH):
        lw = (g_attn[l], w_in[l], g_q[l], g_kv[l], w_uq[l], w_uk[l], w_uv[l], w_a2[l], b_a[l], g_gla[l],
              w_o_mla[l], w_o_gla[l], w_out[l], g_cross[l], w_mq[l], w_mo[l],
              g_ffn[l], w_router[l], b_router[l], w_gate_up[l], b_gate_up[l], w_down[l], b_down[l])
        mem_k, mem_v = memory_kv(mem_prompt, g_mem[l], w_mk[l], w_mv[l])
        h_p, ckv, kr, st = decoder_layer(h_p, pos_p, mem_k, mem_v, mla_prompt, gla_prompt, *lw)
        lat_p.append(ckv); rope_p.append(kr); st_p.append(st); mk_p.append(mem_k); mv_p.append(mem_v)
        attend_s = functools.partial(mla_sample, cache_lat=cache_kv_latent[l], cache_rope=cache_k_rope[l], page_table=page_table)
        recur_s = functools.partial(gla_sample, state=state_gla[l])
        h_s, ckv, kr, st = decoder_layer(h_s, pos_s, cache_mem_k[l], cache_mem_v[l], attend_s, recur_s, *lw)
        lat_s.append(ckv); rope_s.append(kr); st_s.append(st)
    y_prompt = rmsnorm(h_p, g_final)
    y_sample = rmsnorm(h_s, g_final)
    return (y_prompt, y_sample,
            jnp.stack(lat_p), jnp.stack(rope_p), jnp.stack(st_p), jnp.stack(mk_p), jnp.stack(mv_p),
            jnp.stack(lat_s), jnp.stack(rope_s), jnp.stack(st_s))
```

```python
import functools

import jax
import jax.numpy as jnp
import numpy as np
from jax import lax
from jax.experimental import pallas as pl
from jax.experimental.pallas import tpu as pltpu

F32 = jnp.float32
BF16 = jnp.bfloat16

D_MODEL = 1024
MLA_HEADS = 8
Q_LORA = 256
KV_LORA = 128
QK_NOPE = 128
QK_ROPE = 64
ROPE_THETA = 10000.0
ATTN_SCALE = (QK_NOPE + QK_ROPE) ** -0.5
GLA_HEADS = 4
GLA_DK = 128
GLA_DV = 256
GLA_RANK = 16
GLA_TAU = 16.0
GLA_CHUNK = 64
N_MEM = 256
MEM_HEADS = 4
MEM_HEAD_DIM = 128
N_EXPERTS = 32
TOP_K = 4
D_FF = 1024
SWIGLU_LIMIT = 7.0
SWIGLU_ALPHA = 1.702
EPS = 1e-6

GLA_K_WIDTH = GLA_HEADS * GLA_DK
GLA_V_WIDTH = GLA_HEADS * GLA_DV
MEM_WIDTH = MEM_HEADS * MEM_HEAD_DIM
IN_SPLITS = (Q_LORA, KV_LORA, QK_ROPE, GLA_K_WIDTH, GLA_K_WIDTH, GLA_V_WIDTH, GLA_V_WIDTH, GLA_RANK, D_MODEL, D_MODEL)

LANES = 128
_PACKED_WIDTHS = (Q_LORA, KV_LORA, LANES, GLA_K_WIDTH, GLA_K_WIDTH, GLA_V_WIDTH, GLA_V_WIDTH, LANES, D_MODEL, D_MODEL)
_OFF = tuple(int(v) for v in np.cumsum((0,) + _PACKED_WIDTHS))
QCAT = 2 * LANES

VMEM_LIMIT = 56 << 20


def _params(n_axes):
    return pltpu.CompilerParams(dimension_semantics=("arbitrary",) * n_axes, vmem_limit_bytes=VMEM_LIMIT)


def _rms(x, g):
    var = jnp.mean(x * x, axis=-1, keepdims=True)
    return x * lax.rsqrt(var + EPS) * g


def _dot(a, b):
    return jnp.dot(a.astype(BF16), b.astype(BF16), preferred_element_type=F32)


def _dot_nt(a, b):
    return lax.dot_general(a.astype(BF16), b.astype(BF16), (((1,), (1,)), ((), ())), preferred_element_type=F32)


def _dot_tn(a, b):
    return lax.dot_general(a.astype(BF16), b.astype(BF16), (((0,), (0,)), ((), ())), preferred_element_type=F32)


def _split3(x):
    hi = x.astype(BF16)
    r1 = x - hi.astype(F32)
    mid = r1.astype(BF16)
    lo = (r1 - mid.astype(F32)).astype(BF16)
    return hi, mid, lo


def _dot_f32(a, b):
    a0, a1, a2 = _split3(a)
    b0, b1, b2 = _split3(b)
    d = lambda u, v: jnp.dot(u, v, preferred_element_type=F32)
    return ((d(a1, b1) + d(a0, b2) + d(a2, b0)) + (d(a0, b1) + d(a1, b0))) + d(a0, b0)


def _full(shape):
    n = len(shape)
    return pl.BlockSpec(shape, lambda *_: (0,) * n)


def _premix_kernel(x_ref, tab_ref, g_attn_ref, w_in_ref, g_q_ref, g_kv_ref, w_uq_ref, wuk_ref, w_a2_ref, b_a_ref,
                   qcat_ref, ckv_ref, krope_ref, kcat_ref, gq_ref, gk_ref, gv_ref, sg_ref, la_ref, szm_ref, szg_ref):
    xn = _rms(x_ref[...], g_attn_ref[...]).astype(BF16)

    def proj(i):
        return jnp.dot(xn, w_in_ref[:, _OFF[i]:_OFF[i + 1]], preferred_element_type=F32)

    tab = tab_ref[...]
    cos, sin_lo, sin_hi = tab[:, :LANES], tab[:, LANES:2 * LANES], tab[:, 2 * LANES:]

    def rope(t):
        return t * cos + pltpu.roll(t, LANES - QK_ROPE // 2, 1) * sin_lo + pltpu.roll(t, QK_ROPE // 2, 1) * sin_hi

    c_q = _rms(proj(0), g_q_ref[...]).astype(BF16)
    ckv = _rms(proj(1), g_kv_ref[...])
    kr = rope(proj(2))
    ckv_ref[...] = ckv
    krope_ref[...] = kr[:, :QK_ROPE]
    kcat_ref[...] = jnp.concatenate([ckv, kr], axis=1).astype(BF16)
    for h in range(MLA_HEADS):
        q_nope = jnp.dot(c_q, w_uq_ref[:, h * LANES:(h + 1) * LANES], preferred_element_type=F32)
        q_lat = _dot(q_nope, wuk_ref[h])
        q_rope = rope(jnp.dot(c_q, w_uq_ref[:, (MLA_HEADS + h) * LANES:(MLA_HEADS + h + 1) * LANES],
                              preferred_element_type=F32))
        qcat_ref[:, h * QCAT:h * QCAT + LANES] = q_lat.astype(qcat_ref.dtype)
        qcat_ref[:, h * QCAT + LANES:(h + 1) * QCAT] = q_rope.astype(qcat_ref.dtype)
    gq_ref[...] = (proj(3) * GLA_DK ** -0.5).astype(gq_ref.dtype)
    gk_ref[...] = proj(4).astype(gk_ref.dtype)
    gv_ref[...] = proj(5).astype(gv_ref.dtype)
    gg = proj(6)
    sg_ref[...] = (gg * jax.nn.sigmoid(gg)).astype(sg_ref.dtype)
    z = _dot(proj(7), w_a2_ref[...]) + b_a_ref[...]
    la_ref[...] = -(jnp.maximum(-z, 0.0) + jnp.log1p(jnp.exp(-jnp.abs(z)))) * (1.0 / GLA_TAU)
    szm_ref[...] = jax.nn.sigmoid(proj(8)).astype(szm_ref.dtype)
    szg_ref[...] = jax.nn.sigmoid(proj(9)).astype(szg_ref.dtype)


def _premix(x, tab, w, tm, gdtype):
    T = x.shape[0]
    n_tab = tab.shape[0] // tm
    row = lambda n: pl.BlockSpec((tm, n), lambda i: (i, 0))
    outs = [(MLA_HEADS * QCAT, BF16), (KV_LORA, F32), (QK_ROPE, F32), (QCAT, BF16), (GLA_K_WIDTH, gdtype),
            (GLA_K_WIDTH, gdtype), (GLA_V_WIDTH, gdtype), (GLA_V_WIDTH, BF16), (GLA_K_WIDTH, F32),
            (D_MODEL, BF16), (D_MODEL, BF16)]
    consts = [w["g_attn"], w["w_in"], w["g_q"], w["g_kv"], w["w_uq"], w["wuk"], w["w_a2"], w["b_a"]]
    return pl.pallas_call(
        _premix_kernel,
        grid=(T // tm,),
        in_specs=[row(D_MODEL), pl.BlockSpec((tm, 3 * LANES), lambda i: (i % n_tab, 0))]
        + [_full(c.shape) for c in consts],
        out_specs=[row(n) for n, _ in outs],
        out_shape=[jax.ShapeDtypeStruct((T, n), d) for n, d in outs],
        compiler_params=_params(1),
        name="premix",
    )(x, tab, *consts)


def _mla_prompt_kernel(q_ref, k_ref, o_ref, *, tq):
    qi = pl.program_id(1)
    row = lax.broadcasted_iota(jnp.int32, (tq, tq), 0)
    col = lax.broadcasted_iota(jnp.int32, (tq, tq), 1)
    for h in range(MLA_HEADS):
        q = q_ref[:, h * QCAT:(h + 1) * QCAT]

        def step(j, carry, masked):
            m, l, acc = carry
            kblk = k_ref[pl.ds(pl.multiple_of(j * tq, tq), tq), :]
            s = _dot_nt(q, kblk) * ATTN_SCALE
            if masked:
                s = jnp.where(col <= row, s, -jnp.inf)
            m_new = jnp.maximum(m, jnp.max(s, axis=-1, keepdims=True))
            alpha = jnp.exp(m - m_new)
            p = jnp.exp(s - m_new)
            l = alpha * l + jnp.sum(p, axis=-1, keepdims=True)
            acc = alpha * acc + _dot(p, kblk[:, :KV_LORA])
            return m_new, l, acc

        init = (jnp.full((tq, 1), -jnp.inf, F32), jnp.zeros((tq, 1), F32), jnp.zeros((tq, KV_LORA), F32))
        carry = lax.fori_loop(0, qi, functools.partial(step, masked=False), init)
        m, l, acc = step(qi, carry, True)
        o_ref[:, h * KV_LORA:(h + 1) * KV_LORA] = (acc / l).astype(o_ref.dtype)


def _mla_prompt(qcat, kcat, tq):
    B, S, _ = qcat.shape
    return pl.pallas_call(
        functools.partial(_mla_prompt_kernel, tq=tq),
        grid=(B, S // tq),
        in_specs=[pl.BlockSpec((None, tq, MLA_HEADS * QCAT), lambda b, i: (b, i, 0)),
                  pl.BlockSpec((None, S, QCAT), lambda b, i: (b, 0, 0))],
        out_specs=pl.BlockSpec((None, tq, MLA_HEADS * KV_LORA), lambda b, i: (b, i, 0)),
        out_shape=jax.ShapeDtypeStruct((B, S, MLA_HEADS * KV_LORA), BF16),
        compiler_params=_params(2),
        name="mla_prompt",
    )(qcat, kcat)


def _mla_sample_kernel(pt_ref, q_ref, knew_ref, *rest, pages):
    lat_refs, rope_refs = rest[:pages], rest[pages:2 * pages]
    o_ref, m_ref, l_ref, acc_ref = rest[2 * pages:]
    c = pl.program_id(1)
    q = q_ref[...]
    q_lat, q_rope = q[:, :KV_LORA], q[:, KV_LORA:KV_LORA + QK_ROPE]

    @pl.when(c == 0)
    def _():
        knew = knew_ref[...].astype(F32)
        m_ref[...] = jnp.sum(q.astype(F32) * knew, axis=-1, keepdims=True) * ATTN_SCALE
        l_ref[...] = jnp.ones_like(l_ref)
        acc_ref[...] = jnp.broadcast_to(knew[:, :KV_LORA], acc_ref.shape)

    lats = [r[...].astype(BF16) for r in lat_refs]
    s = jnp.concatenate([_dot_nt(q_lat, lats[i]) + _dot_nt(q_rope, rope_refs[i][...]) for i in range(pages)],
                        axis=1) * ATTN_SCALE
    m = m_ref[...]
    m_new = jnp.maximum(m, jnp.max(s, axis=-1, keepdims=True))
    alpha = jnp.exp(m - m_new)
    p = jnp.exp(s - m_new)
    l_ref[...] = alpha * l_ref[...] + jnp.sum(p, axis=-1, keepdims=True)
    page = lats[0].shape[0]
    pv = _dot(p[:, :page], lats[0])
    for i in range(1, pages):
        pv = pv + _dot(p[:, i * page:(i + 1) * page], lats[i])
    acc_ref[...] = alpha * acc_ref[...] + pv
    m_ref[...] = m_new

    @pl.when(c == pl.num_programs(1) - 1)
    def _():
        o_ref[...] = (acc_ref[...] / l_ref[...]).astype(o_ref.dtype)


def _mla_sample(page_table, qcat, knew, cache_lat, cache_rope, pages):
    Bd, n_pages = page_table.shape
    page = cache_lat.shape[1]

    def page_spec(width, i):
        return pl.BlockSpec((None, page, width), lambda b, c, pt: (pt[b, c * pages + i], 0, 0))

    grid_spec = pltpu.PrefetchScalarGridSpec(
        num_scalar_prefetch=1,
        grid=(Bd, n_pages // pages),
        in_specs=[pl.BlockSpec((None, MLA_HEADS, QCAT), lambda b, c, pt: (b, 0, 0)),
                  pl.BlockSpec((None, 1, QCAT), lambda b, c, pt: (b, 0, 0))]
        + [page_spec(KV_LORA, i) for i in range(pages)] + [page_spec(QK_ROPE, i) for i in range(pages)],
        out_specs=pl.BlockSpec((None, MLA_HEADS, KV_LORA), lambda b, c, pt: (b, 0, 0)),
        scratch_shapes=[pltpu.VMEM((MLA_HEADS, 1), F32), pltpu.VMEM((MLA_HEADS, 1), F32),
                        pltpu.VMEM((MLA_HEADS, KV_LORA), F32)],
    )
    return pl.pallas_call(
        functools.partial(_mla_sample_kernel, pages=pages),
        grid_spec=grid_spec,
        out_shape=jax.ShapeDtypeStruct((Bd, MLA_HEADS, KV_LORA), BF16),
        compiler_params=_params(2),
        name="mla_sample",
    )(page_table, qcat.reshape(Bd, MLA_HEADS, QCAT), knew.reshape(Bd, 1, QCAT),
      *([cache_lat] * pages), *([cache_rope] * pages))


def _gla_prompt_kernel(q_ref, k_ref, v_ref, la_ref, o_ref, state_ref, st_ref):
    C = GLA_CHUNK
    S = q_ref.shape[0]
    row = lax.broadcasted_iota(jnp.int32, (C, C), 0)
    col = lax.broadcasted_iota(jnp.int32, (C, C), 1)
    causal = row >= col
    tri = causal.astype(BF16)
    st_ref[...] = jnp.zeros_like(st_ref)

    def chunk(c, carry):
        rows = pl.ds(pl.multiple_of(c * C, C), C)
        for h in range(GLA_HEADS):
            kcols = slice(h * GLA_DK, (h + 1) * GLA_DK)
            vcols = slice(h * GLA_DV, (h + 1) * GLA_DV)
            hi, mid, lo = _split3(la_ref[rows, kcols])
            d = lambda u: jnp.dot(tri, u, preferred_element_type=F32)
            b = (d(lo) + d(mid)) + d(hi)
            b_last = b[C - 1:C, :]
            q = q_ref[rows, kcols].astype(F32)
            k = k_ref[rows, kcols].astype(F32)
            v = v_ref[rows, vcols]
            q_in = q * jnp.exp(b)
            k_in = k * jnp.exp(-b)
            k_out = k * jnp.exp(b_last - b)
            a = jnp.where(causal, _dot_nt(q_in, k_in), 0.0)
            st = st_ref[h]
            o = _dot(a, v) + _dot_nt(q_in, st)
            o_ref[rows, vcols] = o.astype(o_ref.dtype)
            st_ref[h] = st * jnp.exp(b_last) + _dot_tn(v, k_out)
        return carry

    lax.fori_loop(0, S // C, chunk, 0)
    for h in range(GLA_HEADS):
        state_ref[h] = st_ref[h].T


def _gla_prompt(q, k, v, la):
    B, S, _ = q.shape
    seq = lambda n: pl.BlockSpec((None, S, n), lambda b: (b, 0, 0))
    return pl.pallas_call(
        _gla_prompt_kernel,
        grid=(B,),
        in_specs=[seq(GLA_K_WIDTH), seq(GLA_K_WIDTH), seq(GLA_V_WIDTH), seq(GLA_K_WIDTH)],
        out_specs=[seq(GLA_V_WIDTH), pl.BlockSpec((None, GLA_HEADS, GLA_DK, GLA_DV), lambda b: (b, 0, 0, 0))],
        out_shape=[jax.ShapeDtypeStruct((B, S, GLA_V_WIDTH), BF16),
                   jax.ShapeDtypeStruct((B, GLA_HEADS, GLA_DK, GLA_DV), F32)],
        scratch_shapes=[pltpu.VMEM((GLA_HEADS, GLA_DV, GLA_DK), F32)],
        compiler_params=_params(1),
        name="gla_prompt",
    )(q, k, v, la)


def _gla_sample_kernel(q_ref, k_ref, v_ref, la_ref, st_ref, o_ref, sto_ref):
    nb = q_ref.shape[0]
    pad = jnp.zeros((GLA_DK - nb, GLA_DK), F32)
    for h in range(GLA_HEADS):
        kcols = slice(h * GLA_DK, (h + 1) * GLA_DK)
        vcols = slice(h * GLA_DV, (h + 1) * GLA_DV)
        col = lambda ref: jnp.concatenate([ref[:, kcols].astype(F32), pad], axis=0).T
        q_t, k_t, decay_t = col(q_ref), col(k_ref), jnp.exp(col(la_ref))
        for i in range(nb):
            new = decay_t[:, i:i + 1] * st_ref[i, h] + k_t[:, i:i + 1] * v_ref[i:i + 1, vcols].astype(F32)
            sto_ref[i, h] = new
            o_ref[i:i + 1, vcols] = jnp.sum(q_t[:, i:i + 1] * new, axis=0, keepdims=True).astype(o_ref.dtype)


def _gla_sample(q, k, v, la, state, nb=8):
    Bd = q.shape[0]
    rows = lambda n: pl.BlockSpec((nb, n), lambda i: (i, 0))
    st_spec = pl.BlockSpec((nb, GLA_HEADS, GLA_DK, GLA_DV), lambda i: (i, 0, 0, 0))
    return pl.pallas_call(
        _gla_sample_kernel,
        grid=(Bd // nb,),
        in_specs=[rows(GLA_K_WIDTH), rows(GLA_K_WIDTH), rows(GLA_V_WIDTH), rows(GLA_K_WIDTH), st_spec],
        out_specs=[rows(GLA_V_WIDTH), st_spec],
        out_shape=[jax.ShapeDtypeStruct((Bd, GLA_V_WIDTH), BF16), jax.ShapeDtypeStruct(state.shape, F32)],
        compiler_params=_params(1),
        name="gla_sample",
    )(q, k, v, la, state)


def _postmix_kernel(x_ref, olat_ref, og_ref, sg_ref, szm_ref, szg_ref, wuv_ref, w_o_mla_ref, g_gla_ref, w_o_gla_ref,
                    w_out_ref, g_cross_ref, w_mq_ref, h_ref, qx_ref):
    ov = jnp.concatenate(
        [_dot(olat_ref[:, h * KV_LORA:(h + 1) * KV_LORA], wuv_ref[h]).astype(BF16) for h in range(MLA_HEADS)], axis=1)
    o_mla = jnp.dot(ov, w_o_mla_ref[...], preferred_element_type=F32)
    g_gla = g_gla_ref[...]
    og = jnp.concatenate(
        [(_rms(og_ref[:, h * GLA_DV:(h + 1) * GLA_DV].astype(F32), g_gla)
          * sg_ref[:, h * GLA_DV:(h + 1) * GLA_DV].astype(F32)).astype(BF16) for h in range(GLA_HEADS)], axis=1)
    o_gla = jnp.dot(og, w_o_gla_ref[...], preferred_element_type=F32)
    merged = szm_ref[...].astype(F32) * o_mla + szg_ref[...].astype(F32) * o_gla
    h1 = x_ref[...] + _dot(merged, w_out_ref[...])
    h_ref[...] = h1
    qx_ref[...] = _dot(_rms(h1, g_cross_ref[...]), w_mq_ref[...]).astype(qx_ref.dtype)


def _postmix(x, olat, og, sg, szm, szg, w, tm):
    T = x.shape[0]
    row = lambda n: pl.BlockSpec((tm, n), lambda i: (i, 0))
    consts = [w["wuv"], w["w_o_mla"], w["g_gla"], w["w_o_gla"], w["w_out"], w["g_cross"], w["w_mq"]]
    return pl.pallas_call(
        _postmix_kernel,
        grid=(T // tm,),
        in_specs=[row(D_MODEL)] * 6 + [_full(c.shape) for c in consts],
        out_specs=[row(D_MODEL), row(MEM_WIDTH)],
        out_shape=[jax.ShapeDtypeStruct((T, D_MODEL), F32), jax.ShapeDtypeStruct((T, MEM_WIDTH), BF16)],
        compiler_params=_params(1),
        name="postmix",
    )(x, olat, og, sg, szm, szg, *consts)


def _memkv_kernel(mem_ref, g_ref, w_ref, k_ref, v_ref):
    kv = _dot(_rms(mem_ref[...], g_ref[...]), w_ref[...])
    k_ref[...] = kv[:, :MEM_WIDTH]
    v_ref[...] = kv[:, MEM_WIDTH:]


def _memkv(mem, g_mem, w_mkv, tm=512):
    T = mem.shape[0]
    row = lambda n: pl.BlockSpec((tm, n), lambda i: (i, 0))
    return pl.pallas_call(
        _memkv_kernel,
        grid=(T // tm,),
        in_specs=[row(D_MODEL), _full(g_mem.shape), _full(w_mkv.shape)],
        out_specs=[row(MEM_WIDTH), row(MEM_WIDTH)],
        out_shape=[jax.ShapeDtypeStruct((T, MEM_WIDTH), F32)] * 2,
        compiler_params=_params(1),
        name="memkv",
    )(mem, g_mem, w_mkv)


def _attend_memory(q, mem_k, mem_v):
    outs = []
    for h in range(MEM_HEADS):
        cols = slice(h * MEM_HEAD_DIM, (h + 1) * MEM_HEAD_DIM)
        s = _dot_nt(q[:, cols], mem_k[:, cols]) * MEM_HEAD_DIM ** -0.5
        p = jnp.exp(s - jnp.max(s, axis=-1, keepdims=True))
        p = p / jnp.sum(p, axis=-1, keepdims=True)
        outs.append(_dot(p, mem_v[:, cols]).astype(BF16))
    return jnp.concatenate(outs, axis=1)


def _route(o, h1, w_mo_ref, g_ffn_ref, w_router_ref, b_router_ref, h_ref, xn_ref, comb_ref):
    h2 = h1 + jnp.dot(o, w_mo_ref[...], preferred_element_type=F32)
    h_ref[...] = h2
    xn = _rms(h2, g_ffn_ref[...])
    xn_ref[...] = xn.astype(xn_ref.dtype)
    logits = _dot_f32(xn, w_router_ref[...]) + b_router_ref[...]
    lane = lax.broadcasted_iota(jnp.int32, logits.shape, 1)
    work = logits
    top = None
    comb = jnp.zeros_like(logits)
    denom = jnp.zeros((logits.shape[0], 1), F32)
    for _ in range(TOP_K):
        best = jnp.max(work, axis=-1, keepdims=True)
        first = jnp.min(jnp.where(work == best, lane, N_EXPERTS), axis=-1, keepdims=True)
        hit = lane == first
        top = best if top is None else top
        e = jnp.exp(best - top)
        comb = comb + jnp.where(hit, e, 0.0)
        denom = denom + e
        work = jnp.where(hit, -jnp.inf, work)
    comb_ref[...] = comb / denom


def _cross_prompt_kernel(qx_ref, h1_ref, mk_ref, mv_ref, w_mo_ref, g_ffn_ref, w_router_ref, b_router_ref,
                         h_ref, xn_ref, comb_ref):
    o = _attend_memory(qx_ref[...], mk_ref[...], mv_ref[...])
    _route(o, h1_ref[...], w_mo_ref, g_ffn_ref, w_router_ref, b_router_ref, h_ref, xn_ref, comb_ref)


def _cross_sample_kernel(qx_ref, h1_ref, mk_ref, mv_ref, w_mo_ref, g_ffn_ref, w_router_ref, b_router_ref,
                         h_ref, xn_ref, comb_ref):
    nb = qx_ref.shape[0]
    rows = []
    for i in range(nb):
        q = jnp.broadcast_to(qx_ref[i:i + 1, :], (8, MEM_WIDTH))
        rows.append(_attend_memory(q, mk_ref[i], mv_ref[i])[:1])
    o = jnp.concatenate(rows, axis=0)
    _route(o, h1_ref[...], w_mo_ref, g_ffn_ref, w_router_ref, b_router_ref, h_ref, xn_ref, comb_ref)


def _cross(qx, h1, mem_k, mem_v, w, tm):
    T = qx.shape[0]
    Bm = mem_k.shape[0]
    per_mem = T // Bm
    consts = [w["w_mo"], w["g_ffn"], w["w_router"], w["b_router"]]
    row = lambda n: pl.BlockSpec((tm, n), lambda i: (i, 0))
    if per_mem == 1:
        kern = _cross_sample_kernel
        mem_spec = pl.BlockSpec((tm, N_MEM, MEM_WIDTH), lambda i: (i, 0, 0))
    else:
        kern = _cross_prompt_kernel
        mem_spec = pl.BlockSpec((None, N_MEM, MEM_WIDTH), lambda i: (i // (per_mem // tm), 0, 0))
    return pl.pallas_call(
        kern,
        grid=(T // tm,),
        in_specs=[row(MEM_WIDTH), row(D_MODEL), mem_spec, mem_spec] + [_full(c.shape) for c in consts],
        out_specs=[row(D_MODEL), row(D_MODEL), row(N_EXPERTS)],
        out_shape=[jax.ShapeDtypeStruct((T, D_MODEL), F32), jax.ShapeDtypeStruct((T, D_MODEL), BF16),
                   jax.ShapeDtypeStruct((T, N_EXPERTS), F32)],
        compiler_params=_params(1),
        name="cross",
    )(qx, h1, mem_k, mem_v, *consts)


def _moe_kernel(xn_ref, comb_ref, h_ref, wgu_ref, bgu_ref, wd_ref, bd_ref, g_final_ref, y_ref, acc_ref):
    e = pl.program_id(1)

    @pl.when(e == 0)
    def _():
        acc_ref[...] = h_ref[...]

    xn = xn_ref[...]
    bgu = bgu_ref[...]
    gate = jnp.dot(xn, wgu_ref[:, :D_FF], preferred_element_type=F32) + bgu[:, :D_FF]
    up = jnp.dot(xn, wgu_ref[:, D_FF:], preferred_element_type=F32) + bgu[:, D_FF:]
    gate = jnp.minimum(gate, SWIGLU_LIMIT)
    up = jnp.clip(up, -SWIGLU_LIMIT, SWIGLU_LIMIT)
    hidden = (up + 1.0) * gate * jax.nn.sigmoid(SWIGLU_ALPHA * gate)
    out = _dot(hidden, wd_ref[...]) + bd_ref[...]
    comb = comb_ref[...]
    lane = lax.broadcasted_iota(jnp.int32, comb.shape, 1)
    weight = jnp.sum(jnp.where(lane == e, comb, 0.0), axis=-1, keepdims=True)
    acc_ref[...] += weight * out

    @pl.when(e == pl.num_programs(1) - 1)
    def _():
        y_ref[...] = _rms(acc_ref[...], g_final_ref[...])


def _moe(xn, comb, h2, w, tm):
    T = xn.shape[0]
    row = lambda n: pl.BlockSpec((tm, n), lambda i, e: (i, 0))
    per_e = lambda a, b: pl.BlockSpec((None, a, b), lambda i, e: (e, 0, 0))
    return pl.pallas_call(
        _moe_kernel,
        grid=(T // tm, N_EXPERTS),
        in_specs=[row(D_MODEL), row(N_EXPERTS), row(D_MODEL), per_e(D_MODEL, 2 * D_FF), per_e(1, 2 * D_FF),
                  per_e(D_FF, D_MODEL), per_e(1, D_MODEL), pl.BlockSpec((1, D_MODEL), lambda i, e: (0, 0))],
        out_specs=row(D_MODEL),
        out_shape=jax.ShapeDtypeStruct((T, D_MODEL), F32),
        scratch_shapes=[pltpu.VMEM((tm, D_MODEL), F32)],
        compiler_params=_params(2),
        name="moe",
    )(xn, comb, h2, w["w_gate_up"], w["b_gate_up"], w["w_down"], w["b_down"], w["g_final"])


def _rope_table(pos):
    half = QK_ROPE // 2
    inv = ROPE_THETA ** (-jnp.arange(half, dtype=F32) * 2.0 / QK_ROPE)
    ang = pos[:, None] * inv[None, :]
    cos, sin, zero = jnp.cos(ang), jnp.sin(ang), jnp.zeros_like(ang)
    return jnp.concatenate([cos, cos, cos, cos, -sin, zero, -sin, zero, zero, sin, zero, sin], axis=1)


def _prepare_weights(g_attn, w_in, g_q, g_kv, w_uq, w_uk, w_uv, w_a2, b_a, g_gla, w_o_mla, w_o_gla, w_out,
                     g_mem, w_mk, w_mv, g_cross, w_mq, w_mo, g_ffn, w_router, b_router, w_gate_up, b_gate_up,
                     w_down, b_down, g_final):
    splits = np.cumsum((0,) + IN_SPLITS)
    parts = [w_in[0][:, splits[i]:splits[i + 1]] for i in range(len(IN_SPLITS))]
    parts = [jnp.pad(p, ((0, 0), (0, wd - p.shape[1]))) for p, wd in zip(parts, _PACKED_WIDTHS)]
    wq = w_uq[0].reshape(Q_LORA, MLA_HEADS, QK_NOPE + QK_ROPE)
    wq_rope = jnp.pad(wq[:, :, QK_NOPE:], ((0, 0), (0, 0), (0, LANES - QK_ROPE)))
    row = lambda v: v.reshape(1, -1)
    return dict(
        g_attn=row(g_attn[0]), w_in=jnp.concatenate(parts, axis=1).astype(BF16), g_q=row(g_q[0]), g_kv=row(g_kv[0]),
        w_uq=jnp.concatenate([wq[:, :, :QK_NOPE].reshape(Q_LORA, -1), wq_rope.reshape(Q_LORA, -1)], axis=1).astype(BF16),
        wuk=jnp.transpose(w_uk[0], (1, 2, 0)).astype(BF16),
        wuv=jnp.transpose(w_uv[0], (1, 0, 2)).astype(BF16),
        w_a2=jnp.pad(w_a2[0], ((0, LANES - GLA_RANK), (0, 0))).astype(BF16), b_a=row(b_a[0]),
        g_gla=row(g_gla[0]), w_o_mla=w_o_mla[0].astype(BF16), w_o_gla=w_o_gla[0].astype(BF16),
        w_out=w_out[0].astype(BF16), g_mem=row(g_mem[0]),
        w_mkv=jnp.concatenate([w_mk[0], w_mv[0]], axis=1).astype(BF16),
        g_cross=row(g_cross[0]), w_mq=w_mq[0].astype(BF16), w_mo=w_mo[0].astype(BF16), g_ffn=row(g_ffn[0]),
        w_router=w_router[0], b_router=row(b_router[0]),
        w_gate_up=w_gate_up[0].astype(BF16), b_gate_up=b_gate_up[0].reshape(N_EXPERTS, 1, 2 * D_FF),
        w_down=w_down[0].astype(BF16), b_down=b_down[0].reshape(N_EXPERTS, 1, D_MODEL), g_final=row(g_final),
    )


def kernel(x_prompt, x_sample, cache_kv_latent, cache_k_rope, state_gla, cache_mem_k, cache_mem_v, page_table, mem_prompt, g_attn, w_in, g_q, g_kv, w_uq, w_uk, w_uv, w_a2, b_a, g_gla, w_o_mla, w_o_gla, w_out, g_mem, w_mk, w_mv, g_cross, w_mq, w_mo, g_ffn, w_router, b_router, w_gate_up, b_gate_up, w_down, b_down, g_final):
    w = _prepare_weights(g_attn, w_in, g_q, g_kv, w_uq, w_uk, w_uv, w_a2, b_a, g_gla, w_o_mla, w_o_gla, w_out,
                         g_mem, w_mk, w_mv, g_cross, w_mq, w_mo, g_ffn, w_router, b_router, w_gate_up, b_gate_up,
                         w_down, b_down, g_final)
    B, S, D = x_prompt.shape
    Bd = x_sample.shape[0]
    past_len = page_table.shape[1] * cache_kv_latent.shape[2]

    tm = min(256, S)
    xp = x_prompt.reshape(B * S, D)
    (qcat, ckv, krope, kcat, gq, gk, gv, sg, la, szm, szg) = _premix(
        xp, _rope_table(jnp.arange(S, dtype=F32)), w, tm, BF16)
    olat = _mla_prompt(qcat.reshape(B, S, -1), kcat.reshape(B, S, -1), min(256, S))
    og, state_p = _gla_prompt(gq.reshape(B, S, -1), gk.reshape(B, S, -1), gv.reshape(B, S, -1), la.reshape(B, S, -1))
    h1, qx = _postmix(xp, olat.reshape(B * S, -1), og.reshape(B * S, -1), sg, szm, szg, w, tm)
    mem_k, mem_v = _memkv(mem_prompt.reshape(B * N_MEM, D), w["g_mem"], w["w_mkv"], min(512, B * N_MEM))
    h2, xn2, comb = _cross(qx, h1, mem_k.reshape(B, N_MEM, MEM_WIDTH), mem_v.reshape(B, N_MEM, MEM_WIDTH), w, tm)
    y_prompt = _moe(xn2, comb, h2, w, min(512, B * S))

    xs = x_sample.reshape(Bd, D)
    (qcat_s, ckv_s, krope_s, kcat_s, gq_s, gk_s, gv_s, sg_s, la_s, szm_s, szg_s) = _premix(
        xs, _rope_table(jnp.full((Bd,), past_len, F32)), w, Bd, F32)
    olat_s = _mla_sample(page_table, qcat_s, kcat_s, cache_kv_latent[0], cache_k_rope[0],
                         min(16, page_table.shape[1]))
    og_s, state_s = _gla_sample(gq_s, gk_s, gv_s, la_s, state_gla[0])
    h1_s, qx_s = _postmix(xs, olat_s.reshape(Bd, -1), og_s, sg_s, szm_s, szg_s, w, Bd)
    h2_s, xn2_s, comb_s = _cross(qx_s, h1_s, cache_mem_k[0].reshape(Bd, N_MEM, MEM_WIDTH),
                                 cache_mem_v[0].reshape(Bd, N_MEM, MEM_WIDTH), w, 8)
    y_sample = _moe(xn2_s, comb_s, h2_s, w, Bd)

    return (y_prompt.reshape(B, S, D), y_sample.reshape(Bd, 1, D),
            ckv.reshape(1, B, S, KV_LORA), krope.reshape(1, B, S, QK_ROPE), state_p[None],
            mem_k.reshape(1, B, N_MEM, MEM_HEADS, MEM_HEAD_DIM), mem_v.reshape(1, B, N_MEM, MEM_HEADS, MEM_HEAD_DIM),
            ckv_s.reshape(1, Bd, 1, KV_LORA), krope_s.reshape(1, Bd, 1, QK_ROPE), state_s[None])
```

```python
import functools

import jax
import jax.numpy as jnp
import numpy as np
from jax import lax
from jax.experimental import pallas as pl
from jax.experimental.pallas import tpu as pltpu

F32 = jnp.float32
BF16 = jnp.bfloat16

D_MODEL = 1024
MLA_HEADS = 8
Q_LORA = 256
KV_LORA = 128
QK_NOPE = 128
QK_ROPE = 64
ROPE_THETA = 10000.0
ATTN_SCALE = (QK_NOPE + QK_ROPE) ** -0.5
GLA_HEADS = 4
GLA_DK = 128
GLA_DV = 256
GLA_RANK = 16
GLA_TAU = 16.0
GLA_CHUNK = 64
N_MEM = 256
MEM_HEADS = 4
MEM_HEAD_DIM = 128
N_EXPERTS = 32
TOP_K = 4
D_FF = 1024
SWIGLU_LIMIT = 7.0
SWIGLU_ALPHA = 1.702
EPS = 1e-6

GLA_K_WIDTH = GLA_HEADS * GLA_DK
GLA_V_WIDTH = GLA_HEADS * GLA_DV
MEM_WIDTH = MEM_HEADS * MEM_HEAD_DIM
IN_SPLITS = (Q_LORA, KV_LORA, QK_ROPE, GLA_K_WIDTH, GLA_K_WIDTH, GLA_V_WIDTH, GLA_V_WIDTH, GLA_RANK, D_MODEL, D_MODEL)

LANES = 128
_PACKED_WIDTHS = (Q_LORA, KV_LORA, LANES, GLA_K_WIDTH, GLA_K_WIDTH, GLA_V_WIDTH, GLA_V_WIDTH, LANES, D_MODEL, D_MODEL)
_OFF = tuple(int(v) for v in np.cumsum((0,) + _PACKED_WIDTHS))
QCAT = 2 * LANES

VMEM_LIMIT = 56 << 20


def _params(n_axes):
    return pltpu.CompilerParams(dimension_semantics=("arbitrary",) * n_axes, vmem_limit_bytes=VMEM_LIMIT)


def _rms(x, g):
    var = jnp.mean(x * x, axis=-1, keepdims=True)
    return x * lax.rsqrt(var + EPS) * g


def _dot(a, b):
    return jnp.dot(a.astype(BF16), b.astype(BF16), preferred_element_type=F32)


def _dot_nt(a, b):
    return lax.dot_general(a.astype(BF16), b.astype(BF16), (((1,), (1,)), ((), ())), preferred_element_type=F32)


def _dot_tn(a, b):
    return lax.dot_general(a.astype(BF16), b.astype(BF16), (((0,), (0,)), ((), ())), preferred_element_type=F32)


def _split3(x):
    hi = x.astype(BF16)
    r1 = x - hi.astype(F32)
    mid = r1.astype(BF16)
    lo = (r1 - mid.astype(F32)).astype(BF16)
    return hi, mid, lo


def _dot_f32_nt(a, b):
    a0, a1, a2 = _split3(a)
    b0, b1, b2 = _split3(b)
    d = lambda u, v: lax.dot_general(u, v, (((1,), (1,)), ((), ())), preferred_element_type=F32)
    return ((d(a1, b1) + d(a0, b2) + d(a2, b0)) + (d(a0, b1) + d(a1, b0))) + d(a0, b0)


def _full(shape):
    n = len(shape)
    return pl.BlockSpec(shape, lambda *_: (0,) * n)


def _premix_kernel(x_ref, tab_ref, g_attn_ref, w_in_ref, g_q_ref, g_kv_ref, w_uq_ref, wuk_ref, w_a2_ref, b_a_ref,
                   qcat_ref, ckv_ref, krope_ref, kcat_ref, kcat_t_ref, gq_ref, gk_ref, gv_ref, sg_ref, la_ref, szm_ref, szg_ref):
    xn = _rms(x_ref[...], g_attn_ref[...]).astype(BF16)

    def proj(i):
        return jnp.dot(xn, w_in_ref[:, _OFF[i]:_OFF[i + 1]], preferred_element_type=F32)

    tab = tab_ref[...]
    cos, sin_lo, sin_hi = tab[:, :LANES], tab[:, LANES:2 * LANES], tab[:, 2 * LANES:]

    def rope(t):
        return t * cos + pltpu.roll(t, LANES - QK_ROPE // 2, 1) * sin_lo + pltpu.roll(t, QK_ROPE // 2, 1) * sin_hi

    c_q = _rms(proj(0), g_q_ref[...]).astype(BF16)
    ckv = _rms(proj(1), g_kv_ref[...])
    kr = rope(proj(2))
    ckv_ref[...] = ckv
    krope_ref[...] = kr[:, :QK_ROPE]
    kcat = jnp.concatenate([ckv, kr], axis=1)
    kcat_ref[...] = kcat.astype(BF16)
    kcat_t_ref[...] = kcat.T.astype(BF16)
    for h in range(MLA_HEADS):
        q_nope = jnp.dot(c_q, w_uq_ref[:, h * LANES:(h + 1) * LANES], preferred_element_type=F32)
        q_lat = _dot(q_nope, wuk_ref[h])
        q_rope = rope(jnp.dot(c_q, w_uq_ref[:, (MLA_HEADS + h) * LANES:(MLA_HEADS + h + 1) * LANES],
                              preferred_element_type=F32))
        qcat_ref[h, :, :LANES] = q_lat.astype(qcat_ref.dtype)
        qcat_ref[h, :, LANES:] = q_rope.astype(qcat_ref.dtype)
    gq_ref[...] = (proj(3) * GLA_DK ** -0.5).astype(gq_ref.dtype)
    gk_ref[...] = proj(4).astype(gk_ref.dtype)
    gv_ref[...] = proj(5).astype(gv_ref.dtype)
    gg = proj(6)
    sg_ref[...] = (gg * jax.nn.sigmoid(gg)).astype(sg_ref.dtype)
    z = _dot(proj(7), w_a2_ref[...]) + b_a_ref[...]
    la_ref[...] = -(jnp.maximum(-z, 0.0) + jnp.log1p(jnp.exp(-jnp.abs(z)))) * (1.0 / GLA_TAU)
    szm_ref[...] = jax.nn.sigmoid(proj(8)).astype(szm_ref.dtype)
    szg_ref[...] = jax.nn.sigmoid(proj(9)).astype(szg_ref.dtype)


def _premix(x, tab, w, tm, gdtype):
    T = x.shape[0]
    n_tab = tab.shape[0] // tm
    row = lambda n: pl.BlockSpec((tm, n), lambda i: (i, 0))
    outs = [(KV_LORA, F32), (QK_ROPE, F32), (QCAT, BF16), None, (GLA_K_WIDTH, gdtype),
            (GLA_K_WIDTH, gdtype), (GLA_V_WIDTH, gdtype), (GLA_V_WIDTH, BF16), (GLA_K_WIDTH, F32),
            (D_MODEL, BF16), (D_MODEL, BF16)]
    out_specs = [row(o[0]) if o else pl.BlockSpec((None, QCAT, tm), lambda i: (i, 0, 0)) for o in outs]
    out_shape = [jax.ShapeDtypeStruct((T, o[0]) if o else (T // tm, QCAT, tm), o[1] if o else BF16) for o in outs]
    consts = [w["g_attn"], w["w_in"], w["g_q"], w["g_kv"], w["w_uq"], w["wuk"], w["w_a2"], w["b_a"]]
    return pl.pallas_call(
        _premix_kernel,
        grid=(T // tm,),
        in_specs=[row(D_MODEL), pl.BlockSpec((tm, 3 * LANES), lambda i: (i % n_tab, 0))]
        + [_full(c.shape) for c in consts],
        out_specs=[pl.BlockSpec((MLA_HEADS, tm, QCAT), lambda i: (0, i, 0))] + out_specs,
        out_shape=[jax.ShapeDtypeStruct((MLA_HEADS, T, QCAT), BF16)] + out_shape,
        compiler_params=_params(1),
        name="premix",
    )(x, tab, *consts)


def _mla_prompt_kernel(q_ref, k_ref, kt_ref, o_ref, m_ref, l_ref, acc_ref):
    tq = q_ref.shape[1]
    qi = pl.program_id(1)
    q = q_ref[...].reshape(MLA_HEADS * tq, QCAT)
    row = lax.broadcasted_iota(jnp.int32, (MLA_HEADS, tq, tq), 1).reshape(MLA_HEADS * tq, tq)
    col = lax.broadcasted_iota(jnp.int32, (MLA_HEADS * tq, tq), 1)
    m_ref[...] = jnp.full_like(m_ref, -jnp.inf)
    l_ref[...] = jnp.zeros_like(l_ref)
    acc_ref[...] = jnp.zeros_like(acc_ref)
    c = ATTN_SCALE * np.log2(np.e)

    def block(j, masked):
        keys = pl.ds(pl.multiple_of(j * tq, tq), tq)
        s = jnp.dot(q, kt_ref[j], preferred_element_type=F32)
        if masked:
            s = jnp.where(col <= row, s, -jnp.inf)
        m = m_ref[...]
        m_new = jnp.maximum(m, jnp.max(s, axis=-1, keepdims=True))
        alpha = jnp.exp2((m - m_new) * c)
        p = jnp.exp2((s - jnp.concatenate([m_new] * (tq // LANES), axis=1)) * c)
        p_lanes = p[:, :LANES]
        for t in range(1, tq // LANES):
            p_lanes = p_lanes + p[:, t * LANES:(t + 1) * LANES]
        l_ref[...] = alpha * l_ref[...] + p_lanes
        acc_ref[...] = alpha * acc_ref[...] + _dot(p, k_ref[keys, :KV_LORA])
        m_ref[...] = m_new

    def body(j, carry):
        block(j, False)
        return carry

    lax.fori_loop(0, qi, body, 0)
    block(qi, True)
    o = acc_ref[...] / jnp.sum(l_ref[...], axis=-1, keepdims=True)
    for h in range(MLA_HEADS):
        o_ref[:, h * KV_LORA:(h + 1) * KV_LORA] = o[h * tq:(h + 1) * tq].astype(o_ref.dtype)


def _mla_prompt(qcat, kcat, kcat_t, tq):
    _, B, S, _ = qcat.shape
    rows = MLA_HEADS * tq
    return pl.pallas_call(
        _mla_prompt_kernel,
        grid=(B, S // tq),
        in_specs=[pl.BlockSpec((MLA_HEADS, None, tq, QCAT), lambda b, i: (0, b, i, 0)),
                  pl.BlockSpec((None, S, QCAT), lambda b, i: (b, 0, 0)),
                  pl.BlockSpec((S // tq, QCAT, tq), lambda b, i: (b, 0, 0))],
        out_specs=pl.BlockSpec((None, tq, MLA_HEADS * KV_LORA), lambda b, i: (b, i, 0)),
        out_shape=jax.ShapeDtypeStruct((B, S, MLA_HEADS * KV_LORA), BF16),
        scratch_shapes=[pltpu.VMEM((rows, LANES), F32), pltpu.VMEM((rows, LANES), F32), pltpu.VMEM((rows, KV_LORA), F32)],
        compiler_params=_params(2),
        name="mla_prompt",
    )(qcat, kcat, kcat_t)


def _mla_sample_kernel(pt_ref, q_ref, knew_ref, lat_hbm, rope_hbm, o_ref, lat_buf, rope_buf, lat_bf, sems):
    b = pl.program_id(0)
    n_pages, page = lat_buf.shape[1], lat_buf.shape[2]

    def page_copies(sample, slot, i):
        pg = pt_ref[sample, i]
        return (pltpu.make_async_copy(lat_hbm.at[pg], lat_buf.at[slot, i], sems.at[0, slot]),
                pltpu.make_async_copy(rope_hbm.at[pg], rope_buf.at[slot, :, pl.ds(pl.multiple_of(i * page, page), page)],
                                      sems.at[1, slot]))

    def for_pages(sample, slot, act):
        def body(i, carry):
            for cp in page_copies(sample, slot, i):
                act(cp)
            return carry
        lax.fori_loop(0, n_pages, body, 0)

    slot = b % 2

    @pl.when(b == 0)
    def _():
        for_pages(0, 0, lambda cp: cp.start())

    @pl.when(b + 1 < pl.num_programs(0))
    def _():
        for_pages(b + 1, 1 - slot, lambda cp: cp.start())

    for_pages(b, slot, lambda cp: cp.wait())

    lat_bf[...] = lat_buf[slot].reshape(n_pages * page, KV_LORA).astype(BF16)
    q = q_ref[...].reshape(MLA_HEADS, QCAT)
    knew = knew_ref[...].astype(F32)
    s = (_dot_nt(q[:, :KV_LORA], lat_bf[...]) + _dot(q[:, KV_LORA:KV_LORA + QK_ROPE], rope_buf[slot])) * ATTN_SCALE
    s_new = jnp.sum(q.astype(F32) * knew, axis=-1, keepdims=True) * ATTN_SCALE
    m = jnp.maximum(jnp.max(s, axis=-1, keepdims=True), s_new)
    p = jnp.exp(s - m)
    p_new = jnp.exp(s_new - m)
    denom = jnp.sum(p, axis=-1, keepdims=True) + p_new
    o = _dot(p, lat_bf[...]) + p_new * knew[:, :KV_LORA]
    o_ref[...] = (o / denom).astype(o_ref.dtype)


def _mla_sample(page_table, qcat, knew, cache_lat, cache_rope_t):
    Bd, n_pages = page_table.shape
    page = cache_lat.shape[1]
    grid_spec = pltpu.PrefetchScalarGridSpec(
        num_scalar_prefetch=1,
        grid=(Bd,),
        in_specs=[pl.BlockSpec((MLA_HEADS, None, 1, QCAT), lambda b, pt: (0, b, 0, 0)),
                  pl.BlockSpec((None, 1, QCAT), lambda b, pt: (b, 0, 0)),
                  pl.BlockSpec(memory_space=pl.ANY), pl.BlockSpec(memory_space=pl.ANY)],
        out_specs=pl.BlockSpec((None, MLA_HEADS, KV_LORA), lambda b, pt: (b, 0, 0)),
        scratch_shapes=[pltpu.VMEM((2, n_pages, page, KV_LORA), F32), pltpu.VMEM((2, QK_ROPE, n_pages * page), F32),
                        pltpu.VMEM((n_pages * page, KV_LORA), BF16), pltpu.SemaphoreType.DMA((2, 2))],
    )
    return pl.pallas_call(
        _mla_sample_kernel,
        grid_spec=grid_spec,
        out_shape=jax.ShapeDtypeStruct((Bd, MLA_HEADS, KV_LORA), BF16),
        compiler_params=_params(1),
        name="mla_sample",
    )(page_table, qcat.reshape(MLA_HEADS, Bd, 1, QCAT), knew.reshape(Bd, 1, QCAT), cache_lat, cache_rope_t)


def _gla_prompt_kernel(q_ref, k_ref, v_ref, la_ref, o_ref, state_ref, st_ref):
    C = GLA_CHUNK
    S = q_ref.shape[0]
    row = lax.broadcasted_iota(jnp.int32, (C, C), 0)
    col = lax.broadcasted_iota(jnp.int32, (C, C), 1)
    causal = row >= col
    tri = causal.astype(BF16)
    st_ref[...] = jnp.zeros_like(st_ref)

    def chunk(c, carry):
        rows = pl.ds(pl.multiple_of(c * C, C), C)
        for h in range(GLA_HEADS):
            kcols = slice(h * GLA_DK, (h + 1) * GLA_DK)
            vcols = slice(h * GLA_DV, (h + 1) * GLA_DV)
            hi, mid, lo = _split3(la_ref[rows, kcols])
            d = lambda u: jnp.dot(tri, u, preferred_element_type=F32)
            b = (d(lo) + d(mid)) + d(hi)
            b_last = b[C - 1:C, :]
            q = q_ref[rows, kcols].astype(F32)
            k = k_ref[rows, kcols].astype(F32)
            v = v_ref[rows, vcols]
            q_in = q * jnp.exp(b)
            k_in = k * jnp.exp(-b)
            k_out = k * jnp.exp(b_last - b)
            a = jnp.where(causal, _dot_nt(q_in, k_in), 0.0)
            st = st_ref[h]
            o = _dot(a, v) + _dot_nt(q_in, st)
            o_ref[rows, vcols] = o.astype(o_ref.dtype)
            st_ref[h] = st * jnp.exp(b_last) + _dot_tn(v, k_out)
        return carry

    lax.fori_loop(0, S // C, chunk, 0)
    for h in range(GLA_HEADS):
        state_ref[h] = st_ref[h].T


def _gla_prompt(q, k, v, la):
    B, S, _ = q.shape
    seq = lambda n: pl.BlockSpec((None, S, n), lambda b: (b, 0, 0))
    return pl.pallas_call(
        _gla_prompt_kernel,
        grid=(B,),
        in_specs=[seq(GLA_K_WIDTH), seq(GLA_K_WIDTH), seq(GLA_V_WIDTH), seq(GLA_K_WIDTH)],
        out_specs=[seq(GLA_V_WIDTH), pl.BlockSpec((None, GLA_HEADS, GLA_DK, GLA_DV), lambda b: (b, 0, 0, 0))],
        out_shape=[jax.ShapeDtypeStruct((B, S, GLA_V_WIDTH), BF16),
                   jax.ShapeDtypeStruct((B, GLA_HEADS, GLA_DK, GLA_DV), F32)],
        scratch_shapes=[pltpu.VMEM((GLA_HEADS, GLA_DV, GLA_DK), F32)],
        compiler_params=_params(1),
        name="gla_prompt",
    )(q, k, v, la)


def _gla_sample_kernel(q_ref, k_ref, v_ref, la_ref, st_ref, o_ref, sto_ref):
    nb = q_ref.shape[0]
    pad = jnp.zeros((GLA_DK - nb, GLA_DK), F32)
    for h in range(GLA_HEADS):
        kcols = slice(h * GLA_DK, (h + 1) * GLA_DK)
        vcols = slice(h * GLA_DV, (h + 1) * GLA_DV)
        col = lambda ref: jnp.concatenate([ref[:, kcols].astype(F32), pad], axis=0).T
        q_t, k_t, decay_t = col(q_ref), col(k_ref), jnp.exp(col(la_ref))
        for i in range(nb):
            new = decay_t[:, i:i + 1] * st_ref[i, h] + k_t[:, i:i + 1] * v_ref[i:i + 1, vcols].astype(F32)
            sto_ref[i, h] = new
            o_ref[i:i + 1, vcols] = jnp.sum(q_t[:, i:i + 1] * new, axis=0, keepdims=True).astype(o_ref.dtype)


def _gla_sample(q, k, v, la, state, nb=8):
    Bd = q.shape[0]
    rows = lambda n: pl.BlockSpec((nb, n), lambda i: (i, 0))
    st_spec = pl.BlockSpec((nb, GLA_HEADS, GLA_DK, GLA_DV), lambda i: (i, 0, 0, 0))
    return pl.pallas_call(
        _gla_sample_kernel,
        grid=(Bd // nb,),
        in_specs=[rows(GLA_K_WIDTH), rows(GLA_K_WIDTH), rows(GLA_V_WIDTH), rows(GLA_K_WIDTH), st_spec],
        out_specs=[rows(GLA_V_WIDTH), st_spec],
        out_shape=[jax.ShapeDtypeStruct((Bd, GLA_V_WIDTH), BF16), jax.ShapeDtypeStruct(state.shape, F32)],
        compiler_params=_params(1),
        name="gla_sample",
    )(q, k, v, la, state)


def _postmix_kernel(x_ref, olat_ref, og_ref, sg_ref, szm_ref, szg_ref, wuv_ref, w_o_mla_ref, g_gla_ref, w_o_gla_ref,
                    w_out_ref, g_cross_ref, w_mq_ref, h_ref, qx_ref):
    ov = jnp.concatenate(
        [_dot(olat_ref[:, h * KV_LORA:(h + 1) * KV_LORA], wuv_ref[h]).astype(BF16) for h in range(MLA_HEADS)], axis=1)
    o_mla = jnp.dot(ov, w_o_mla_ref[...], preferred_element_type=F32)
    g_gla = g_gla_ref[...]
    og = jnp.concatenate(
        [(_rms(og_ref[:, h * GLA_DV:(h + 1) * GLA_DV].astype(F32), g_gla)
          * sg_ref[:, h * GLA_DV:(h + 1) * GLA_DV].astype(F32)).astype(BF16) for h in range(GLA_HEADS)], axis=1)
    o_gla = jnp.dot(og, w_o_gla_ref[...], preferred_element_type=F32)
    merged = szm_ref[...].astype(F32) * o_mla + szg_ref[...].astype(F32) * o_gla
    h1 = x_ref[...] + _dot(merged, w_out_ref[...])
    h_ref[...] = h1
    qx_ref[...] = _dot(_rms(h1, g_cross_ref[...]), w_mq_ref[...]).astype(qx_ref.dtype)


def _postmix(x, olat, og, sg, szm, szg, w, tm):
    T = x.shape[0]
    row = lambda n: pl.BlockSpec((tm, n), lambda i: (i, 0))
    consts = [w["wuv"], w["w_o_mla"], w["g_gla"], w["w_o_gla"], w["w_out"], w["g_cross"], w["w_mq"]]
    return pl.pallas_call(
        _postmix_kernel,
        grid=(T // tm,),
        in_specs=[row(D_MODEL)] * 6 + [_full(c.shape) for c in consts],
        out_specs=[row(D_MODEL), row(MEM_WIDTH)],
        out_shape=[jax.ShapeDtypeStruct((T, D_MODEL), F32), jax.ShapeDtypeStruct((T, MEM_WIDTH), BF16)],
        compiler_params=_params(1),
        name="postmix",
    )(x, olat, og, sg, szm, szg, *consts)


def _memkv_kernel(mem_ref, g_ref, w_ref, k_ref, v_ref):
    kv = _dot(_rms(mem_ref[...], g_ref[...]), w_ref[...])
    k_ref[...] = kv[:, :MEM_WIDTH]
    v_ref[...] = kv[:, MEM_WIDTH:]


def _memkv(mem, g_mem, w_mkv, tm=512):
    T = mem.shape[0]
    row = lambda n: pl.BlockSpec((tm, n), lambda i: (i, 0))
    return pl.pallas_call(
        _memkv_kernel,
        grid=(T // tm,),
        in_specs=[row(D_MODEL), _full(g_mem.shape), _full(w_mkv.shape)],
        out_specs=[row(MEM_WIDTH), row(MEM_WIDTH)],
        out_shape=[jax.ShapeDtypeStruct((T, MEM_WIDTH), F32)] * 2,
        compiler_params=_params(1),
        name="memkv",
    )(mem, g_mem, w_mkv)


def _attend_memory(q, mem_k, mem_v):
    outs = []
    for h in range(MEM_HEADS):
        cols = slice(h * MEM_HEAD_DIM, (h + 1) * MEM_HEAD_DIM)
        s = _dot_nt(q[:, cols], mem_k[:, cols]) * MEM_HEAD_DIM ** -0.5
        p = jnp.exp(s - jnp.max(s, axis=-1, keepdims=True))
        p = p / jnp.sum(p, axis=-1, keepdims=True)
        outs.append(_dot(p, mem_v[:, cols]).astype(BF16))
    return jnp.concatenate(outs, axis=1)


def _route(o, h1, w_mo_ref, g_ffn_ref, w_router_ref, b_router_ref, h_ref, xn_ref, idx_ref, rank_ref, wt_ref, cnt_ref):
    h2 = h1 + jnp.dot(o, w_mo_ref[...], preferred_element_type=F32)
    h_ref[...] = h2
    xn = _rms(h2, g_ffn_ref[...])
    xn_ref[...] = xn.astype(xn_ref.dtype)
    tb = xn.shape[0]
    logits = _dot_f32_nt(w_router_ref[...], xn) + b_router_ref[...]
    expert = lax.broadcasted_iota(jnp.int32, logits.shape, 0)
    work = logits
    hits, firsts, exps = [], [], []
    top = None
    for _ in range(TOP_K):
        best = jnp.max(work, axis=0, keepdims=True)
        first = jnp.min(jnp.where(work == best, expert, N_EXPERTS), axis=0, keepdims=True)
        hit = expert == first
        top = best if top is None else top
        hits.append(hit)
        firsts.append(first)
        exps.append(jnp.exp(best - top))
        work = jnp.where(hit, -jnp.inf, work)
    denom = (exps[0] + exps[1]) + (exps[2] + exps[3])
    chosen = jnp.zeros(logits.shape, F32)
    for hit in hits:
        chosen = chosen + jnp.where(hit, 1.0, 0.0)
    before = (lax.broadcasted_iota(jnp.int32, (tb, tb), 0) < lax.broadcasted_iota(jnp.int32, (tb, tb), 1)).astype(BF16)
    rank = jnp.dot(chosen.astype(BF16), before, preferred_element_type=F32)
    idx_ref[...] = jnp.concatenate(firsts, axis=0)
    rank_ref[...] = jnp.concatenate(
        [jnp.sum(jnp.where(hit, rank, 0.0), axis=0, keepdims=True) for hit in hits], axis=0).astype(jnp.int32)
    wt_ref[...] = jnp.concatenate([e / denom for e in exps], axis=0)
    cnt_ref[...] = jnp.sum(chosen, axis=1, keepdims=True).astype(jnp.int32)


def _cross_prompt_kernel(qx_ref, h1_ref, mk_ref, mv_ref, *rest):
    o = _attend_memory(qx_ref[...], mk_ref[...], mv_ref[...])
    _route(o, h1_ref[...], *rest)


def _cross_sample_kernel(qx_ref, mk_ref, mv_ref, o_ref):
    for i in range(qx_ref.shape[0]):
        q = jnp.broadcast_to(qx_ref[i:i + 1, :], (8, MEM_WIDTH))
        o_ref[i:i + 1, :] = _attend_memory(q, mk_ref[i], mv_ref[i])[:1]


def _route_kernel(o_ref, h1_ref, *rest):
    _route(o_ref[...], h1_ref[...], *rest)


def _route_specs(T, tb, w):
    consts = [w["w_mo"], w["g_ffn"], w["w_router_t"], w["b_router"]]
    row = lambda n: pl.BlockSpec((tb, n), lambda i: (i, 0))
    per_choice = pl.BlockSpec((TOP_K, tb), lambda i: (0, i))
    out_specs = [row(D_MODEL), row(D_MODEL), per_choice, per_choice, per_choice,
                 pl.BlockSpec((None, N_EXPERTS, 1), lambda i: (i, 0, 0))]
    out_shape = [jax.ShapeDtypeStruct((T, D_MODEL), F32), jax.ShapeDtypeStruct((T, D_MODEL), BF16),
                 jax.ShapeDtypeStruct((TOP_K, T), jnp.int32), jax.ShapeDtypeStruct((TOP_K, T), jnp.int32),
                 jax.ShapeDtypeStruct((TOP_K, T), F32), jax.ShapeDtypeStruct((T // tb, N_EXPERTS, 1), jnp.int32)]
    return consts, out_specs, out_shape


def _cross_prompt(qx, h1, mem_k, mem_v, w, tb):
    T = qx.shape[0]
    blocks_per_mem = T // mem_k.shape[0] // tb
    consts, out_specs, out_shape = _route_specs(T, tb, w)
    row = lambda n: pl.BlockSpec((tb, n), lambda i: (i, 0))
    mem_spec = pl.BlockSpec((None, N_MEM, MEM_WIDTH), lambda i: (i // blocks_per_mem, 0, 0))
    return pl.pallas_call(
        _cross_prompt_kernel,
        grid=(T // tb,),
        in_specs=[row(MEM_WIDTH), row(D_MODEL), mem_spec, mem_spec] + [_full(c.shape) for c in consts],
        out_specs=out_specs,
        out_shape=out_shape,
        compiler_params=_params(1),
        name="cross_prompt",
    )(qx, h1, mem_k, mem_v, *consts)


def _cross_sample(qx, h1, mem_k, mem_v, w, nb=8):
    T = qx.shape[0]
    mem_spec = pl.BlockSpec((nb, N_MEM, MEM_WIDTH), lambda i: (i, 0, 0))
    rows = pl.BlockSpec((nb, MEM_WIDTH), lambda i: (i, 0))
    o = pl.pallas_call(
        _cross_sample_kernel,
        grid=(T // nb,),
        in_specs=[rows, mem_spec, mem_spec],
        out_specs=rows,
        out_shape=jax.ShapeDtypeStruct((T, MEM_WIDTH), BF16),
        compiler_params=_params(1),
        name="cross_sample",
    )(qx, mem_k, mem_v)
    consts, out_specs, out_shape = _route_specs(T, T, w)
    return pl.pallas_call(
        _route_kernel,
        grid=(1,),
        in_specs=[_full(o.shape), _full(h1.shape)] + [_full(c.shape) for c in consts],
        out_specs=out_specs,
        out_shape=out_shape,
        compiler_params=_params(1),
        name="route_sample",
    )(o, h1, *consts)


SEG_ALIGN = 16
SEL_CHUNK = 256
MOE_BLOCK = 512


def _block_rows(tb):
    worst = TOP_K * tb + N_EXPERTS * (SEG_ALIGN - 1)
    return -(-worst // SEL_CHUNK) * SEL_CHUNK


def _seg_sizes(limit):
    sizes, b = [], SEG_ALIGN
    while b <= limit:
        sizes.append(b)
        b *= 2
    return sizes[::-1]


def _for_segments(n, sizes, make_copy, act):
    for size in sizes:
        @pl.when((n & size) != 0)
        def _():
            act(make_copy(pl.multiple_of(n & (-2 * size), SEG_ALIGN), size))


def _slot_rows(idx, rank, loc_ref, j):
    pos = rank
    for e in range(N_EXPERTS):
        pos = pos + jnp.where(idx == e, loc_ref[j, e], 0)
    return pos


def _dispatch_kernel(loc_ref, pad_ref, goff_ref, gap_off_ref, gap_len_ref, x_ref, idx_ref, rank_ref, xs_hbm,
                     buf, sem, *, tm):
    j = pl.program_id(0)
    tb = x_ref.shape[0]
    r_blk = buf.shape[0]
    pos = _slot_rows(idx_ref[...], rank_ref[...], loc_ref, j)
    x = x_ref[...]
    for c in range(r_blk // SEL_CHUNK):
        r = lax.broadcasted_iota(jnp.int32, (SEL_CHUNK, tb), 0) + c * SEL_CHUNK
        sel = jnp.where(r == pos[0:1], 1.0, jnp.where(r == pos[1:2], 1.0, jnp.where(
            r == pos[2:3], 1.0, jnp.where(r == pos[3:4], 1.0, 0.0))))
        buf[c * SEL_CHUNK:(c + 1) * SEL_CHUNK, :] = jnp.dot(
            sel.astype(BF16), x, preferred_element_type=F32).astype(BF16)

    sizes = _seg_sizes(tb)

    def for_experts(act):
        def body(e, carry):
            src, dst = loc_ref[j, e], goff_ref[j, e]
            _for_segments(pad_ref[j, e], sizes, lambda off, size: pltpu.make_async_copy(
                buf.at[pl.ds(pl.multiple_of(src + off, SEG_ALIGN), size)],
                xs_hbm.at[pl.ds(pl.multiple_of(dst + off, SEG_ALIGN), size)], sem), act)
            return carry
        lax.fori_loop(0, N_EXPERTS, body, 0)

    for_experts(lambda cp: cp.start())
    for_experts(lambda cp: cp.wait())

    @pl.when(j == pl.num_programs(0) - 1)
    def _():
        buf[0:tm, :] = jnp.zeros((tm, D_MODEL), BF16)
        gap_sizes = _seg_sizes(tm - SEG_ALIGN)

        def for_gaps(act):
            def body(e, carry):
                dst = gap_off_ref[e]
                _for_segments(gap_len_ref[e], gap_sizes, lambda off, size: pltpu.make_async_copy(
                    buf.at[pl.ds(0, size)], xs_hbm.at[pl.ds(pl.multiple_of(dst + off, SEG_ALIGN), size)], sem), act)
                return carry
            lax.fori_loop(0, N_EXPERTS, body, 0)

        for_gaps(lambda cp: cp.start())
        for_gaps(lambda cp: cp.wait())


def _ffn_kernel(tile_expert_ref, n_valid_ref, x_ref, wgu_ref, bgu_ref, wd_ref, bd_ref, o_ref):
    @pl.when(pl.program_id(0) < n_valid_ref[0])
    def _():
        x = x_ref[...]
        bgu = bgu_ref[...]
        gate = jnp.dot(x, wgu_ref[:, :D_FF], preferred_element_type=F32) + bgu[:, :D_FF]
        up = jnp.dot(x, wgu_ref[:, D_FF:], preferred_element_type=F32) + bgu[:, D_FF:]
        gate = jnp.minimum(gate, SWIGLU_LIMIT)
        up = jnp.clip(up, -SWIGLU_LIMIT, SWIGLU_LIMIT)
        hidden = (up + 1.0) * gate * jax.nn.sigmoid(SWIGLU_ALPHA * gate)
        o_ref[...] = (_dot(hidden, wd_ref[...]) + bd_ref[...]).astype(o_ref.dtype)


def _combine_kernel(loc_ref, pad_ref, goff_ref, idx_ref, rank_ref, wt_ref, h_ref, g_final_ref, o_hbm, y_ref,
                    buf, sel_ref, sem):
    j = pl.program_id(0)
    tb = h_ref.shape[0]
    r_blk = buf.shape[0]

    @pl.when(j == 0)
    def _():
        buf[...] = jnp.zeros_like(buf)

    sizes = _seg_sizes(tb)

    def for_experts(act):
        def body(e, carry):
            dst, src = loc_ref[j, e], goff_ref[j, e]
            _for_segments(pad_ref[j, e], sizes, lambda off, size: pltpu.make_async_copy(
                o_hbm.at[pl.ds(pl.multiple_of(src + off, SEG_ALIGN), size)],
                buf.at[pl.ds(pl.multiple_of(dst + off, SEG_ALIGN), size)], sem), act)
            return carry
        lax.fori_loop(0, N_EXPERTS, body, 0)

    for_experts(lambda cp: cp.start())
    pos = _slot_rows(idx_ref[...], rank_ref[...], loc_ref, j)
    stacked = jnp.concatenate([pos.astype(F32), wt_ref[...], jnp.zeros((LANES - 2 * TOP_K, tb), F32)], axis=0).T
    for c in range(r_blk // SEL_CHUNK):
        r = (lax.broadcasted_iota(jnp.int32, (tb, SEL_CHUNK), 1) + c * SEL_CHUNK).astype(F32)
        sel = jnp.zeros((tb, SEL_CHUNK), F32)
        for k in range(TOP_K):
            sel = jnp.where(r == stacked[:, k:k + 1], stacked[:, TOP_K + k:TOP_K + k + 1], sel)
        sel_ref[:, c * SEL_CHUNK:(c + 1) * SEL_CHUNK] = sel.astype(BF16)
    for_experts(lambda cp: cp.wait())
    moe = jnp.dot(sel_ref[...], buf[...], preferred_element_type=F32)
    y_ref[...] = _rms(h_ref[...] + moe, g_final_ref[...])


def _moe(xn, idx, rank, wt, cnt, h2, w, tb, tm):
    T = xn.shape[0]
    nb = T // tb
    r_blk = _block_rows(tb)
    i32 = jnp.int32
    pad = (cnt + (SEG_ALIGN - 1)) // SEG_ALIGN * SEG_ALIGN
    loc = jnp.cumsum(pad, axis=1) - pad
    seg = jnp.sum(pad, axis=0)
    region = (seg + (tm - 1)) // tm * tm
    region_end = jnp.cumsum(region)
    region_start = region_end - region
    goff = region_start[None, :] + jnp.cumsum(pad, axis=0) - pad
    n_tiles = -(-(nb * (TOP_K * tb + N_EXPERTS * (SEG_ALIGN - 1)) + N_EXPERTS * (tm - 1)) // tm)
    n_valid = (region_end[-1] // tm).astype(i32).reshape(1)
    tile = jnp.minimum(jnp.arange(n_tiles, dtype=i32), n_valid - 1)
    tile_expert = jnp.minimum(jnp.searchsorted(region_end, tile * tm, side="right"), N_EXPERTS - 1).astype(i32)
    plan = [a.astype(i32) for a in (loc, pad, goff)]
    gaps = [(region_start + seg).astype(i32), (region - seg).astype(i32)]

    choice = pl.BlockSpec((TOP_K, tb), lambda j, *_: (0, j))
    xs = pl.pallas_call(
        functools.partial(_dispatch_kernel, tm=tm),
        grid_spec=pltpu.PrefetchScalarGridSpec(
            num_scalar_prefetch=5, grid=(nb,),
            in_specs=[pl.BlockSpec((tb, D_MODEL), lambda j, *_: (j, 0)), choice, choice],
            out_specs=pl.BlockSpec(memory_space=pl.ANY),
            scratch_shapes=[pltpu.VMEM((max(r_blk, tm), D_MODEL), BF16), pltpu.SemaphoreType.DMA(())]),
        out_shape=jax.ShapeDtypeStruct((n_tiles * tm, D_MODEL), BF16),
        compiler_params=_params(1),
        name="moe_dispatch",
    )(*plan, *gaps, xn, idx, rank)

    rows = pl.BlockSpec((tm, D_MODEL), lambda t, te, nv: (jnp.minimum(t, nv[0] - 1), 0))
    per_e = lambda a, b: pl.BlockSpec((None, a, b), lambda t, te, nv: (te[t], 0, 0))
    out = pl.pallas_call(
        _ffn_kernel,
        grid_spec=pltpu.PrefetchScalarGridSpec(
            num_scalar_prefetch=2, grid=(n_tiles,),
            in_specs=[rows, per_e(D_MODEL, 2 * D_FF), per_e(1, 2 * D_FF), per_e(D_FF, D_MODEL), per_e(1, D_MODEL)],
            out_specs=rows),
        out_shape=jax.ShapeDtypeStruct((n_tiles * tm, D_MODEL), BF16),
        compiler_params=_params(1),
        name="moe_ffn",
    )(tile_expert, n_valid, xs, w["w_gate_up"], w["b_gate_up"], w["w_down"], w["b_down"])

    return pl.pallas_call(
        _combine_kernel,
        grid_spec=pltpu.PrefetchScalarGridSpec(
            num_scalar_prefetch=3, grid=(nb,),
            in_specs=[choice, choice, choice, pl.BlockSpec((tb, D_MODEL), lambda j, *_: (j, 0)),
                      pl.BlockSpec((1, D_MODEL), lambda j, *_: (0, 0)), pl.BlockSpec(memory_space=pl.ANY)],
            out_specs=pl.BlockSpec((tb, D_MODEL), lambda j, *_: (j, 0)),
            scratch_shapes=[pltpu.VMEM((r_blk, D_MODEL), BF16), pltpu.VMEM((tb, r_blk), BF16),
                            pltpu.SemaphoreType.DMA(())]),
        out_shape=jax.ShapeDtypeStruct((T, D_MODEL), F32),
        compiler_params=_params(1),
        name="moe_combine",
    )(*plan, idx, rank, wt, h2, w["g_final"], out)


def _rope_table(pos):
    half = QK_ROPE // 2
    inv = ROPE_THETA ** (-jnp.arange(half, dtype=F32) * 2.0 / QK_ROPE)
    ang = pos[:, None] * inv[None, :]
    cos, sin, zero = jnp.cos(ang), jnp.sin(ang), jnp.zeros_like(ang)
    return jnp.concatenate([cos, cos, cos, cos, -sin, zero, -sin, zero, zero, sin, zero, sin], axis=1)


def _prepare_weights(g_attn, w_in, g_q, g_kv, w_uq, w_uk, w_uv, w_a2, b_a, g_gla, w_o_mla, w_o_gla, w_out,
                     g_mem, w_mk, w_mv, g_cross, w_mq, w_mo, g_ffn, w_router, b_router, w_gate_up, b_gate_up,
                     w_down, b_down, g_final):
    splits = np.cumsum((0,) + IN_SPLITS)
    parts = [w_in[0][:, splits[i]:splits[i + 1]] for i in range(len(IN_SPLITS))]
    parts = [jnp.pad(p, ((0, 0), (0, wd - p.shape[1]))) for p, wd in zip(parts, _PACKED_WIDTHS)]
    wq = w_uq[0].reshape(Q_LORA, MLA_HEADS, QK_NOPE + QK_ROPE)
    wq_rope = jnp.pad(wq[:, :, QK_NOPE:], ((0, 0), (0, 0), (0, LANES - QK_ROPE)))
    row = lambda v: v.reshape(1, -1)
    return dict(
        g_attn=row(g_attn[0]), w_in=jnp.concatenate(parts, axis=1).astype(BF16), g_q=row(g_q[0]), g_kv=row(g_kv[0]),
        w_uq=jnp.concatenate([wq[:, :, :QK_NOPE].reshape(Q_LORA, -1), wq_rope.reshape(Q_LORA, -1)], axis=1).astype(BF16),
        wuk=jnp.transpose(w_uk[0], (1, 2, 0)).astype(BF16),
        wuv=jnp.transpose(w_uv[0], (1, 0, 2)).astype(BF16),
        w_a2=jnp.pad(w_a2[0], ((0, LANES - GLA_RANK), (0, 0))).astype(BF16), b_a=row(b_a[0]),
        g_gla=row(g_gla[0]), w_o_mla=w_o_mla[0].astype(BF16), w_o_gla=w_o_gla[0].astype(BF16),
        w_out=w_out[0].astype(BF16), g_mem=row(g_mem[0]),
        w_mkv=jnp.concatenate([w_mk[0], w_mv[0]], axis=1).astype(BF16),
        g_cross=row(g_cross[0]), w_mq=w_mq[0].astype(BF16), w_mo=w_mo[0].astype(BF16), g_ffn=row(g_ffn[0]),
        w_router_t=w_router[0].T, b_router=b_router[0].reshape(-1, 1),
        w_gate_up=w_gate_up[0].astype(BF16), b_gate_up=b_gate_up[0].reshape(N_EXPERTS, 1, 2 * D_FF),
        w_down=w_down[0].astype(BF16), b_down=b_down[0].reshape(N_EXPERTS, 1, D_MODEL), g_final=row(g_final),
    )


def kernel(x_prompt, x_sample, cache_kv_latent, cache_k_rope, state_gla, cache_mem_k, cache_mem_v, page_table, mem_prompt, g_attn, w_in, g_q, g_kv, w_uq, w_uk, w_uv, w_a2, b_a, g_gla, w_o_mla, w_o_gla, w_out, g_mem, w_mk, w_mv, g_cross, w_mq, w_mo, g_ffn, w_router, b_router, w_gate_up, b_gate_up, w_down, b_down, g_final):
    w = _prepare_weights(g_attn, w_in, g_q, g_kv, w_uq, w_uk, w_uv, w_a2, b_a, g_gla, w_o_mla, w_o_gla, w_out,
                         g_mem, w_mk, w_mv, g_cross, w_mq, w_mo, g_ffn, w_router, b_router, w_gate_up, b_gate_up,
                         w_down, b_down, g_final)
    B, S, D = x_prompt.shape
    Bd = x_sample.shape[0]
    past_len = page_table.shape[1] * cache_kv_latent.shape[2]

    tm = min(256, S)
    xp = x_prompt.reshape(B * S, D)
    (qcat, ckv, krope, kcat, kcat_t, gq, gk, gv, sg, la, szm, szg) = _premix(
        xp, _rope_table(jnp.arange(S, dtype=F32)), w, tm, BF16)
    olat = _mla_prompt(qcat.reshape(MLA_HEADS, B, S, QCAT), kcat.reshape(B, S, QCAT), kcat_t, tm)
    og, state_p = _gla_prompt(gq.reshape(B, S, -1), gk.reshape(B, S, -1), gv.reshape(B, S, -1), la.reshape(B, S, -1))
    h1, qx = _postmix(xp, olat.reshape(B * S, -1), og.reshape(B * S, -1), sg, szm, szg, w, tm)
    mem_k, mem_v = _memkv(mem_prompt.reshape(B * N_MEM, D), w["g_mem"], w["w_mkv"], min(512, B * N_MEM))
    tb = min(MOE_BLOCK, S)
    h2, xn2, idx, rank, wt, cnt = _cross_prompt(qx, h1, mem_k.reshape(B, N_MEM, MEM_WIDTH),
                                                mem_v.reshape(B, N_MEM, MEM_WIDTH), w, tb)
    y_prompt = _moe(xn2, idx, rank, wt, cnt.reshape(-1, N_EXPERTS), h2, w, tb, tb)

    xs = x_sample.reshape(Bd, D)
    (qcat_s, ckv_s, krope_s, kcat_s, _, gq_s, gk_s, gv_s, sg_s, la_s, szm_s, szg_s) = _premix(
        xs, _rope_table(jnp.full((Bd,), past_len, F32)), w, Bd, F32)
    olat_s = _mla_sample(page_table, qcat_s, kcat_s, cache_kv_latent[0], jnp.swapaxes(cache_k_rope[0], 1, 2))
    og_s, state_s = _gla_sample(gq_s, gk_s, gv_s, la_s, state_gla[0])
    h1_s, qx_s = _postmix(xs, olat_s.reshape(Bd, -1), og_s, sg_s, szm_s, szg_s, w, Bd)
    h2_s, xn2_s, idx_s, rank_s, wt_s, cnt_s = _cross_sample(
        qx_s, h1_s, cache_mem_k[0].reshape(Bd, N_MEM, MEM_WIDTH), cache_mem_v[0].reshape(Bd, N_MEM, MEM_WIDTH), w)
    y_sample = _moe(xn2_s, idx_s, rank_s, wt_s, cnt_s.reshape(-1, N_EXPERTS), h2_s, w, Bd, Bd)

    return (y_prompt.reshape(B, S, D), y_sample.reshape(Bd, 1, D),
            ckv.reshape(1, B, S, KV_LORA), krope.reshape(1, B, S, QK_ROPE), state_p[None],
            mem_k.reshape(1, B, N_MEM, MEM_HEADS, MEM_HEAD_DIM), mem_v.reshape(1, B, N_MEM, MEM_HEADS, MEM_HEAD_DIM),
            ckv_s.reshape(1, Bd, 1, KV_LORA), krope_s.reshape(1, Bd, 1, QK_ROPE), state_s[None])
```

```python
import functools

import jax
import jax.numpy as jnp
import numpy as np
from jax import lax
from jax.experimental import pallas as pl
from jax.experimental.pallas import tpu as pltpu

F32 = jnp.float32
BF16 = jnp.bfloat16

D_MODEL = 1024
MLA_HEADS = 8
Q_LORA = 256
KV_LORA = 128
QK_NOPE = 128
QK_ROPE = 64
ROPE_THETA = 10000.0
ATTN_SCALE = (QK_NOPE + QK_ROPE) ** -0.5
GLA_HEADS = 4
GLA_DK = 128
GLA_DV = 256
GLA_RANK = 16
GLA_TAU = 16.0
GLA_CHUNK = 64
GLA_GROUP = 4
N_MEM = 256
MEM_HEADS = 4
MEM_HEAD_DIM = 128
N_EXPERTS = 32
TOP_K = 4
D_FF = 1024
SWIGLU_LIMIT = 7.0
SWIGLU_ALPHA = 1.702
EPS = 1e-6

GLA_K_WIDTH = GLA_HEADS * GLA_DK
GLA_V_WIDTH = GLA_HEADS * GLA_DV
MEM_WIDTH = MEM_HEADS * MEM_HEAD_DIM
IN_SPLITS = (Q_LORA, KV_LORA, QK_ROPE, GLA_K_WIDTH, GLA_K_WIDTH, GLA_V_WIDTH, GLA_V_WIDTH, GLA_RANK, D_MODEL, D_MODEL)

LANES = 128
_PACKED_WIDTHS = (Q_LORA, KV_LORA, LANES, GLA_K_WIDTH, GLA_K_WIDTH, GLA_V_WIDTH, GLA_V_WIDTH, LANES, D_MODEL, D_MODEL)
_OFF = tuple(int(v) for v in np.cumsum((0,) + _PACKED_WIDTHS))
QCAT = 2 * LANES

VMEM_LIMIT = 56 << 20


def _params(n_axes):
    return pltpu.CompilerParams(dimension_semantics=("arbitrary",) * n_axes, vmem_limit_bytes=VMEM_LIMIT)


def _rms(x, g):
    var = jnp.mean(x * x, axis=-1, keepdims=True)
    return x * lax.rsqrt(var + EPS) * g


def _dot(a, b):
    return jnp.dot(a.astype(BF16), b.astype(BF16), preferred_element_type=F32)


def _dot_nt(a, b):
    return lax.dot_general(a.astype(BF16), b.astype(BF16), (((1,), (1,)), ((), ())), preferred_element_type=F32)


def _dot_tn(a, b):
    return lax.dot_general(a.astype(BF16), b.astype(BF16), (((0,), (0,)), ((), ())), preferred_element_type=F32)


def _split3(x):
    hi = x.astype(BF16)
    r1 = x - hi.astype(F32)
    mid = r1.astype(BF16)
    lo = (r1 - mid.astype(F32)).astype(BF16)
    return hi, mid, lo


def _dot_f32_nt(a, b):
    a0, a1, a2 = _split3(a)
    b0, b1, b2 = _split3(b)
    d = lambda u, v: lax.dot_general(u, v, (((1,), (1,)), ((), ())), preferred_element_type=F32)
    return ((d(a1, b1) + d(a0, b2) + d(a2, b0)) + (d(a0, b1) + d(a1, b0))) + d(a0, b0)


def _full(shape):
    n = len(shape)
    return pl.BlockSpec(shape, lambda *_: (0,) * n)


def _premix_kernel(x_ref, tab_ref, g_attn_ref, w_in_ref, g_q_ref, g_kv_ref, w_uq_ref, wuk_ref, w_a2_ref, b_a_ref,
                   qcat_ref, ckv_ref, krope_ref, kcat_ref, kcat_t_ref, gq_ref, gk_ref, gv_ref, sg_ref, la_ref, szm_ref, szg_ref):
    xn = _rms(x_ref[...], g_attn_ref[...]).astype(BF16)

    def proj(i):
        return jnp.dot(xn, w_in_ref[:, _OFF[i]:_OFF[i + 1]], preferred_element_type=F32)

    tab = tab_ref[...]
    cos, sin_lo, sin_hi = tab[:, :LANES], tab[:, LANES:2 * LANES], tab[:, 2 * LANES:]

    def rope(t):
        return t * cos + pltpu.roll(t, LANES - QK_ROPE // 2, 1) * sin_lo + pltpu.roll(t, QK_ROPE // 2, 1) * sin_hi

    c_q = _rms(proj(0), g_q_ref[...]).astype(BF16)
    ckv = _rms(proj(1), g_kv_ref[...])
    kr = rope(proj(2))
    ckv_ref[...] = ckv
    krope_ref[...] = kr[:, :QK_ROPE]
    kcat = jnp.concatenate([ckv, kr], axis=1)
    kcat_ref[...] = kcat.astype(BF16)
    kcat_t_ref[...] = kcat.T.astype(BF16)
    for h in range(MLA_HEADS):
        q_nope = jnp.dot(c_q, w_uq_ref[:, h * LANES:(h + 1) * LANES], preferred_element_type=F32)
        q_lat = _dot(q_nope, wuk_ref[h])
        q_rope = rope(jnp.dot(c_q, w_uq_ref[:, (MLA_HEADS + h) * LANES:(MLA_HEADS + h + 1) * LANES],
                              preferred_element_type=F32))
        qcat_ref[h, :, :LANES] = q_lat.astype(qcat_ref.dtype)
        qcat_ref[h, :, LANES:] = q_rope.astype(qcat_ref.dtype)
    gq_ref[...] = (proj(3) * GLA_DK ** -0.5).astype(gq_ref.dtype)
    gk_ref[...] = proj(4).astype(gk_ref.dtype)
    gv_ref[...] = proj(5).astype(gv_ref.dtype)
    gg = proj(6)
    sg_ref[...] = (gg * jax.nn.sigmoid(gg)).astype(sg_ref.dtype)
    z = _dot(proj(7), w_a2_ref[...]) + b_a_ref[...]
    la_ref[...] = -(jnp.maximum(-z, 0.0) + jnp.log1p(jnp.exp(-jnp.abs(z)))) * (1.0 / GLA_TAU)
    szm_ref[...] = jax.nn.sigmoid(proj(8)).astype(szm_ref.dtype)
    szg_ref[...] = jax.nn.sigmoid(proj(9)).astype(szg_ref.dtype)


def _premix(x, tab, w, tm, gdtype):
    T = x.shape[0]
    n_tab = tab.shape[0] // tm
    row = lambda n: pl.BlockSpec((tm, n), lambda i: (i, 0))
    outs = [(KV_LORA, F32), (QK_ROPE, F32), (QCAT, BF16), None, (GLA_K_WIDTH, gdtype),
            (GLA_K_WIDTH, gdtype), (GLA_V_WIDTH, gdtype), (GLA_V_WIDTH, BF16), (GLA_K_WIDTH, F32),
            (D_MODEL, BF16), (D_MODEL, BF16)]
    out_specs = [row(o[0]) if o else pl.BlockSpec((None, QCAT, tm), lambda i: (i, 0, 0)) for o in outs]
    out_shape = [jax.ShapeDtypeStruct((T, o[0]) if o else (T // tm, QCAT, tm), o[1] if o else BF16) for o in outs]
    consts = [w["g_attn"], w["w_in"], w["g_q"], w["g_kv"], w["w_uq"], w["wuk"], w["w_a2"], w["b_a"]]
    return pl.pallas_call(
        _premix_kernel,
        grid=(T // tm,),
        in_specs=[row(D_MODEL), pl.BlockSpec((tm, 3 * LANES), lambda i: (i % n_tab, 0))]
        + [_full(c.shape) for c in consts],
        out_specs=[pl.BlockSpec((MLA_HEADS, tm, QCAT), lambda i: (0, i, 0))] + out_specs,
        out_shape=[jax.ShapeDtypeStruct((MLA_HEADS, T, QCAT), BF16)] + out_shape,
        compiler_params=_params(1),
        name="premix",
    )(x, tab, *consts)


def _mla_prompt_kernel(q_ref, k_ref, kt_ref, o_ref, m_ref, l_ref, acc_ref):
    tq = q_ref.shape[1]
    qi = pl.program_id(1)
    q = q_ref[...].reshape(MLA_HEADS * tq, QCAT)
    row = lax.broadcasted_iota(jnp.int32, (MLA_HEADS, tq, tq), 1).reshape(MLA_HEADS * tq, tq)
    col = lax.broadcasted_iota(jnp.int32, (MLA_HEADS * tq, tq), 1)
    m_ref[...] = jnp.full_like(m_ref, -jnp.inf)
    l_ref[...] = jnp.zeros_like(l_ref)
    acc_ref[...] = jnp.zeros_like(acc_ref)
    c = ATTN_SCALE * np.log2(np.e)

    def block(j, masked):
        keys = pl.ds(pl.multiple_of(j * tq, tq), tq)
        s = jnp.dot(q, kt_ref[j], preferred_element_type=F32)
        if masked:
            s = jnp.where(col <= row, s, -jnp.inf)
        m = m_ref[...]
        m_new = jnp.maximum(m, jnp.max(s, axis=-1, keepdims=True))
        alpha = jnp.exp2((m - m_new) * c)
        p = jnp.exp2((s - jnp.concatenate([m_new] * (tq // LANES), axis=1)) * c)
        p_lanes = p[:, :LANES]
        for t in range(1, tq // LANES):
            p_lanes = p_lanes + p[:, t * LANES:(t + 1) * LANES]
        l_ref[...] = alpha * l_ref[...] + p_lanes
        acc_ref[...] = alpha * acc_ref[...] + _dot(p, k_ref[keys, :KV_LORA])
        m_ref[...] = m_new

    def body(j, carry):
        block(j, False)
        return carry

    lax.fori_loop(0, qi, body, 0)
    block(qi, True)
    o = acc_ref[...] / jnp.sum(l_ref[...], axis=-1, keepdims=True)
    for h in range(MLA_HEADS):
        o_ref[:, h * KV_LORA:(h + 1) * KV_LORA] = o[h * tq:(h + 1) * tq].astype(o_ref.dtype)


def _mla_prompt(qcat, kcat, kcat_t, tq):
    _, B, S, _ = qcat.shape
    rows = MLA_HEADS * tq
    return pl.pallas_call(
        _mla_prompt_kernel,
        grid=(B, S // tq),
        in_specs=[pl.BlockSpec((MLA_HEADS, None, tq, QCAT), lambda b, i: (0, b, i, 0)),
                  pl.BlockSpec((None, S, QCAT), lambda b, i: (b, 0, 0)),
                  pl.BlockSpec((S // tq, QCAT, tq), lambda b, i: (b, 0, 0))],
        out_specs=pl.BlockSpec((None, tq, MLA_HEADS * KV_LORA), lambda b, i: (b, i, 0)),
        out_shape=jax.ShapeDtypeStruct((B, S, MLA_HEADS * KV_LORA), BF16),
        scratch_shapes=[pltpu.VMEM((rows, LANES), F32), pltpu.VMEM((rows, LANES), F32), pltpu.VMEM((rows, KV_LORA), F32)],
        compiler_params=_params(2),
        name="mla_prompt",
    )(qcat, kcat, kcat_t)


def _mla_sample_kernel(pt_ref, q_ref, knew_ref, lat_hbm, rope_hbm, o_ref, lat_buf, rope_buf, lat_bf, sems):
    b = pl.program_id(0)
    n_pages, page = lat_buf.shape[1], lat_buf.shape[2]

    def page_copies(sample, slot, i):
        pg = pt_ref[sample, i]
        return (pltpu.make_async_copy(lat_hbm.at[pg], lat_buf.at[slot, i], sems.at[0, slot]),
                pltpu.make_async_copy(rope_hbm.at[pg], rope_buf.at[slot, :, pl.ds(pl.multiple_of(i * page, page), page)],
                                      sems.at[1, slot]))

    def for_pages(sample, slot, act):
        def body(i, carry):
            for cp in page_copies(sample, slot, i):
                act(cp)
            return carry
        lax.fori_loop(0, n_pages, body, 0)

    slot = b % 2

    @pl.when(b == 0)
    def _():
        for_pages(0, 0, lambda cp: cp.start())

    @pl.when(b + 1 < pl.num_programs(0))
    def _():
        for_pages(b + 1, 1 - slot, lambda cp: cp.start())

    for_pages(b, slot, lambda cp: cp.wait())

    lat_bf[...] = lat_buf[slot].reshape(n_pages * page, KV_LORA).astype(BF16)
    q = q_ref[...].reshape(MLA_HEADS, QCAT)
    knew = knew_ref[...].astype(F32)
    s = (_dot_nt(q[:, :KV_LORA], lat_bf[...]) + _dot(q[:, KV_LORA:KV_LORA + QK_ROPE], rope_buf[slot])) * ATTN_SCALE
    s_new = jnp.sum(q.astype(F32) * knew, axis=-1, keepdims=True) * ATTN_SCALE
    m = jnp.maximum(jnp.max(s, axis=-1, keepdims=True), s_new)
    p = jnp.exp(s - m)
    p_new = jnp.exp(s_new - m)
    denom = jnp.sum(p, axis=-1, keepdims=True) + p_new
    o = _dot(p, lat_bf[...]) + p_new * knew[:, :KV_LORA]
    o_ref[...] = (o / denom).astype(o_ref.dtype)


def _mla_sample(page_table, qcat, knew, cache_lat, cache_rope_t):
    Bd, n_pages = page_table.shape
    page = cache_lat.shape[1]
    grid_spec = pltpu.PrefetchScalarGridSpec(
        num_scalar_prefetch=1,
        grid=(Bd,),
        in_specs=[pl.BlockSpec((MLA_HEADS, None, 1, QCAT), lambda b, pt: (0, b, 0, 0)),
                  pl.BlockSpec((None, 1, QCAT), lambda b, pt: (b, 0, 0)),
                  pl.BlockSpec(memory_space=pl.ANY), pl.BlockSpec(memory_space=pl.ANY)],
        out_specs=pl.BlockSpec((None, MLA_HEADS, KV_LORA), lambda b, pt: (b, 0, 0)),
        scratch_shapes=[pltpu.VMEM((2, n_pages, page, KV_LORA), F32), pltpu.VMEM((2, QK_ROPE, n_pages * page), F32),
                        pltpu.VMEM((n_pages * page, KV_LORA), BF16), pltpu.SemaphoreType.DMA((2, 2))],
    )
    return pl.pallas_call(
        _mla_sample_kernel,
        grid_spec=grid_spec,
        out_shape=jax.ShapeDtypeStruct((Bd, MLA_HEADS, KV_LORA), BF16),
        compiler_params=_params(1),
        name="mla_sample",
    )(page_table, qcat.reshape(MLA_HEADS, Bd, 1, QCAT), knew.reshape(Bd, 1, QCAT), cache_lat, cache_rope_t)


def _gla_prompt_kernel(q_ref, k_ref, v_ref, la_ref, o_ref, state_ref, st_ref):
    C = GLA_CHUNK
    S = q_ref.shape[0]
    row = lax.broadcasted_iota(jnp.int32, (C, 2 * C), 0)
    col = lax.broadcasted_iota(jnp.int32, (C, 2 * C), 1)
    causal = row >= col
    tri = causal[:, :C].astype(BF16)
    pad_k = jnp.zeros((C, GLA_DK), F32)
    pad_v = jnp.zeros((C, GLA_DV), F32)
    st_ref[...] = jnp.zeros_like(st_ref)

    group = min(GLA_GROUP, S // C)

    def chunks(c, carry):
        parts = []
        for g in range(group):
            rows = pl.ds(pl.multiple_of((c * group + g) * C, C), C)
            split = jnp.dot(tri, jnp.concatenate(_split3(la_ref[rows, :]), axis=1), preferred_element_type=F32)
            b_all = (split[:, 2 * GLA_K_WIDTH:] + split[:, GLA_K_WIDTH:2 * GLA_K_WIDTH]) + split[:, :GLA_K_WIDTH]
            for h in range(GLA_HEADS):
                kcols = slice(h * GLA_DK, (h + 1) * GLA_DK)
                vcols = slice(h * GLA_DV, (h + 1) * GLA_DV)
                b = b_all[:, kcols]
                b_last = b[C - 1:C, :]
                q = q_ref[rows, kcols].astype(F32)
                k = k_ref[rows, kcols].astype(F32)
                q_in = (q * jnp.exp(b)).astype(BF16)
                k_in = jnp.concatenate([k * jnp.exp(-b), pad_k], axis=0)
                k_out = jnp.concatenate([k * jnp.exp(b_last - b), pad_k], axis=0)
                a = jnp.where(causal, _dot_nt(q_in, k_in), 0.0)
                v_t = jnp.concatenate([v_ref[rows, vcols].astype(F32), pad_v], axis=0).T.astype(BF16)
                lhs = jnp.concatenate([q_in, a.astype(BF16)], axis=1)
                parts.append((rows, h, vcols, lhs, v_t, jnp.exp(b_last), _dot(v_t, k_out)))
        for rows, h, vcols, lhs, v_t, decay, kv_t in parts:
            st = st_ref[h]
            o_ref[rows, vcols] = _dot_nt(lhs, jnp.concatenate([st.astype(BF16), v_t], axis=1)).astype(o_ref.dtype)
            st_ref[h] = st * decay + kv_t
        return carry

    lax.fori_loop(0, S // C // group, chunks, 0)
    for h in range(GLA_HEADS):
        state_ref[h] = st_ref[h].T


def _gla_prompt(q, k, v, la):
    B, S, _ = q.shape
    seq = lambda n: pl.BlockSpec((None, S, n), lambda b: (b, 0, 0))
    return pl.pallas_call(
        _gla_prompt_kernel,
        grid=(B,),
        in_specs=[seq(GLA_K_WIDTH), seq(GLA_K_WIDTH), seq(GLA_V_WIDTH), seq(GLA_K_WIDTH)],
        out_specs=[seq(GLA_V_WIDTH), pl.BlockSpec((None, GLA_HEADS, GLA_DK, GLA_DV), lambda b: (b, 0, 0, 0))],
        out_shape=[jax.ShapeDtypeStruct((B, S, GLA_V_WIDTH), BF16),
                   jax.ShapeDtypeStruct((B, GLA_HEADS, GLA_DK, GLA_DV), F32)],
        scratch_shapes=[pltpu.VMEM((GLA_HEADS, GLA_DV, GLA_DK), F32)],
        compiler_params=_params(1),
        name="gla_prompt",
    )(q, k, v, la)


def _gla_sample_kernel(q_ref, k_ref, v_ref, la_ref, st_ref, o_ref, sto_ref):
    nb = q_ref.shape[0]
    pad = jnp.zeros((GLA_DK - nb, GLA_DK), F32)
    for h in range(GLA_HEADS):
        kcols = slice(h * GLA_DK, (h + 1) * GLA_DK)
        vcols = slice(h * GLA_DV, (h + 1) * GLA_DV)
        col = lambda ref: jnp.concatenate([ref[:, kcols].astype(F32), pad], axis=0).T
        q_t, k_t, decay_t = col(q_ref), col(k_ref), jnp.exp(col(la_ref))
        for i in range(nb):
            new = decay_t[:, i:i + 1] * st_ref[i, h] + k_t[:, i:i + 1] * v_ref[i:i + 1, vcols].astype(F32)
            sto_ref[i, h] = new
            o_ref[i:i + 1, vcols] = jnp.sum(q_t[:, i:i + 1] * new, axis=0, keepdims=True).astype(o_ref.dtype)


def _gla_sample(q, k, v, la, state, nb=8):
    Bd = q.shape[0]
    rows = lambda n: pl.BlockSpec((nb, n), lambda i: (i, 0))
    st_spec = pl.BlockSpec((nb, GLA_HEADS, GLA_DK, GLA_DV), lambda i: (i, 0, 0, 0))
    return pl.pallas_call(
        _gla_sample_kernel,
        grid=(Bd // nb,),
        in_specs=[rows(GLA_K_WIDTH), rows(GLA_K_WIDTH), rows(GLA_V_WIDTH), rows(GLA_K_WIDTH), st_spec],
        out_specs=[rows(GLA_V_WIDTH), st_spec],
        out_shape=[jax.ShapeDtypeStruct((Bd, GLA_V_WIDTH), BF16), jax.ShapeDtypeStruct(state.shape, F32)],
        compiler_params=_params(1),
        name="gla_sample",
    )(q, k, v, la, state)


def _postmix_kernel(x_ref, olat_ref, og_ref, sg_ref, szm_ref, szg_ref, wuv_ref, w_o_mla_ref, g_gla_ref, w_o_gla_ref,
                    w_out_ref, g_cross_ref, w_mq_ref, h_ref, qx_ref):
    ov = jnp.concatenate(
        [_dot(olat_ref[:, h * KV_LORA:(h + 1) * KV_LORA], wuv_ref[h]).astype(BF16) for h in range(MLA_HEADS)], axis=1)
    o_mla = jnp.dot(ov, w_o_mla_ref[...], preferred_element_type=F32)
    g_gla = g_gla_ref[...]
    og = jnp.concatenate(
        [(_rms(og_ref[:, h * GLA_DV:(h + 1) * GLA_DV].astype(F32), g_gla)
          * sg_ref[:, h * GLA_DV:(h + 1) * GLA_DV].astype(F32)).astype(BF16) for h in range(GLA_HEADS)], axis=1)
    o_gla = jnp.dot(og, w_o_gla_ref[...], preferred_element_type=F32)
    merged = szm_ref[...].astype(F32) * o_mla + szg_ref[...].astype(F32) * o_gla
    h1 = x_ref[...] + _dot(merged, w_out_ref[...])
    h_ref[...] = h1
    qx_ref[...] = _dot(_rms(h1, g_cross_ref[...]), w_mq_ref[...]).astype(qx_ref.dtype)


def _postmix(x, olat, og, sg, szm, szg, w, tm):
    T = x.shape[0]
    row = lambda n: pl.BlockSpec((tm, n), lambda i: (i, 0))
    consts = [w["wuv"], w["w_o_mla"], w["g_gla"], w["w_o_gla"], w["w_out"], w["g_cross"], w["w_mq"]]
    return pl.pallas_call(
        _postmix_kernel,
        grid=(T // tm,),
        in_specs=[row(D_MODEL)] * 6 + [_full(c.shape) for c in consts],
        out_specs=[row(D_MODEL), row(MEM_WIDTH)],
        out_shape=[jax.ShapeDtypeStruct((T, D_MODEL), F32), jax.ShapeDtypeStruct((T, MEM_WIDTH), BF16)],
        compiler_params=_params(1),
        name="postmix",
    )(x, olat, og, sg, szm, szg, *consts)


def _memkv_kernel(mem_ref, g_ref, w_ref, k_ref, v_ref):
    kv = _dot(_rms(mem_ref[...], g_ref[...]), w_ref[...])
    k_ref[...] = kv[:, :MEM_WIDTH]
    v_ref[...] = kv[:, MEM_WIDTH:]


def _memkv(mem, g_mem, w_mkv, tm=512):
    T = mem.shape[0]
    row = lambda n: pl.BlockSpec((tm, n), lambda i: (i, 0))
    return pl.pallas_call(
        _memkv_kernel,
        grid=(T // tm,),
        in_specs=[row(D_MODEL), _full(g_mem.shape), _full(w_mkv.shape)],
        out_specs=[row(MEM_WIDTH), row(MEM_WIDTH)],
        out_shape=[jax.ShapeDtypeStruct((T, MEM_WIDTH), F32)] * 2,
        compiler_params=_params(1),
        name="memkv",
    )(mem, g_mem, w_mkv)


def _attend_memory(q, mem_k, mem_v):
    outs = []
    for h in range(MEM_HEADS):
        cols = slice(h * MEM_HEAD_DIM, (h + 1) * MEM_HEAD_DIM)
        s = _dot_nt(q[:, cols], mem_k[:, cols]) * MEM_HEAD_DIM ** -0.5
        p = jnp.exp(s - jnp.max(s, axis=-1, keepdims=True))
        p = p / jnp.sum(p, axis=-1, keepdims=True)
        outs.append(_dot(p, mem_v[:, cols]).astype(BF16))
    return jnp.concatenate(outs, axis=1)


def _route(o, h1, w_mo_ref, g_ffn_ref, w_router_ref, b_router_ref, h_ref, xn_ref, idx_ref, rank_ref, wt_ref, cnt_ref):
    h2 = h1 + jnp.dot(o, w_mo_ref[...], preferred_element_type=F32)
    h_ref[...] = h2
    xn = _rms(h2, g_ffn_ref[...])
    xn_ref[...] = xn.astype(xn_ref.dtype)
    tb = xn.shape[0]
    logits = _dot_f32_nt(w_router_ref[...], xn) + b_router_ref[...]
    expert = lax.broadcasted_iota(jnp.int32, logits.shape, 0)
    work = logits
    hits, firsts, exps = [], [], []
    top = None
    for _ in range(TOP_K):
        best = jnp.max(work, axis=0, keepdims=True)
        first = jnp.min(jnp.where(work == best, expert, N_EXPERTS), axis=0, keepdims=True)
        hit = expert == first
        top = best if top is None else top
        hits.append(hit)
        firsts.append(first)
        exps.append(jnp.exp(best - top))
        work = jnp.where(hit, -jnp.inf, work)
    denom = (exps[0] + exps[1]) + (exps[2] + exps[3])
    chosen = jnp.zeros(logits.shape, F32)
    for hit in hits:
        chosen = chosen + jnp.where(hit, 1.0, 0.0)
    before = (lax.broadcasted_iota(jnp.int32, (tb, tb), 0) < lax.broadcasted_iota(jnp.int32, (tb, tb), 1)).astype(BF16)
    rank = jnp.dot(chosen.astype(BF16), before, preferred_element_type=F32)
    idx_ref[...] = jnp.concatenate(firsts, axis=0)
    rank_ref[...] = jnp.concatenate(
        [jnp.sum(jnp.where(hit, rank, 0.0), axis=0, keepdims=True) for hit in hits], axis=0).astype(jnp.int32)
    wt_ref[...] = jnp.concatenate([e / denom for e in exps], axis=0)
    cnt_ref[...] = jnp.sum(chosen, axis=1, keepdims=True).astype(jnp.int32)


def _cross_prompt_kernel(qx_ref, h1_ref, mk_ref, mv_ref, *rest):
    o = _attend_memory(qx_ref[...], mk_ref[...], mv_ref[...])
    _route(o, h1_ref[...], *rest)


def _cross_sample_kernel(qx_ref, mk_ref, mv_ref, o_ref):
    for i in range(qx_ref.shape[0]):
        q = jnp.broadcast_to(qx_ref[i:i + 1, :], (8, MEM_WIDTH))
        o_ref[i:i + 1, :] = _attend_memory(q, mk_ref[i], mv_ref[i])[:1]


def _route_kernel(o_ref, h1_ref, *rest):
    _route(o_ref[...], h1_ref[...], *rest)


def _route_specs(T, tb, w):
    consts = [w["w_mo"], w["g_ffn"], w["w_router_t"], w["b_router"]]
    row = lambda n: pl.BlockSpec((tb, n), lambda i: (i, 0))
    per_choice = pl.BlockSpec((TOP_K, tb), lambda i: (0, i))
    out_specs = [row(D_MODEL), row(D_MODEL), per_choice, per_choice, per_choice,
                 pl.BlockSpec((None, N_EXPERTS, 1), lambda i: (i, 0, 0))]
    out_shape = [jax.ShapeDtypeStruct((T, D_MODEL), F32), jax.ShapeDtypeStruct((T, D_MODEL), BF16),
                 jax.ShapeDtypeStruct((TOP_K, T), jnp.int32), jax.ShapeDtypeStruct((TOP_K, T), jnp.int32),
                 jax.ShapeDtypeStruct((TOP_K, T), F32), jax.ShapeDtypeStruct((T // tb, N_EXPERTS, 1), jnp.int32)]
    return consts, out_specs, out_shape


def _cross_prompt(qx, h1, mem_k, mem_v, w, tb):
    T = qx.shape[0]
    blocks_per_mem = T // mem_k.shape[0] // tb
    consts, out_specs, out_shape = _route_specs(T, tb, w)
    row = lambda n: pl.BlockSpec((tb, n), lambda i: (i, 0))
    mem_spec = pl.BlockSpec((None, N_MEM, MEM_WIDTH), lambda i: (i // blocks_per_mem, 0, 0))
    return pl.pallas_call(
        _cross_prompt_kernel,
        grid=(T // tb,),
        in_specs=[row(MEM_WIDTH), row(D_MODEL), mem_spec, mem_spec] + [_full(c.shape) for c in consts],
        out_specs=out_specs,
        out_shape=out_shape,
        compiler_params=_params(1),
        name="cross_prompt",
    )(qx, h1, mem_k, mem_v, *consts)


def _cross_sample(qx, h1, mem_k, mem_v, w, nb=8):
    T = qx.shape[0]
    mem_spec = pl.BlockSpec((nb, N_MEM, MEM_WIDTH), lambda i: (i, 0, 0))
    rows = pl.BlockSpec((nb, MEM_WIDTH), lambda i: (i, 0))
    o = pl.pallas_call(
        _cross_sample_kernel,
        grid=(T // nb,),
        in_specs=[rows, mem_spec, mem_spec],
        out_specs=rows,
        out_shape=jax.ShapeDtypeStruct((T, MEM_WIDTH), BF16),
        compiler_params=_params(1),
        name="cross_sample",
    )(qx, mem_k, mem_v)
    consts, out_specs, out_shape = _route_specs(T, T, w)
    return pl.pallas_call(
        _route_kernel,
        grid=(1,),
        in_specs=[_full(o.shape), _full(h1.shape)] + [_full(c.shape) for c in consts],
        out_specs=out_specs,
        out_shape=out_shape,
        compiler_params=_params(1),
        name="route_sample",
    )(o, h1, *consts)


SEG_ALIGN = 16
SEL_CHUNK = 256
MOE_BLOCK = 512


def _block_rows(tb):
    worst = TOP_K * tb + N_EXPERTS * (SEG_ALIGN - 1)
    return -(-worst // SEL_CHUNK) * SEL_CHUNK


def _seg_sizes(limit):
    sizes, b = [], SEG_ALIGN
    while b <= limit:
        sizes.append(b)
        b *= 2
    return sizes[::-1]


def _for_segments(n, sizes, make_copy, act):
    for size in sizes:
        @pl.when((n & size) != 0)
        def _():
            act(make_copy(pl.multiple_of(n & (-2 * size), SEG_ALIGN), size))


def _slot_rows(idx, rank, loc_ref, j):
    pos = rank
    for e in range(N_EXPERTS):
        pos = pos + jnp.where(idx == e, loc_ref[j, e], 0)
    return pos


def _dispatch_kernel(loc_ref, pad_ref, goff_ref, total_ref, gap_off_ref, gap_len_ref, x_ref, idx_ref, rank_ref, xs_hbm,
                     buf, sem, *, tm):
    j = pl.program_id(0)
    tb = x_ref.shape[0]
    r_blk = buf.shape[0]
    pos = _slot_rows(idx_ref[...], rank_ref[...], loc_ref, j)
    x = x_ref[...]
    for c in range(r_blk // SEL_CHUNK):
        r = lax.broadcasted_iota(jnp.int32, (SEL_CHUNK, tb), 0) + c * SEL_CHUNK
        sel = jnp.where(r == pos[0:1], 1.0, jnp.where(r == pos[1:2], 1.0, jnp.where(
            r == pos[2:3], 1.0, jnp.where(r == pos[3:4], 1.0, 0.0))))
        buf[c * SEL_CHUNK:(c + 1) * SEL_CHUNK, :] = jnp.dot(
            sel.astype(BF16), x, preferred_element_type=F32).astype(BF16)

    sizes = _seg_sizes(tb)

    def for_experts(act):
        def body(e, carry):
            src, dst = loc_ref[j, e], goff_ref[j, e]
            _for_segments(pad_ref[j, e], sizes, lambda off, size: pltpu.make_async_copy(
                buf.at[pl.ds(pl.multiple_of(src + off, SEG_ALIGN), size)],
                xs_hbm.at[pl.ds(pl.multiple_of(dst + off, SEG_ALIGN), size)], sem), act)
            return carry
        lax.fori_loop(0, N_EXPERTS, body, 0)

    for_experts(lambda cp: cp.start())
    _for_segments(total_ref[j], _seg_sizes(r_blk), lambda off, size: pltpu.make_async_copy(
        buf.at[pl.ds(0, size)], xs_hbm.at[pl.ds(0, size)], sem), lambda cp: cp.wait())

    @pl.when(j == pl.num_programs(0) - 1)
    def _():
        buf[0:tm, :] = jnp.zeros((tm, D_MODEL), BF16)
        gap_sizes = _seg_sizes(tm - SEG_ALIGN)

        def for_gaps(act):
            def body(e, carry):
                dst = gap_off_ref[e]
                _for_segments(gap_len_ref[e], gap_sizes, lambda off, size: pltpu.make_async_copy(
                    buf.at[pl.ds(0, size)], xs_hbm.at[pl.ds(pl.multiple_of(dst + off, SEG_ALIGN), size)], sem), act)
                return carry
            lax.fori_loop(0, N_EXPERTS, body, 0)

        for_gaps(lambda cp: cp.start())
        for_gaps(lambda cp: cp.wait())


def _ffn_kernel(tile_expert_ref, n_valid_ref, x_ref, wgu_f32_ref, bgu_ref, wd_f32_ref, bd_ref, o_ref, wgu_ref, wd_ref):
    t = pl.program_id(0)

    @pl.when(t < n_valid_ref[0])
    def _():
        @pl.when((t == 0) | (tile_expert_ref[t] != tile_expert_ref[jnp.maximum(t - 1, 0)]))
        def _():
            wgu_ref[...] = wgu_f32_ref[...].astype(BF16)
            wd_ref[...] = wd_f32_ref[...].astype(BF16)

        x = x_ref[...]
        bgu = bgu_ref[...]
        gate = jnp.dot(x, wgu_ref[:, :D_FF], preferred_element_type=F32) + bgu[:, :D_FF]
        up = jnp.dot(x, wgu_ref[:, D_FF:], preferred_element_type=F32) + bgu[:, D_FF:]
        gate = jnp.minimum(gate, SWIGLU_LIMIT)
        up = jnp.clip(up, -SWIGLU_LIMIT, SWIGLU_LIMIT)
        hidden = (up + 1.0) * gate * jax.nn.sigmoid(SWIGLU_ALPHA * gate)
        o_ref[...] = (_dot(hidden, wd_ref[...]) + bd_ref[...]).astype(o_ref.dtype)


def _combine_kernel(loc_ref, pad_ref, goff_ref, total_ref, idx_ref, rank_ref, wt_ref, h_ref, g_final_ref, o_hbm, y_ref,
                    buf, sel_ref, sem):
    j = pl.program_id(0)
    tb = h_ref.shape[0]
    r_blk = buf.shape[0]

    @pl.when(j == 0)
    def _():
        buf[...] = jnp.zeros_like(buf)

    sizes = _seg_sizes(tb)

    def for_experts(act):
        def body(e, carry):
            dst, src = loc_ref[j, e], goff_ref[j, e]
            _for_segments(pad_ref[j, e], sizes, lambda off, size: pltpu.make_async_copy(
                o_hbm.at[pl.ds(pl.multiple_of(src + off, SEG_ALIGN), size)],
                buf.at[pl.ds(pl.multiple_of(dst + off, SEG_ALIGN), size)], sem), act)
            return carry
        lax.fori_loop(0, N_EXPERTS, body, 0)

    for_experts(lambda cp: cp.start())
    pos = _slot_rows(idx_ref[...], rank_ref[...], loc_ref, j)
    stacked = jnp.concatenate([pos.astype(F32), wt_ref[...], jnp.zeros((LANES - 2 * TOP_K, tb), F32)], axis=0).T
    for c in range(r_blk // SEL_CHUNK):
        r = (lax.broadcasted_iota(jnp.int32, (tb, SEL_CHUNK), 1) + c * SEL_CHUNK).astype(F32)
        sel = jnp.zeros((tb, SEL_CHUNK), F32)
        for k in range(TOP_K):
            sel = jnp.where(r == stacked[:, k:k + 1], stacked[:, TOP_K + k:TOP_K + k + 1], sel)
        sel_ref[:, c * SEL_CHUNK:(c + 1) * SEL_CHUNK] = sel.astype(BF16)
    _for_segments(total_ref[j], _seg_sizes(r_blk), lambda off, size: pltpu.make_async_copy(
        o_hbm.at[pl.ds(0, size)], buf.at[pl.ds(0, size)], sem), lambda cp: cp.wait())
    moe = jnp.dot(sel_ref[...], buf[...], preferred_element_type=F32)
    y_ref[...] = _rms(h_ref[...] + moe, g_final_ref[...])


def _moe(xn, idx, rank, wt, cnt, h2, w, tb, tm):
    T = xn.shape[0]
    nb = T // tb
    r_blk = _block_rows(tb)
    i32 = jnp.int32
    pad = (cnt + (SEG_ALIGN - 1)) // SEG_ALIGN * SEG_ALIGN
    loc = jnp.cumsum(pad, axis=1) - pad
    seg = jnp.sum(pad, axis=0)
    region = (seg + (tm - 1)) // tm * tm
    region_end = jnp.cumsum(region)
    region_start = region_end - region
    goff = region_start[None, :] + jnp.cumsum(pad, axis=0) - pad
    n_tiles = -(-(nb * (TOP_K * tb + N_EXPERTS * (SEG_ALIGN - 1)) + N_EXPERTS * (tm - 1)) // tm)
    n_valid = (region_end[-1] // tm).astype(i32).reshape(1)
    tile = jnp.minimum(jnp.arange(n_tiles, dtype=i32), n_valid - 1)
    tile_expert = jnp.minimum(jnp.sum((region_end[None, :] <= (tile * tm)[:, None]).astype(i32), axis=1), N_EXPERTS - 1)
    plan = [a.astype(i32) for a in (loc, pad, goff, jnp.sum(pad, axis=1))]
    gaps = [(region_start + seg).astype(i32), (region - seg).astype(i32)]

    choice = pl.BlockSpec((TOP_K, tb), lambda j, *_: (0, j))
    xs = pl.pallas_call(
        functools.partial(_dispatch_kernel, tm=tm),
        grid_spec=pltpu.PrefetchScalarGridSpec(
            num_scalar_prefetch=6, grid=(nb,),
            in_specs=[pl.BlockSpec((tb, D_MODEL), lambda j, *_: (j, 0)), choice, choice],
            out_specs=pl.BlockSpec(memory_space=pl.ANY),
            scratch_shapes=[pltpu.VMEM((max(r_blk, tm), D_MODEL), BF16), pltpu.SemaphoreType.DMA(())]),
        out_shape=jax.ShapeDtypeStruct((n_tiles * tm, D_MODEL), BF16),
        compiler_params=_params(1),
        name="moe_dispatch",
    )(*plan, *gaps, xn, idx, rank)

    rows = pl.BlockSpec((tm, D_MODEL), lambda t, te, nv: (jnp.minimum(t, nv[0] - 1), 0))
    per_e = lambda a, b: pl.BlockSpec((None, a, b), lambda t, te, nv: (te[t], 0, 0))
    out = pl.pallas_call(
        _ffn_kernel,
        grid_spec=pltpu.PrefetchScalarGridSpec(
            num_scalar_prefetch=2, grid=(n_tiles,),
            in_specs=[rows, per_e(D_MODEL, 2 * D_FF), per_e(1, 2 * D_FF), per_e(D_FF, D_MODEL), per_e(1, D_MODEL)],
            out_specs=rows,
            scratch_shapes=[pltpu.VMEM((D_MODEL, 2 * D_FF), BF16), pltpu.VMEM((D_FF, D_MODEL), BF16)]),
        out_shape=jax.ShapeDtypeStruct((n_tiles * tm, D_MODEL), BF16),
        compiler_params=_params(1),
        name="moe_ffn",
    )(tile_expert, n_valid, xs, w["w_gate_up"], w["b_gate_up"], w["w_down"], w["b_down"])

    return pl.pallas_call(
        _combine_kernel,
        grid_spec=pltpu.PrefetchScalarGridSpec(
            num_scalar_prefetch=4, grid=(nb,),
            in_specs=[choice, choice, choice, pl.BlockSpec((tb, D_MODEL), lambda j, *_: (j, 0)),
                      pl.BlockSpec((1, D_MODEL), lambda j, *_: (0, 0)), pl.BlockSpec(memory_space=pl.ANY)],
            out_specs=pl.BlockSpec((tb, D_MODEL), lambda j, *_: (j, 0)),
            scratch_shapes=[pltpu.VMEM((r_blk, D_MODEL), BF16), pltpu.VMEM((tb, r_blk), BF16),
                            pltpu.SemaphoreType.DMA(())]),
        out_shape=jax.ShapeDtypeStruct((T, D_MODEL), F32),
        compiler_params=_params(1),
        name="moe_combine",
    )(*plan, idx, rank, wt, h2, w["g_final"], out)


def _rope_table(pos):
    half = QK_ROPE // 2
    inv = ROPE_THETA ** (-jnp.arange(half, dtype=F32) * 2.0 / QK_ROPE)
    ang = pos[:, None] * inv[None, :]
    cos, sin, zero = jnp.cos(ang), jnp.sin(ang), jnp.zeros_like(ang)
    return jnp.concatenate([cos, cos, cos, cos, -sin, zero, -sin, zero, zero, sin, zero, sin], axis=1)


def _prepare_weights(g_attn, w_in, g_q, g_kv, w_uq, w_uk, w_uv, w_a2, b_a, g_gla, w_o_mla, w_o_gla, w_out,
                     g_mem, w_mk, w_mv, g_cross, w_mq, w_mo, g_ffn, w_router, b_router, w_gate_up, b_gate_up,
                     w_down, b_down, g_final):
    splits = np.cumsum((0,) + IN_SPLITS)
    parts = [w_in[0][:, splits[i]:splits[i + 1]] for i in range(len(IN_SPLITS))]
    parts = [jnp.pad(p, ((0, 0), (0, wd - p.shape[1]))) for p, wd in zip(parts, _PACKED_WIDTHS)]
    wq = w_uq[0].reshape(Q_LORA, MLA_HEADS, QK_NOPE + QK_ROPE)
    wq_rope = jnp.pad(wq[:, :, QK_NOPE:], ((0, 0), (0, 0), (0, LANES - QK_ROPE)))
    row = lambda v: v.reshape(1, -1)
    return dict(
        g_attn=row(g_attn[0]), w_in=jnp.concatenate(parts, axis=1).astype(BF16), g_q=row(g_q[0]), g_kv=row(g_kv[0]),
        w_uq=jnp.concatenate([wq[:, :, :QK_NOPE].reshape(Q_LORA, -1), wq_rope.reshape(Q_LORA, -1)], axis=1).astype(BF16),
        wuk=jnp.transpose(w_uk[0], (1, 2, 0)).astype(BF16),
        wuv=jnp.transpose(w_uv[0], (1, 0, 2)).astype(BF16),
        w_a2=jnp.pad(w_a2[0], ((0, LANES - GLA_RANK), (0, 0))).astype(BF16), b_a=row(b_a[0]),
        g_gla=row(g_gla[0]), w_o_mla=w_o_mla[0].astype(BF16), w_o_gla=w_o_gla[0].astype(BF16),
        w_out=w_out[0].astype(BF16), g_mem=row(g_mem[0]),
        w_mkv=jnp.concatenate([w_mk[0], w_mv[0]], axis=1).astype(BF16),
        g_cross=row(g_cross[0]), w_mq=w_mq[0].astype(BF16), w_mo=w_mo[0].astype(BF16), g_ffn=row(g_ffn[0]),
        w_router_t=w_router[0].T, b_router=b_router[0].reshape(-1, 1),
        w_gate_up=w_gate_up[0], b_gate_up=b_gate_up[0].reshape(N_EXPERTS, 1, 2 * D_FF),
        w_down=w_down[0], b_down=b_down[0].reshape(N_EXPERTS, 1, D_MODEL), g_final=row(g_final),
    )


def kernel(x_prompt, x_sample, cache_kv_latent, cache_k_rope, state_gla, cache_mem_k, cache_mem_v, page_table, mem_prompt, g_attn, w_in, g_q, g_kv, w_uq, w_uk, w_uv, w_a2, b_a, g_gla, w_o_mla, w_o_gla, w_out, g_mem, w_mk, w_mv, g_cross, w_mq, w_mo, g_ffn, w_router, b_router, w_gate_up, b_gate_up, w_down, b_down, g_final):
    w = _prepare_weights(g_attn, w_in, g_q, g_kv, w_uq, w_uk, w_uv, w_a2, b_a, g_gla, w_o_mla, w_o_gla, w_out,
                         g_mem, w_mk, w_mv, g_cross, w_mq, w_mo, g_ffn, w_router, b_router, w_gate_up, b_gate_up,
                         w_down, b_down, g_final)
    B, S, D = x_prompt.shape
    Bd = x_sample.shape[0]
    past_len = page_table.shape[1] * cache_kv_latent.shape[2]

    tm = min(256, S)
    xp = x_prompt.reshape(B * S, D)
    (qcat, ckv, krope, kcat, kcat_t, gq, gk, gv, sg, la, szm, szg) = _premix(
        xp, _rope_table(jnp.arange(S, dtype=F32)), w, tm, BF16)
    olat = _mla_prompt(qcat.reshape(MLA_HEADS, B, S, QCAT), kcat.reshape(B, S, QCAT), kcat_t, tm)
    og, state_p = _gla_prompt(gq.reshape(B, S, -1), gk.reshape(B, S, -1), gv.reshape(B, S, -1), la.reshape(B, S, -1))
    h1, qx = _postmix(xp, olat.reshape(B * S, -1), og.reshape(B * S, -1), sg, szm, szg, w, tm)
    mem_k, mem_v = _memkv(mem_prompt.reshape(B * N_MEM, D), w["g_mem"], w["w_mkv"], min(512, B * N_MEM))
    tb = min(MOE_BLOCK, S)
    h2, xn2, idx, rank, wt, cnt = _cross_prompt(qx, h1, mem_k.reshape(B, N_MEM, MEM_WIDTH),
                                                mem_v.reshape(B, N_MEM, MEM_WIDTH), w, tb)
    y_prompt = _moe(xn2, idx, rank, wt, cnt.reshape(-1, N_EXPERTS), h2, w, tb, tb)

    xs = x_sample.reshape(Bd, D)
    (qcat_s, ckv_s, krope_s, kcat_s, _, gq_s, gk_s, gv_s, sg_s, la_s, szm_s, szg_s) = _premix(
        xs, _rope_table(jnp.full((Bd,), past_len, F32)), w, Bd, F32)
    olat_s = _mla_sample(page_table, qcat_s, kcat_s, cache_kv_latent[0], jnp.swapaxes(cache_k_rope[0], 1, 2))
    og_s, state_s = _gla_sample(gq_s, gk_s, gv_s, la_s, state_gla[0])
    h1_s, qx_s = _postmix(xs, olat_s.reshape(Bd, -1), og_s, sg_s, szm_s, szg_s, w, Bd)
    h2_s, xn2_s, idx_s, rank_s, wt_s, cnt_s = _cross_sample(
        qx_s, h1_s, cache_mem_k[0].reshape(Bd, N_MEM, MEM_WIDTH), cache_mem_v[0].reshape(Bd, N_MEM, MEM_WIDTH), w)
    y_sample = _moe(xn2_s, idx_s, rank_s, wt_s, cnt_s.reshape(-1, N_EXPERTS), h2_s, w, Bd, Bd)

    return (y_prompt.reshape(B, S, D), y_sample.reshape(Bd, 1, D),
            ckv.reshape(1, B, S, KV_LORA), krope.reshape(1, B, S, QK_ROPE), state_p[None],
            mem_k.reshape(1, B, N_MEM, MEM_HEADS, MEM_HEAD_DIM), mem_v.reshape(1, B, N_MEM, MEM_HEADS, MEM_HEAD_DIM),
            ckv_s.reshape(1, Bd, 1, KV_LORA), krope_s.reshape(1, Bd, 1, QK_ROPE), state_s[None])
```

```python
import functools

import jax
import jax.numpy as jnp
import numpy as np
from jax import lax
from jax.experimental import pallas as pl
from jax.experimental.pallas import tpu as pltpu

F32 = jnp.float32
BF16 = jnp.bfloat16

D_MODEL = 1024
MLA_HEADS = 8
Q_LORA = 256
KV_LORA = 128
QK_NOPE = 128
QK_ROPE = 64
ROPE_THETA = 10000.0
ATTN_SCALE = (QK_NOPE + QK_ROPE) ** -0.5
GLA_HEADS = 4
GLA_DK = 128
GLA_DV = 256
GLA_RANK = 16
GLA_TAU = 16.0
GLA_CHUNK = 64
GLA_GROUP = 4
N_MEM = 256
MEM_HEADS = 4
MEM_HEAD_DIM = 128
N_EXPERTS = 32
TOP_K = 4
D_FF = 1024
SWIGLU_LIMIT = 7.0
SWIGLU_ALPHA = 1.702
EPS = 1e-6

GLA_K_WIDTH = GLA_HEADS * GLA_DK
GLA_V_WIDTH = GLA_HEADS * GLA_DV
MEM_WIDTH = MEM_HEADS * MEM_HEAD_DIM
IN_SPLITS = (Q_LORA, KV_LORA, QK_ROPE, GLA_K_WIDTH, GLA_K_WIDTH, GLA_V_WIDTH, GLA_V_WIDTH, GLA_RANK, D_MODEL, D_MODEL)

LANES = 128
_PACKED_WIDTHS = (Q_LORA, KV_LORA, LANES, GLA_K_WIDTH, GLA_K_WIDTH, GLA_V_WIDTH, GLA_V_WIDTH, LANES, D_MODEL, D_MODEL)
_OFF = tuple(int(v) for v in np.cumsum((0,) + _PACKED_WIDTHS))
QCAT = 2 * LANES

VMEM_LIMIT = 56 << 20
PREMIX_ROWS = 512


def _params(n_axes):
    return pltpu.CompilerParams(dimension_semantics=("arbitrary",) * n_axes, vmem_limit_bytes=VMEM_LIMIT)


def _rms(x, g):
    var = jnp.mean(x * x, axis=-1, keepdims=True)
    return x * lax.rsqrt(var + EPS) * g


def _dot(a, b):
    return jnp.dot(a.astype(BF16), b.astype(BF16), preferred_element_type=F32)


def _dot_nt(a, b):
    return lax.dot_general(a.astype(BF16), b.astype(BF16), (((1,), (1,)), ((), ())), preferred_element_type=F32)


def _dot_tn(a, b):
    return lax.dot_general(a.astype(BF16), b.astype(BF16), (((0,), (0,)), ((), ())), preferred_element_type=F32)


def _split3(x):
    hi = x.astype(BF16)
    r1 = x - hi.astype(F32)
    mid = r1.astype(BF16)
    lo = (r1 - mid.astype(F32)).astype(BF16)
    return hi, mid, lo


def _dot_f32_nt(a, b):
    a0, a1, a2 = _split3(a)
    b0, b1, b2 = _split3(b)
    d = lambda u, v: lax.dot_general(u, v, (((1,), (1,)), ((), ())), preferred_element_type=F32)
    return ((d(a1, b1) + d(a0, b2) + d(a2, b0)) + (d(a0, b1) + d(a1, b0))) + d(a0, b0)


def _full(shape):
    n = len(shape)
    return pl.BlockSpec(shape, lambda *_: (0,) * n)


def _premix_kernel(x_ref, tab_ref, g_attn_ref, w_in_ref, g_q_ref, g_kv_ref, w_uq_ref, wuk_ref, w_a2_ref, b_a_ref,
                   qcat_ref, ckv_ref, krope_ref, kcat_ref, kcat_t_ref, gq_ref, gk_ref, gv_ref, sg_ref, la_ref, szm_ref, szg_ref):
    xn = _rms(x_ref[...], g_attn_ref[...]).astype(BF16)

    def proj(i):
        return jnp.dot(xn, w_in_ref[:, _OFF[i]:_OFF[i + 1]], preferred_element_type=F32)

    tab = tab_ref[...]
    cos, sin_lo, sin_hi = tab[:, :LANES], tab[:, LANES:2 * LANES], tab[:, 2 * LANES:]

    def rope(t):
        return t * cos + pltpu.roll(t, LANES - QK_ROPE // 2, 1) * sin_lo + pltpu.roll(t, QK_ROPE // 2, 1) * sin_hi

    c_q = _rms(proj(0), g_q_ref[...]).astype(BF16)
    ckv = _rms(proj(1), g_kv_ref[...])
    kr = rope(proj(2))
    ckv_ref[...] = ckv
    krope_ref[...] = kr[:, :QK_ROPE]
    kcat = jnp.concatenate([ckv, kr], axis=1)
    kcat_ref[...] = kcat.astype(BF16)
    tk = kcat_t_ref.shape[2]
    for t in range(kcat_t_ref.shape[0]):
        kcat_t_ref[t] = kcat[t * tk:(t + 1) * tk].T.astype(BF16)
    for h in range(MLA_HEADS):
        q_nope = jnp.dot(c_q, w_uq_ref[:, h * LANES:(h + 1) * LANES], preferred_element_type=F32)
        q_lat = _dot(q_nope, wuk_ref[h])
        q_rope = rope(jnp.dot(c_q, w_uq_ref[:, (MLA_HEADS + h) * LANES:(MLA_HEADS + h + 1) * LANES],
                              preferred_element_type=F32))
        qcat_ref[h, :, :LANES] = q_lat.astype(qcat_ref.dtype)
        qcat_ref[h, :, LANES:] = q_rope.astype(qcat_ref.dtype)
    gq_ref[...] = (proj(3) * GLA_DK ** -0.5).astype(gq_ref.dtype)
    gk_ref[...] = proj(4).astype(gk_ref.dtype)
    gv_ref[...] = proj(5).astype(gv_ref.dtype)
    gg = proj(6)
    sg_ref[...] = (gg * jax.nn.sigmoid(gg)).astype(sg_ref.dtype)
    z = _dot(proj(7), w_a2_ref[...]) + b_a_ref[...]
    la_ref[...] = -(jnp.maximum(-z, 0.0) + jnp.log1p(jnp.exp(-jnp.abs(z)))) * (1.0 / GLA_TAU)
    szm_ref[...] = jax.nn.sigmoid(proj(8)).astype(szm_ref.dtype)
    szg_ref[...] = jax.nn.sigmoid(proj(9)).astype(szg_ref.dtype)


def _premix(x, tab, w, tm, tk, gdtype):
    T = x.shape[0]
    n_tab = tab.shape[0] // tm
    row = lambda n: pl.BlockSpec((tm, n), lambda i: (i, 0))
    outs = [(KV_LORA, F32), (QK_ROPE, F32), (QCAT, BF16), None, (GLA_K_WIDTH, gdtype),
            (GLA_K_WIDTH, gdtype), (GLA_V_WIDTH, gdtype), (GLA_V_WIDTH, BF16), (GLA_K_WIDTH, F32),
            (D_MODEL, BF16), (D_MODEL, BF16)]
    out_specs = [row(o[0]) if o else pl.BlockSpec((tm // tk, QCAT, tk), lambda i: (i, 0, 0)) for o in outs]
    out_shape = [jax.ShapeDtypeStruct((T, o[0]) if o else (T // tk, QCAT, tk), o[1] if o else BF16) for o in outs]
    consts = [w["g_attn"], w["w_in"], w["g_q"], w["g_kv"], w["w_uq"], w["wuk"], w["w_a2"], w["b_a"]]
    const_specs = [pl.BlockSpec(c.shape, lambda i, n=len(c.shape): (0,) * n,
                                pipeline_mode=pl.Buffered(1) if c is w["w_in"] else None) for c in consts]
    return pl.pallas_call(
        _premix_kernel,
        grid=(T // tm,),
        in_specs=[row(D_MODEL), pl.BlockSpec((tm, 3 * LANES), lambda i: (i % n_tab, 0))] + const_specs,
        out_specs=[pl.BlockSpec((MLA_HEADS, tm, QCAT), lambda i: (0, i, 0))] + out_specs,
        out_shape=[jax.ShapeDtypeStruct((MLA_HEADS, T, QCAT), BF16)] + out_shape,
        compiler_params=_params(1),
        name="premix",
    )(x, tab, *consts)


def _mla_prompt_kernel(q_ref, k_ref, kt_ref, o_ref, m_ref, l_ref, acc_ref):
    tq = q_ref.shape[1]
    qi = pl.program_id(1)
    q = q_ref[...].reshape(MLA_HEADS * tq, QCAT)
    row = lax.broadcasted_iota(jnp.int32, (MLA_HEADS, tq, tq), 1).reshape(MLA_HEADS * tq, tq)
    col = lax.broadcasted_iota(jnp.int32, (MLA_HEADS * tq, tq), 1)
    m_ref[...] = jnp.full_like(m_ref, -jnp.inf)
    l_ref[...] = jnp.zeros_like(l_ref)
    acc_ref[...] = jnp.zeros_like(acc_ref)
    c = ATTN_SCALE * np.log2(np.e)

    def block(j, masked):
        keys = pl.ds(pl.multiple_of(j * tq, tq), tq)
        s = jnp.dot(q, kt_ref[j], preferred_element_type=F32)
        if masked:
            s = jnp.where(col <= row, s, -jnp.inf)
        m = m_ref[...]
        m_new = jnp.maximum(m, jnp.max(s, axis=-1, keepdims=True))
        alpha = jnp.exp2((m - m_new) * c)
        p = jnp.exp2((s - jnp.concatenate([m_new] * (tq // LANES), axis=1)) * c)
        p_lanes = p[:, :LANES]
        for t in range(1, tq // LANES):
            p_lanes = p_lanes + p[:, t * LANES:(t + 1) * LANES]
        l_ref[...] = alpha * l_ref[...] + p_lanes
        acc_ref[...] = alpha * acc_ref[...] + _dot(p, k_ref[keys, :KV_LORA])
        m_ref[...] = m_new

    def body(j, carry):
        block(j, False)
        return carry

    lax.fori_loop(0, qi, body, 0)
    block(qi, True)
    o = acc_ref[...] / jnp.sum(l_ref[...], axis=-1, keepdims=True)
    for h in range(MLA_HEADS):
        o_ref[:, h * KV_LORA:(h + 1) * KV_LORA] = o[h * tq:(h + 1) * tq].astype(o_ref.dtype)


def _mla_prompt(qcat, kcat, kcat_t, tq):
    _, B, S, _ = qcat.shape
    rows = MLA_HEADS * tq
    return pl.pallas_call(
        _mla_prompt_kernel,
        grid=(B, S // tq),
        in_specs=[pl.BlockSpec((MLA_HEADS, None, tq, QCAT), lambda b, i: (0, b, i, 0)),
                  pl.BlockSpec((None, S, QCAT), lambda b, i: (b, 0, 0)),
                  pl.BlockSpec((S // tq, QCAT, tq), lambda b, i: (b, 0, 0))],
        out_specs=pl.BlockSpec((None, tq, MLA_HEADS * KV_LORA), lambda b, i: (b, i, 0)),
        out_shape=jax.ShapeDtypeStruct((B, S, MLA_HEADS * KV_LORA), BF16),
        scratch_shapes=[pltpu.VMEM((rows, LANES), F32), pltpu.VMEM((rows, LANES), F32), pltpu.VMEM((rows, KV_LORA), F32)],
        compiler_params=_params(2),
        name="mla_prompt",
    )(qcat, kcat, kcat_t)


def _mla_sample_kernel(pt_ref, q_ref, knew_ref, lat_hbm, rope_hbm, o_ref, lat_buf, rope_buf, lat_bf, sems):
    b = pl.program_id(0)
    n_pages, page = lat_buf.shape[1], lat_buf.shape[2]

    def page_copies(sample, slot, i):
        pg = pt_ref[sample, i]
        return (pltpu.make_async_copy(lat_hbm.at[pg], lat_buf.at[slot, i], sems.at[0, slot]),
                pltpu.make_async_copy(rope_hbm.at[pg], rope_buf.at[slot, :, pl.ds(pl.multiple_of(i * page, page), page)],
                                      sems.at[1, slot]))

    def for_pages(sample, slot, act):
        def body(i, carry):
            for cp in page_copies(sample, slot, i):
                act(cp)
            return carry
        lax.fori_loop(0, n_pages, body, 0)

    slot = b % 2

    @pl.when(b == 0)
    def _():
        for_pages(0, 0, lambda cp: cp.start())

    @pl.when(b + 1 < pl.num_programs(0))
    def _():
        for_pages(b + 1, 1 - slot, lambda cp: cp.start())

    for_pages(b, slot, lambda cp: cp.wait())

    lat_bf[...] = lat_buf[slot].reshape(n_pages * page, KV_LORA).astype(BF16)
    q = q_ref[...].reshape(MLA_HEADS, QCAT)
    knew = knew_ref[...].astype(F32)
    s = (_dot_nt(q[:, :KV_LORA], lat_bf[...]) + _dot(q[:, KV_LORA:KV_LORA + QK_ROPE], rope_buf[slot])) * ATTN_SCALE
    s_new = jnp.sum(q.astype(F32) * knew, axis=-1, keepdims=True) * ATTN_SCALE
    m = jnp.maximum(jnp.max(s, axis=-1, keepdims=True), s_new)
    p = jnp.exp(s - m)
    p_new = jnp.exp(s_new - m)
    denom = jnp.sum(p, axis=-1, keepdims=True) + p_new
    o = _dot(p, lat_bf[...]) + p_new * knew[:, :KV_LORA]
    o_ref[...] = (o / denom).astype(o_ref.dtype)


def _mla_sample(page_table, qcat, knew, cache_lat, cache_rope_t):
    Bd, n_pages = page_table.shape
    page = cache_lat.shape[1]
    grid_spec = pltpu.PrefetchScalarGridSpec(
        num_scalar_prefetch=1,
        grid=(Bd,),
        in_specs=[pl.BlockSpec((MLA_HEADS, None, 1, QCAT), lambda b, pt: (0, b, 0, 0)),
                  pl.BlockSpec((None, 1, QCAT), lambda b, pt: (b, 0, 0)),
                  pl.BlockSpec(memory_space=pl.ANY), pl.BlockSpec(memory_space=pl.ANY)],
        out_specs=pl.BlockSpec((None, MLA_HEADS, KV_LORA), lambda b, pt: (b, 0, 0)),
        scratch_shapes=[pltpu.VMEM((2, n_pages, page, KV_LORA), F32), pltpu.VMEM((2, QK_ROPE, n_pages * page), F32),
                        pltpu.VMEM((n_pages * page, KV_LORA), BF16), pltpu.SemaphoreType.DMA((2, 2))],
    )
    return pl.pallas_call(
        _mla_sample_kernel,
        grid_spec=grid_spec,
        out_shape=jax.ShapeDtypeStruct((Bd, MLA_HEADS, KV_LORA), BF16),
        compiler_params=_params(1),
        name="mla_sample",
    )(page_table, qcat.reshape(MLA_HEADS, Bd, 1, QCAT), knew.reshape(Bd, 1, QCAT), cache_lat, cache_rope_t)


def _gla_prompt_kernel(q_ref, k_ref, v_ref, la_ref, o_ref, state_ref, st_ref):
    C = GLA_CHUNK
    S = q_ref.shape[0]
    row = lax.broadcasted_iota(jnp.int32, (C, 2 * C), 0)
    col = lax.broadcasted_iota(jnp.int32, (C, 2 * C), 1)
    causal = row >= col
    tri = causal[:, :C].astype(BF16)
    pad_k = jnp.zeros((C, GLA_DK), F32)
    pad_v = jnp.zeros((C, GLA_DV), F32)
    st_ref[...] = jnp.zeros_like(st_ref)

    group = min(GLA_GROUP, S // C)

    def chunks(c, carry):
        parts = []
        for g in range(group):
            rows = pl.ds(pl.multiple_of((c * group + g) * C, C), C)
            split = jnp.dot(tri, jnp.concatenate(_split3(la_ref[rows, :]), axis=1), preferred_element_type=F32)
            b_all = (split[:, 2 * GLA_K_WIDTH:] + split[:, GLA_K_WIDTH:2 * GLA_K_WIDTH]) + split[:, :GLA_K_WIDTH]
            for h in range(GLA_HEADS):
                kcols = slice(h * GLA_DK, (h + 1) * GLA_DK)
                vcols = slice(h * GLA_DV, (h + 1) * GLA_DV)
                b = b_all[:, kcols]
                b_last = b[C - 1:C, :]
                q = q_ref[rows, kcols].astype(F32)
                k = k_ref[rows, kcols].astype(F32)
                q_in = (q * jnp.exp(b)).astype(BF16)
                k_in = jnp.concatenate([k * jnp.exp(-b), pad_k], axis=0)
                k_out = jnp.concatenate([k * jnp.exp(b_last - b), pad_k], axis=0)
                a = jnp.where(causal, _dot_nt(q_in, k_in), 0.0)
                v_t = jnp.concatenate([v_ref[rows, vcols].astype(F32), pad_v], axis=0).T.astype(BF16)
                lhs = jnp.concatenate([q_in, a.astype(BF16)], axis=1)
                parts.append((rows, h, vcols, lhs, v_t, jnp.exp(b_last), _dot(v_t, k_out)))
        for rows, h, vcols, lhs, v_t, decay, kv_t in parts:
            st = st_ref[h]
            o_ref[rows, vcols] = _dot_nt(lhs, jnp.concatenate([st.astype(BF16), v_t], axis=1)).astype(o_ref.dtype)
            st_ref[h] = st * decay + kv_t
        return carry

    lax.fori_loop(0, S // C // group, chunks, 0)
    for h in range(GLA_HEADS):
        state_ref[h] = st_ref[h].T


def _gla_prompt(q, k, v, la):
    B, S, _ = q.shape
    seq = lambda n: pl.BlockSpec((None, S, n), lambda b: (b, 0, 0))
    return pl.pallas_call(
        _gla_prompt_kernel,
        grid=(B,),
        in_specs=[seq(GLA_K_WIDTH), seq(GLA_K_WIDTH), seq(GLA_V_WIDTH), seq(GLA_K_WIDTH)],
        out_specs=[seq(GLA_V_WIDTH), pl.BlockSpec((None, GLA_HEADS, GLA_DK, GLA_DV), lambda b: (b, 0, 0, 0))],
        out_shape=[jax.ShapeDtypeStruct((B, S, GLA_V_WIDTH), BF16),
                   jax.ShapeDtypeStruct((B, GLA_HEADS, GLA_DK, GLA_DV), F32)],
        scratch_shapes=[pltpu.VMEM((GLA_HEADS, GLA_DV, GLA_DK), F32)],
        compiler_params=_params(1),
        name="gla_prompt",
    )(q, k, v, la)


def _gla_sample_kernel(q_ref, k_ref, v_ref, la_ref, st_ref, o_ref, sto_ref):
    nb = q_ref.shape[0]
    pad = jnp.zeros((GLA_DK - nb, GLA_DK), F32)
    for h in range(GLA_HEADS):
        kcols = slice(h * GLA_DK, (h + 1) * GLA_DK)
        vcols = slice(h * GLA_DV, (h + 1) * GLA_DV)
        col = lambda ref: jnp.concatenate([ref[:, kcols].astype(F32), pad], axis=0).T
        q_t, k_t, decay_t = col(q_ref), col(k_ref), jnp.exp(col(la_ref))
        for i in range(nb):
            new = decay_t[:, i:i + 1] * st_ref[i, h] + k_t[:, i:i + 1] * v_ref[i:i + 1, vcols].astype(F32)
            sto_ref[i, h] = new
            o_ref[i:i + 1, vcols] = jnp.sum(q_t[:, i:i + 1] * new, axis=0, keepdims=True).astype(o_ref.dtype)


def _gla_sample(q, k, v, la, state, nb=8):
    Bd = q.shape[0]
    rows = lambda n: pl.BlockSpec((nb, n), lambda i: (i, 0))
    st_spec = pl.BlockSpec((nb, GLA_HEADS, GLA_DK, GLA_DV), lambda i: (i, 0, 0, 0))
    return pl.pallas_call(
        _gla_sample_kernel,
        grid=(Bd // nb,),
        in_specs=[rows(GLA_K_WIDTH), rows(GLA_K_WIDTH), rows(GLA_V_WIDTH), rows(GLA_K_WIDTH), st_spec],
        out_specs=[rows(GLA_V_WIDTH), st_spec],
        out_shape=[jax.ShapeDtypeStruct((Bd, GLA_V_WIDTH), BF16), jax.ShapeDtypeStruct(state.shape, F32)],
        compiler_params=_params(1),
        name="gla_sample",
    )(q, k, v, la, state)


def _postmix_kernel(x_ref, olat_ref, og_ref, sg_ref, szm_ref, szg_ref, wuv_ref, w_o_mla_ref, g_gla_ref, w_o_gla_ref,
                    w_out_ref, g_cross_ref, w_mq_ref, h_ref, qx_ref):
    ov = jnp.concatenate(
        [_dot(olat_ref[:, h * KV_LORA:(h + 1) * KV_LORA], wuv_ref[h]).astype(BF16) for h in range(MLA_HEADS)], axis=1)
    o_mla = jnp.dot(ov, w_o_mla_ref[...], preferred_element_type=F32)
    g_gla = g_gla_ref[...]
    og = jnp.concatenate(
        [(_rms(og_ref[:, h * GLA_DV:(h + 1) * GLA_DV].astype(F32), g_gla)
          * sg_ref[:, h * GLA_DV:(h + 1) * GLA_DV].astype(F32)).astype(BF16) for h in range(GLA_HEADS)], axis=1)
    o_gla = jnp.dot(og, w_o_gla_ref[...], preferred_element_type=F32)
    merged = szm_ref[...].astype(F32) * o_mla + szg_ref[...].astype(F32) * o_gla
    h1 = x_ref[...] + _dot(merged, w_out_ref[...])
    h_ref[...] = h1
    qx_ref[...] = _dot(_rms(h1, g_cross_ref[...]), w_mq_ref[...]).astype(qx_ref.dtype)


def _postmix(x, olat, og, sg, szm, szg, w, tm):
    T = x.shape[0]
    row = lambda n: pl.BlockSpec((tm, n), lambda i: (i, 0))
    consts = [w["wuv"], w["w_o_mla"], w["g_gla"], w["w_o_gla"], w["w_out"], w["g_cross"], w["w_mq"]]
    return pl.pallas_call(
        _postmix_kernel,
        grid=(T // tm,),
        in_specs=[row(D_MODEL)] * 6 + [_full(c.shape) for c in consts],
        out_specs=[row(D_MODEL), row(MEM_WIDTH)],
        out_shape=[jax.ShapeDtypeStruct((T, D_MODEL), F32), jax.ShapeDtypeStruct((T, MEM_WIDTH), BF16)],
        compiler_params=_params(1),
        name="postmix",
    )(x, olat, og, sg, szm, szg, *consts)


def _memkv_kernel(mem_ref, g_ref, w_ref, k_ref, v_ref):
    kv = _dot(_rms(mem_ref[...], g_ref[...]), w_ref[...])
    k_ref[...] = kv[:, :MEM_WIDTH]
    v_ref[...] = kv[:, MEM_WIDTH:]


def _memkv(mem, g_mem, w_mkv, tm=512):
    T = mem.shape[0]
    row = lambda n: pl.BlockSpec((tm, n), lambda i: (i, 0))
    return pl.pallas_call(
        _memkv_kernel,
        grid=(T // tm,),
        in_specs=[row(D_MODEL), _full(g_mem.shape), _full(w_mkv.shape)],
        out_specs=[row(MEM_WIDTH), row(MEM_WIDTH)],
        out_shape=[jax.ShapeDtypeStruct((T, MEM_WIDTH), F32)] * 2,
        compiler_params=_params(1),
        name="memkv",
    )(mem, g_mem, w_mkv)


def _attend_memory(q, head_k, head_v):
    outs = []
    for h in range(MEM_HEADS):
        cols = slice(h * MEM_HEAD_DIM, (h + 1) * MEM_HEAD_DIM)
        s = _dot_nt(q[:, cols], head_k(h)) * MEM_HEAD_DIM ** -0.5
        p = jnp.exp(s - jnp.max(s, axis=-1, keepdims=True))
        p = p / jnp.sum(p, axis=-1, keepdims=True)
        outs.append(_dot(p, head_v(h)).astype(BF16))
    return jnp.concatenate(outs, axis=1)


def _route(o, h1, w_mo_ref, g_ffn_ref, w_router_ref, b_router_ref, h_ref, xn_ref, idx_ref, rank_ref, wt_ref, cnt_ref):
    h2 = h1 + jnp.dot(o, w_mo_ref[...], preferred_element_type=F32)
    h_ref[...] = h2
    xn = _rms(h2, g_ffn_ref[...])
    xn_ref[...] = xn.astype(xn_ref.dtype)
    tb = xn.shape[0]
    logits = _dot_f32_nt(w_router_ref[...], xn) + b_router_ref[...]
    expert = lax.broadcasted_iota(jnp.int32, logits.shape, 0)
    work = logits
    hits, firsts, exps = [], [], []
    top = None
    for _ in range(TOP_K):
        best = jnp.max(work, axis=0, keepdims=True)
        first = jnp.min(jnp.where(work == best, expert, N_EXPERTS), axis=0, keepdims=True)
        hit = expert == first
        top = best if top is None else top
        hits.append(hit)
        firsts.append(first)
        exps.append(jnp.exp(best - top))
        work = jnp.where(hit, -jnp.inf, work)
    denom = (exps[0] + exps[1]) + (exps[2] + exps[3])
    chosen = jnp.zeros(logits.shape, F32)
    for hit in hits:
        chosen = chosen + jnp.where(hit, 1.0, 0.0)
    before = (lax.broadcasted_iota(jnp.int32, (tb, tb), 0) < lax.broadcasted_iota(jnp.int32, (tb, tb), 1)).astype(BF16)
    rank = jnp.dot(chosen.astype(BF16), before, preferred_element_type=F32)
    idx_ref[...] = jnp.concatenate(firsts, axis=0)
    rank_ref[...] = jnp.concatenate(
        [jnp.sum(jnp.where(hit, rank, 0.0), axis=0, keepdims=True) for hit in hits], axis=0).astype(jnp.int32)
    wt_ref[...] = jnp.concatenate([e / denom for e in exps], axis=0)
    cnt_ref[...] = jnp.sum(chosen, axis=1, keepdims=True).astype(jnp.int32)


def _cross_prompt_kernel(qx_ref, h1_ref, mk_ref, mv_ref, *rest):
    head = lambda ref: lambda h: ref[:, h * MEM_HEAD_DIM:(h + 1) * MEM_HEAD_DIM]
    o = _attend_memory(qx_ref[...], head(mk_ref), head(mv_ref))
    _route(o, h1_ref[...], *rest)


def _cross_sample_kernel(qx_ref, mk_ref, mv_ref, o_ref):
    for i in range(qx_ref.shape[0]):
        q = jnp.broadcast_to(qx_ref[i:i + 1, :], (8, MEM_WIDTH))
        head = lambda ref: lambda h: ref[i, :, h, :]
        o_ref[i:i + 1, :] = _attend_memory(q, head(mk_ref), head(mv_ref))[:1]


def _route_kernel(o_ref, h1_ref, *rest):
    _route(o_ref[...], h1_ref[...], *rest)


def _route_specs(T, tb, w):
    consts = [w["w_mo"], w["g_ffn"], w["w_router_t"], w["b_router"]]
    row = lambda n: pl.BlockSpec((tb, n), lambda i: (i, 0))
    per_choice = pl.BlockSpec((TOP_K, tb), lambda i: (0, i))
    out_specs = [row(D_MODEL), row(D_MODEL), per_choice, per_choice, per_choice,
                 pl.BlockSpec((None, N_EXPERTS, 1), lambda i: (i, 0, 0))]
    out_shape = [jax.ShapeDtypeStruct((T, D_MODEL), F32), jax.ShapeDtypeStruct((T, D_MODEL), BF16),
                 jax.ShapeDtypeStruct((TOP_K, T), jnp.int32), jax.ShapeDtypeStruct((TOP_K, T), jnp.int32),
                 jax.ShapeDtypeStruct((TOP_K, T), F32), jax.ShapeDtypeStruct((T // tb, N_EXPERTS, 1), jnp.int32)]
    return consts, out_specs, out_shape


def _cross_prompt(qx, h1, mem_k, mem_v, w, tb):
    T = qx.shape[0]
    blocks_per_mem = T // mem_k.shape[0] // tb
    consts, out_specs, out_shape = _route_specs(T, tb, w)
    row = lambda n: pl.BlockSpec((tb, n), lambda i: (i, 0))
    mem_spec = pl.BlockSpec((None, N_MEM, MEM_WIDTH), lambda i: (i // blocks_per_mem, 0, 0))
    return pl.pallas_call(
        _cross_prompt_kernel,
        grid=(T // tb,),
        in_specs=[row(MEM_WIDTH), row(D_MODEL), mem_spec, mem_spec] + [_full(c.shape) for c in consts],
        out_specs=out_specs,
        out_shape=out_shape,
        compiler_params=_params(1),
        name="cross_prompt",
    )(qx, h1, mem_k, mem_v, *consts)


def _cross_sample(qx, h1, mem_k, mem_v, w, nb=8):
    T = qx.shape[0]
    mem_spec = pl.BlockSpec((None, nb, N_MEM, MEM_HEADS, MEM_HEAD_DIM), lambda i: (0, i, 0, 0, 0))
    rows = pl.BlockSpec((nb, MEM_WIDTH), lambda i: (i, 0))
    o = pl.pallas_call(
        _cross_sample_kernel,
        grid=(T // nb,),
        in_specs=[rows, mem_spec, mem_spec],
        out_specs=rows,
        out_shape=jax.ShapeDtypeStruct((T, MEM_WIDTH), BF16),
        compiler_params=_params(1),
        name="cross_sample",
    )(qx, mem_k, mem_v)
    consts, out_specs, out_shape = _route_specs(T, T, w)
    return pl.pallas_call(
        _route_kernel,
        grid=(1,),
        in_specs=[_full(o.shape), _full(h1.shape)] + [_full(c.shape) for c in consts],
        out_specs=out_specs,
        out_shape=out_shape,
        compiler_params=_params(1),
        name="route_sample",
    )(o, h1, *consts)


SEG_ALIGN = 16
SEL_CHUNK = 256
MOE_BLOCK = 512


def _block_rows(tb):
    worst = TOP_K * tb + N_EXPERTS * (SEG_ALIGN - 1)
    return -(-worst // SEL_CHUNK) * SEL_CHUNK


def _seg_sizes(limit):
    sizes, b = [], SEG_ALIGN
    while b <= limit:
        sizes.append(b)
        b *= 2
    return sizes[::-1]


def _for_segments(n, sizes, make_copy, act):
    def emit(group):
        for size in group:
            @pl.when((n & size) != 0)
            def _():
                act(make_copy(pl.multiple_of(n & (-2 * size), SEG_ALIGN), size))

    large = [size for size in sizes if size >= 8 * SEG_ALIGN]
    if len(large) > 1:
        pl.when(n >= large[-1])(lambda: emit(large))
    else:
        emit(large)
    emit([size for size in sizes if size < 8 * SEG_ALIGN])


def _slot_rows(idx, rank, loc_ref, j):
    pos = rank
    for e in range(N_EXPERTS):
        pos = pos + jnp.where(idx == e, loc_ref[j, e], 0)
    return pos


def _dispatch_kernel(loc_ref, pad_ref, goff_ref, total_ref, gap_off_ref, gap_len_ref, x_ref, idx_ref, rank_ref, xs_hbm,
                     buf, sem, *, tm):
    j = pl.program_id(0)
    last = pl.num_programs(0) - 1
    slot = j % 2
    tb = x_ref.shape[0]
    r_blk = _block_rows(tb)
    pos = _slot_rows(idx_ref[...], rank_ref[...], loc_ref, j)
    x = x_ref[...]
    for c in range(r_blk // SEL_CHUNK):
        r = lax.broadcasted_iota(jnp.int32, (SEL_CHUNK, tb), 0) + c * SEL_CHUNK
        sel = jnp.where(r == pos[0:1], 1.0, jnp.where(r == pos[1:2], 1.0, jnp.where(
            r == pos[2:3], 1.0, jnp.where(r == pos[3:4], 1.0, 0.0))))
        buf[slot, c * SEL_CHUNK:(c + 1) * SEL_CHUNK, :] = jnp.dot(
            sel.astype(BF16), x, preferred_element_type=F32).astype(BF16)

    sizes = _seg_sizes(tb)

    def start_segments(e, carry):
        src, dst = loc_ref[j, e], goff_ref[j, e]
        _for_segments(pad_ref[j, e], sizes, lambda off, size: pltpu.make_async_copy(
            buf.at[slot, pl.ds(pl.multiple_of(src + off, SEG_ALIGN), size)],
            xs_hbm.at[pl.ds(pl.multiple_of(dst + off, SEG_ALIGN), size)], sem.at[slot]), lambda cp: cp.start())
        return carry

    lax.fori_loop(0, N_EXPERTS, start_segments, 0)

    def wait_block(blk, slot):
        _for_segments(total_ref[blk], _seg_sizes(r_blk), lambda off, size: pltpu.make_async_copy(
            buf.at[slot, pl.ds(0, size)], xs_hbm.at[pl.ds(0, size)], sem.at[slot]), lambda cp: cp.wait())

    @pl.when(j > 0)
    def _():
        wait_block(j - 1, 1 - slot)

    @pl.when(j == last)
    def _():
        wait_block(j, slot)
        buf[1 - slot, 0:tm, :] = jnp.zeros((tm, D_MODEL), BF16)
        gap_sizes = _seg_sizes(tm - SEG_ALIGN)

        def for_gaps(act):
            def body(e, carry):
                dst = gap_off_ref[e]
                _for_segments(gap_len_ref[e], gap_sizes, lambda off, size: pltpu.make_async_copy(
                    buf.at[1 - slot, pl.ds(0, size)], xs_hbm.at[pl.ds(pl.multiple_of(dst + off, SEG_ALIGN), size)],
                    sem.at[1 - slot]), act)
                return carry
            lax.fori_loop(0, N_EXPERTS, body, 0)

        for_gaps(lambda cp: cp.start())
        for_gaps(lambda cp: cp.wait())


def _ffn_kernel(tile_expert_ref, n_valid_ref, x_ref, wgu_f32_ref, bgu_ref, wd_f32_ref, bd_ref, o_ref, wgu_ref, wd_ref):
    t = pl.program_id(0)

    @pl.when(t < n_valid_ref[0])
    def _():
        @pl.when((t == 0) | (tile_expert_ref[t] != tile_expert_ref[jnp.maximum(t - 1, 0)]))
        def _():
            wgu_ref[...] = wgu_f32_ref[...].astype(BF16)
            wd_ref[...] = wd_f32_ref[...].astype(BF16)

        x = x_ref[...]
        bgu = bgu_ref[...]
        gate = jnp.dot(x, wgu_ref[:, :D_FF], preferred_element_type=F32) + bgu[:, :D_FF]
        up = jnp.dot(x, wgu_ref[:, D_FF:], preferred_element_type=F32) + bgu[:, D_FF:]
        gate = jnp.minimum(gate, SWIGLU_LIMIT)
        up = jnp.clip(up, -SWIGLU_LIMIT, SWIGLU_LIMIT)
        hidden = (up + 1.0) * gate * jax.nn.sigmoid(SWIGLU_ALPHA * gate)
        o_ref[...] = (_dot(hidden, wd_ref[...]) + bd_ref[...]).astype(o_ref.dtype)


def _combine_kernel(loc_ref, pad_ref, goff_ref, total_ref, idx0_ref, rank0_ref, wt0_ref, idx_ref, rank_ref, wt_ref,
                    h_ref, g_final_ref, o_hbm, y_ref, buf, sel_ref, sem):
    j = pl.program_id(0)
    last = pl.num_programs(0) - 1
    slot = j % 2
    tb = h_ref.shape[0]
    r_blk = buf.shape[1]
    sizes = _seg_sizes(tb)

    def fetch(blk, slot):
        def body(e, carry):
            dst, src = loc_ref[blk, e], goff_ref[blk, e]
            _for_segments(pad_ref[blk, e], sizes, lambda off, size: pltpu.make_async_copy(
                o_hbm.at[pl.ds(pl.multiple_of(src + off, SEG_ALIGN), size)],
                buf.at[slot, pl.ds(pl.multiple_of(dst + off, SEG_ALIGN), size)], sem.at[slot]), lambda cp: cp.start())
            return carry
        lax.fori_loop(0, N_EXPERTS, body, 0)

    def build(blk, slot, idx, rank, wt):
        pos = _slot_rows(idx, rank, loc_ref, blk)
        for c in range(r_blk // SEL_CHUNK):
            r = lax.broadcasted_iota(jnp.int32, (SEL_CHUNK, tb), 0) + c * SEL_CHUNK
            sel = jnp.zeros((SEL_CHUNK, tb), F32)
            for k in range(TOP_K):
                sel = jnp.where(r == pos[k:k + 1], wt[k:k + 1], sel)
            sel_ref[slot, c * SEL_CHUNK:(c + 1) * SEL_CHUNK, :] = sel.astype(BF16)

    @pl.when(j == 0)
    def _():
        buf[...] = jnp.zeros_like(buf)
        fetch(0, 0)
        build(0, 0, idx0_ref[...], rank0_ref[...], wt0_ref[...])

    @pl.when(j < last)
    def _():
        fetch(j + 1, 1 - slot)

    _for_segments(total_ref[j], _seg_sizes(r_blk), lambda off, size: pltpu.make_async_copy(
        o_hbm.at[pl.ds(0, size)], buf.at[slot, pl.ds(0, size)], sem.at[slot]), lambda cp: cp.wait())
    moe = _dot_tn(sel_ref[slot], buf[slot])
    y_ref[...] = _rms(h_ref[...] + moe, g_final_ref[...])
    build(jnp.minimum(j + 1, last), 1 - slot, idx_ref[...], rank_ref[...], wt_ref[...])


def _moe(xn, idx, rank, wt, cnt, h2, w, tb, tm):
    T = xn.shape[0]
    nb = T // tb
    r_blk = _block_rows(tb)
    i32 = jnp.int32
    pad = (cnt + (SEG_ALIGN - 1)) // SEG_ALIGN * SEG_ALIGN
    loc = jnp.cumsum(pad, axis=1) - pad
    seg = jnp.sum(pad, axis=0)
    region = (seg + (tm - 1)) // tm * tm
    region_end = jnp.cumsum(region)
    region_start = region_end - region
    goff = region_start[None, :] + jnp.cumsum(pad, axis=0) - pad
    n_tiles = -(-(nb * (TOP_K * tb + N_EXPERTS * (SEG_ALIGN - 1)) + N_EXPERTS * (tm - 1)) // tm)
    n_valid = (region_end[-1] // tm).astype(i32).reshape(1)
    tile = jnp.minimum(jnp.arange(n_tiles, dtype=i32), n_valid - 1)
    tile_expert = jnp.minimum(jnp.sum((region_end[None, :] <= (tile * tm)[:, None]).astype(i32), axis=1), N_EXPERTS - 1)
    plan = [a.astype(i32) for a in (loc, pad, goff, jnp.sum(pad, axis=1))]
    gaps = [(region_start + seg).astype(i32), (region - seg).astype(i32)]

    choice = pl.BlockSpec((TOP_K, tb), lambda j, *_: (0, j))
    xs = pl.pallas_call(
        functools.partial(_dispatch_kernel, tm=tm),
        grid_spec=pltpu.PrefetchScalarGridSpec(
            num_scalar_prefetch=6, grid=(nb,),
            in_specs=[pl.BlockSpec((tb, D_MODEL), lambda j, *_: (j, 0)), choice, choice],
            out_specs=pl.BlockSpec(memory_space=pl.ANY),
            scratch_shapes=[pltpu.VMEM((2, r_blk, D_MODEL), BF16), pltpu.SemaphoreType.DMA((2,))]),
        out_shape=jax.ShapeDtypeStruct((n_tiles * tm, D_MODEL), BF16),
        compiler_params=_params(1),
        name="moe_dispatch",
    )(*plan, *gaps, xn, idx, rank)

    rows = pl.BlockSpec((tm, D_MODEL), lambda t, te, nv: (jnp.minimum(t, nv[0] - 1), 0))
    per_e = lambda a, b: pl.BlockSpec((None, a, b), lambda t, te, nv: (te[t], 0, 0))
    out = pl.pallas_call(
        _ffn_kernel,
        grid_spec=pltpu.PrefetchScalarGridSpec(
            num_scalar_prefetch=2, grid=(n_tiles,),
            in_specs=[rows, per_e(D_MODEL, 2 * D_FF), per_e(1, 2 * D_FF), per_e(D_FF, D_MODEL), per_e(1, D_MODEL)],
            out_specs=rows,
            scratch_shapes=[pltpu.VMEM((D_MODEL, 2 * D_FF), BF16), pltpu.VMEM((D_FF, D_MODEL), BF16)]),
        out_shape=jax.ShapeDtypeStruct((n_tiles * tm, D_MODEL), BF16),
        compiler_params=_params(1),
        name="moe_ffn",
    )(tile_expert, n_valid, xs, w["w_gate_up"], w["b_gate_up"], w["w_down"], w["b_down"])

    first = pl.BlockSpec((TOP_K, tb), lambda j, *_: (0, 0))
    ahead = pl.BlockSpec((TOP_K, tb), lambda j, *_: (0, jnp.minimum(j + 1, nb - 1)))
    return pl.pallas_call(
        _combine_kernel,
        grid_spec=pltpu.PrefetchScalarGridSpec(
            num_scalar_prefetch=4, grid=(nb,),
            in_specs=[first, first, first, ahead, ahead, ahead, pl.BlockSpec((tb, D_MODEL), lambda j, *_: (j, 0)),
                      pl.BlockSpec((1, D_MODEL), lambda j, *_: (0, 0)), pl.BlockSpec(memory_space=pl.ANY)],
            out_specs=pl.BlockSpec((tb, D_MODEL), lambda j, *_: (j, 0)),
            scratch_shapes=[pltpu.VMEM((2, r_blk, D_MODEL), BF16), pltpu.VMEM((2, r_blk, tb), BF16),
                            pltpu.SemaphoreType.DMA((2,))]),
        out_shape=jax.ShapeDtypeStruct((T, D_MODEL), F32),
        compiler_params=_params(1),
        name="moe_combine",
    )(*plan, idx, rank, wt, idx, rank, wt, h2, w["g_final"], out)


def _rope_table(pos):
    half = QK_ROPE // 2
    inv = ROPE_THETA ** (-jnp.arange(half, dtype=F32) * 2.0 / QK_ROPE)
    ang = pos[:, None] * inv[None, :]
    cos, sin, zero = jnp.cos(ang), jnp.sin(ang), jnp.zeros_like(ang)
    return jnp.concatenate([cos, cos, cos, cos, -sin, zero, -sin, zero, zero, sin, zero, sin], axis=1)


def _prepare_weights(g_attn, w_in, g_q, g_kv, w_uq, w_uk, w_uv, w_a2, b_a, g_gla, w_o_mla, w_o_gla, w_out,
                     g_mem, w_mk, w_mv, g_cross, w_mq, w_mo, g_ffn, w_router, b_router, w_gate_up, b_gate_up,
                     w_down, b_down, g_final):
    splits = np.cumsum((0,) + IN_SPLITS)
    parts = [w_in[0][:, splits[i]:splits[i + 1]] for i in range(len(IN_SPLITS))]
    parts = [jnp.pad(p, ((0, 0), (0, wd - p.shape[1]))) for p, wd in zip(parts, _PACKED_WIDTHS)]
    wq = w_uq[0].reshape(Q_LORA, MLA_HEADS, QK_NOPE + QK_ROPE)
    wq_rope = jnp.pad(wq[:, :, QK_NOPE:], ((0, 0), (0, 0), (0, LANES - QK_ROPE)))
    row = lambda v: v.reshape(1, -1)
    return dict(
        g_attn=row(g_attn[0]), w_in=jnp.concatenate(parts, axis=1).astype(BF16), g_q=row(g_q[0]), g_kv=row(g_kv[0]),
        w_uq=jnp.concatenate([wq[:, :, :QK_NOPE].reshape(Q_LORA, -1), wq_rope.reshape(Q_LORA, -1)], axis=1).astype(BF16),
        wuk=jnp.transpose(w_uk[0], (1, 2, 0)).astype(BF16),
        wuv=jnp.transpose(w_uv[0], (1, 0, 2)).astype(BF16),
        w_a2=jnp.pad(w_a2[0], ((0, LANES - GLA_RANK), (0, 0))).astype(BF16), b_a=row(b_a[0]),
        g_gla=row(g_gla[0]), w_o_mla=w_o_mla[0].astype(BF16), w_o_gla=w_o_gla[0].astype(BF16),
        w_out=w_out[0].astype(BF16), g_mem=row(g_mem[0]),
        w_mkv=jnp.concatenate([w_mk[0], w_mv[0]], axis=1).astype(BF16),
        g_cross=row(g_cross[0]), w_mq=w_mq[0].astype(BF16), w_mo=w_mo[0].astype(BF16), g_ffn=row(g_ffn[0]),
        w_router_t=w_router[0].T, b_router=b_router[0].reshape(-1, 1),
        w_gate_up=w_gate_up[0], b_gate_up=b_gate_up[0].reshape(N_EXPERTS, 1, 2 * D_FF),
        w_down=w_down[0], b_down=b_down[0].reshape(N_EXPERTS, 1, D_MODEL), g_final=row(g_final),
    )


def kernel(x_prompt, x_sample, cache_kv_latent, cache_k_rope, state_gla, cache_mem_k, cache_mem_v, page_table, mem_prompt, g_attn, w_in, g_q, g_kv, w_uq, w_uk, w_uv, w_a2, b_a, g_gla, w_o_mla, w_o_gla, w_out, g_mem, w_mk, w_mv, g_cross, w_mq, w_mo, g_ffn, w_router, b_router, w_gate_up, b_gate_up, w_down, b_down, g_final):
    w = _prepare_weights(g_attn, w_in, g_q, g_kv, w_uq, w_uk, w_uv, w_a2, b_a, g_gla, w_o_mla, w_o_gla, w_out,
                         g_mem, w_mk, w_mv, g_cross, w_mq, w_mo, g_ffn, w_router, b_router, w_gate_up, b_gate_up,
                         w_down, b_down, g_final)
    B, S, D = x_prompt.shape
    Bd = x_sample.shape[0]
    past_len = page_table.shape[1] * cache_kv_latent.shape[2]

    tm = min(256, S)
    xp = x_prompt.reshape(B * S, D)
    (qcat, ckv, krope, kcat, kcat_t, gq, gk, gv, sg, la, szm, szg) = _premix(
        xp, _rope_table(jnp.arange(S, dtype=F32)), w, min(PREMIX_ROWS, S), tm, BF16)
    olat = _mla_prompt(qcat.reshape(MLA_HEADS, B, S, QCAT), kcat.reshape(B, S, QCAT), kcat_t, tm)
    og, state_p = _gla_prompt(gq.reshape(B, S, -1), gk.reshape(B, S, -1), gv.reshape(B, S, -1), la.reshape(B, S, -1))
    h1, qx = _postmix(xp, olat.reshape(B * S, -1), og.reshape(B * S, -1), sg, szm, szg, w, tm)
    mem_k, mem_v = _memkv(mem_prompt.reshape(B * N_MEM, D), w["g_mem"], w["w_mkv"], min(512, B * N_MEM))
    tb = min(MOE_BLOCK, S)
    h2, xn2, idx, rank, wt, cnt = _cross_prompt(qx, h1, mem_k.reshape(B, N_MEM, MEM_WIDTH),
                                                mem_v.reshape(B, N_MEM, MEM_WIDTH), w, tb)
    y_prompt = _moe(xn2, idx, rank, wt, cnt.reshape(-1, N_EXPERTS), h2, w, tb, tb)

    xs = x_sample.reshape(Bd, D)
    (qcat_s, ckv_s, krope_s, kcat_s, _, gq_s, gk_s, gv_s, sg_s, la_s, szm_s, szg_s) = _premix(
        xs, _rope_table(jnp.full((Bd,), past_len, F32)), w, Bd, Bd, F32)
    olat_s = _mla_sample(page_table, qcat_s, kcat_s, cache_kv_latent[0], jnp.swapaxes(cache_k_rope[0], 1, 2))
    og_s, state_s = _gla_sample(gq_s, gk_s, gv_s, la_s, state_gla[0])
    h1_s, qx_s = _postmix(xs, olat_s.reshape(Bd, -1), og_s, sg_s, szm_s, szg_s, w, Bd)
    h2_s, xn2_s, idx_s, rank_s, wt_s, cnt_s = _cross_sample(
        qx_s, h1_s, cache_mem_k, cache_mem_v, w)
    y_sample = _moe(xn2_s, idx_s, rank_s, wt_s, cnt_s.reshape(-1, N_EXPERTS), h2_s, w, Bd, Bd)

    return (y_prompt.reshape(B, S, D), y_sample.reshape(Bd, 1, D),
            ckv.reshape(1, B, S, KV_LORA), krope.reshape(1, B, S, QK_ROPE), state_p[None],
            mem_k.reshape(1, B, N_MEM, MEM_HEADS, MEM_HEAD_DIM), mem_v.reshape(1, B, N_MEM, MEM_HEADS, MEM_HEAD_DIM),
            ckv_s.reshape(1, Bd, 1, KV_LORA), krope_s.reshape(1, Bd, 1, QK_ROPE), state_s[None])
```

```python
import functools

import jax
import jax.numpy as jnp
import numpy as np
from jax import lax
from jax.experimental import pallas as pl
from jax.experimental.pallas import tpu as pltpu

F32 = jnp.float32
BF16 = jnp.bfloat16

D_MODEL = 1024
MLA_HEADS = 8
Q_LORA = 256
KV_LORA = 128
QK_NOPE = 128
QK_ROPE = 64
ROPE_THETA = 10000.0
ATTN_SCALE = (QK_NOPE + QK_ROPE) ** -0.5
GLA_HEADS = 4
GLA_DK = 128
GLA_DV = 256
GLA_RANK = 16
GLA_TAU = 16.0
GLA_CHUNK = 64
GLA_GROUP = 4
N_MEM = 256
MEM_HEADS = 4
MEM_HEAD_DIM = 128
N_EXPERTS = 32
TOP_K = 4
D_FF = 1024
SWIGLU_LIMIT = 7.0
SWIGLU_ALPHA = 1.702
EPS = 1e-6

GLA_K_WIDTH = GLA_HEADS * GLA_DK
GLA_V_WIDTH = GLA_HEADS * GLA_DV
MEM_WIDTH = MEM_HEADS * MEM_HEAD_DIM
IN_SPLITS = (Q_LORA, KV_LORA, QK_ROPE, GLA_K_WIDTH, GLA_K_WIDTH, GLA_V_WIDTH, GLA_V_WIDTH, GLA_RANK, D_MODEL, D_MODEL)

LANES = 128
_PACKED_WIDTHS = (Q_LORA, KV_LORA, LANES, GLA_K_WIDTH, GLA_K_WIDTH, GLA_V_WIDTH, GLA_V_WIDTH, LANES, D_MODEL, D_MODEL)
_OFF = tuple(int(v) for v in np.cumsum((0,) + _PACKED_WIDTHS))
QCAT = 2 * LANES

VMEM_LIMIT = 56 << 20
PREMIX_ROWS = 512


def _params(n_axes):
    return pltpu.CompilerParams(dimension_semantics=("arbitrary",) * n_axes, vmem_limit_bytes=VMEM_LIMIT)


def _rms(x, g):
    var = jnp.mean(x * x, axis=-1, keepdims=True)
    return x * lax.rsqrt(var + EPS) * g


def _dot(a, b):
    return jnp.dot(a.astype(BF16), b.astype(BF16), preferred_element_type=F32)


def _dot_nt(a, b):
    return lax.dot_general(a.astype(BF16), b.astype(BF16), (((1,), (1,)), ((), ())), preferred_element_type=F32)


def _dot_tn(a, b):
    return lax.dot_general(a.astype(BF16), b.astype(BF16), (((0,), (0,)), ((), ())), preferred_element_type=F32)


def _split3(x):
    hi = x.astype(BF16)
    r1 = x - hi.astype(F32)
    mid = r1.astype(BF16)
    lo = (r1 - mid.astype(F32)).astype(BF16)
    return hi, mid, lo


def _dot_f32_nt(a, b):
    a0, a1, a2 = _split3(a)
    b0, b1, b2 = _split3(b)
    d = lambda u, v: lax.dot_general(u, v, (((1,), (1,)), ((), ())), preferred_element_type=F32)
    return ((d(a1, b1) + d(a0, b2) + d(a2, b0)) + (d(a0, b1) + d(a1, b0))) + d(a0, b0)


def _full(shape):
    n = len(shape)
    return pl.BlockSpec(shape, lambda *_: (0,) * n)


def _premix_kernel(x_ref, tab_ref, g_attn_ref, w_in_ref, g_q_ref, g_kv_ref, w_uq_ref, w_a2_ref, b_a_ref,
                   qcat_ref, ckv_ref, krope_ref, kcat_ref, kcat_t_ref, gq_ref, gk_ref, gv_ref, sg_ref, la_ref, szm_ref, szg_ref):
    xn = _rms(x_ref[...], g_attn_ref[...]).astype(BF16)

    def proj(lo, hi):
        y = jnp.dot(xn, w_in_ref[:, _OFF[lo]:_OFF[hi]], preferred_element_type=F32)
        return [y[:, _OFF[i] - _OFF[lo]:_OFF[i + 1] - _OFF[lo]] for i in range(lo, hi)]

    tab = tab_ref[...]
    cos, sin_lo, sin_hi = tab[:, :LANES], tab[:, LANES:2 * LANES], tab[:, 2 * LANES:]

    def rope(t):
        return t * cos + pltpu.roll(t, LANES - QK_ROPE // 2, 1) * sin_lo + pltpu.roll(t, QK_ROPE // 2, 1) * sin_hi

    cq, ckv, kr = proj(0, 3)
    c_q = _rms(cq, g_q_ref[...]).astype(BF16)
    ckv = _rms(ckv, g_kv_ref[...])
    kr = rope(kr)
    ckv_ref[...] = ckv
    krope_ref[...] = kr[:, :QK_ROPE]
    kcat = jnp.concatenate([ckv, kr], axis=1)
    kcat_ref[...] = kcat.astype(BF16)
    tk = kcat_t_ref.shape[2]
    for t in range(kcat_t_ref.shape[0]):
        kcat_t_ref[t] = kcat[t * tk:(t + 1) * tk].T.astype(BF16)
    q_all = jnp.dot(c_q, w_uq_ref[...], preferred_element_type=F32)
    for h in range(MLA_HEADS):
        qcat_ref[h, :, :LANES] = q_all[:, h * LANES:(h + 1) * LANES].astype(qcat_ref.dtype)
        qcat_ref[h, :, LANES:] = rope(q_all[:, (MLA_HEADS + h) * LANES:(MLA_HEADS + h + 1) * LANES]).astype(qcat_ref.dtype)
    gq, gk = proj(3, 5)
    gq_ref[...] = (gq * GLA_DK ** -0.5).astype(gq_ref.dtype)
    gk_ref[...] = gk.astype(gk_ref.dtype)
    gv_ref[...] = proj(5, 6)[0].astype(gv_ref.dtype)
    gg, ga = proj(6, 8)
    sg_ref[...] = (gg * jax.nn.sigmoid(gg)).astype(sg_ref.dtype)
    z = _dot(ga, w_a2_ref[...]) + b_a_ref[...]
    la_ref[...] = -(jnp.maximum(-z, 0.0) + jnp.log1p(jnp.exp(-jnp.abs(z)))) * (1.0 / GLA_TAU)
    szm_ref[...] = jax.nn.sigmoid(proj(8, 9)[0]).astype(szm_ref.dtype)
    szg_ref[...] = jax.nn.sigmoid(proj(9, 10)[0]).astype(szg_ref.dtype)


def _fold_kernel(wq_ref, wk_ref, o_ref):
    o_ref[...] = _dot_f32_nt(wq_ref[...], wk_ref[...]).astype(o_ref.dtype)


def _fold_latent_query(w_uq_nope, w_uk):
    return pl.pallas_call(
        _fold_kernel,
        grid=(MLA_HEADS,),
        in_specs=[pl.BlockSpec((None, Q_LORA, QK_NOPE), lambda h: (h, 0, 0)),
                  pl.BlockSpec((None, KV_LORA, QK_NOPE), lambda h: (h, 0, 0))],
        out_specs=pl.BlockSpec((Q_LORA, KV_LORA), lambda h: (0, h)),
        out_shape=jax.ShapeDtypeStruct((Q_LORA, MLA_HEADS * KV_LORA), BF16),
        compiler_params=_params(1),
        name="fold_latent_query",
    )(w_uq_nope, w_uk)


def _premix(x, tab, w, tm, tk, gdtype):
    T = x.shape[0]
    n_tab = tab.shape[0] // tm
    row = lambda n: pl.BlockSpec((tm, n), lambda i: (i, 0))
    outs = [(KV_LORA, F32), (QK_ROPE, F32), (QCAT, BF16), None, (GLA_K_WIDTH, gdtype),
            (GLA_K_WIDTH, gdtype), (GLA_V_WIDTH, gdtype), (GLA_V_WIDTH, BF16), (GLA_K_WIDTH, F32),
            (D_MODEL, BF16), (D_MODEL, BF16)]
    out_specs = [row(o[0]) if o else pl.BlockSpec((tm // tk, QCAT, tk), lambda i: (i, 0, 0)) for o in outs]
    out_shape = [jax.ShapeDtypeStruct((T, o[0]) if o else (T // tk, QCAT, tk), o[1] if o else BF16) for o in outs]
    consts = [w["g_attn"], w["w_in"], w["g_q"], w["g_kv"], w["w_uq"], w["w_a2"], w["b_a"]]
    const_specs = [pl.BlockSpec(c.shape, lambda i, n=len(c.shape): (0,) * n,
                                pipeline_mode=pl.Buffered(1) if c is w["w_in"] else None) for c in consts]
    return pl.pallas_call(
        _premix_kernel,
        grid=(T // tm,),
        in_specs=[row(D_MODEL), pl.BlockSpec((tm, 3 * LANES), lambda i: (i % n_tab, 0))] + const_specs,
        out_specs=[pl.BlockSpec((MLA_HEADS, tm, QCAT), lambda i: (0, i, 0))] + out_specs,
        out_shape=[jax.ShapeDtypeStruct((MLA_HEADS, T, QCAT), BF16)] + out_shape,
        compiler_params=_params(1),
        name="premix",
    )(x, tab, *consts)


def _mla_prompt_kernel(q_ref, k_ref, kt_ref, o_ref, m_ref, l_ref, acc_ref):
    tq = q_ref.shape[1]
    qi = pl.program_id(1)
    q = q_ref[...].reshape(MLA_HEADS * tq, QCAT)
    row = lax.broadcasted_iota(jnp.int32, (MLA_HEADS, tq, tq), 1).reshape(MLA_HEADS * tq, tq)
    col = lax.broadcasted_iota(jnp.int32, (MLA_HEADS * tq, tq), 1)
    m_ref[...] = jnp.full_like(m_ref, -jnp.inf)
    l_ref[...] = jnp.zeros_like(l_ref)
    acc_ref[...] = jnp.zeros_like(acc_ref)
    c = ATTN_SCALE * np.log2(np.e)

    def step(j, n, mask_last):
        keys = pl.ds(pl.multiple_of(j * tq, tq), n * tq)
        parts = [jnp.dot(q, kt_ref[j + t], preferred_element_type=F32) for t in range(n)]
        if mask_last:
            parts[-1] = jnp.where(col <= row, parts[-1], -jnp.inf)
        s = jnp.concatenate(parts, axis=1)
        m = m_ref[...]
        m_new = jnp.maximum(m, jnp.max(s, axis=-1, keepdims=True))
        alpha = jnp.exp2((m - m_new) * c)
        p = jnp.exp2((s - jnp.concatenate([m_new] * (n * tq // LANES), axis=1)) * c)
        p_lanes = p[:, :LANES]
        for t in range(1, n * tq // LANES):
            p_lanes = p_lanes + p[:, t * LANES:(t + 1) * LANES]
        l_ref[...] = alpha * l_ref[...] + p_lanes
        acc_ref[...] = alpha * acc_ref[...] + _dot(p, k_ref[keys, :KV_LORA])
        m_ref[...] = m_new

    def body(jj, carry):
        step(2 * jj, 2, False)
        return carry

    lax.fori_loop(0, qi // 2, body, 0)

    @pl.when(qi % 2 == 0)
    def _():
        step(qi, 1, True)

    @pl.when(qi % 2 == 1)
    def _():
        step(qi - 1, 2, True)
    o = acc_ref[...] / jnp.sum(l_ref[...], axis=-1, keepdims=True)
    for h in range(MLA_HEADS):
        o_ref[:, h * KV_LORA:(h + 1) * KV_LORA] = o[h * tq:(h + 1) * tq].astype(o_ref.dtype)


def _mla_prompt(qcat, kcat, kcat_t, tq):
    _, B, S, _ = qcat.shape
    rows = MLA_HEADS * tq
    return pl.pallas_call(
        _mla_prompt_kernel,
        grid=(B, S // tq),
        in_specs=[pl.BlockSpec((MLA_HEADS, None, tq, QCAT), lambda b, i: (0, b, i, 0)),
                  pl.BlockSpec((None, S, QCAT), lambda b, i: (b, 0, 0)),
                  pl.BlockSpec((S // tq, QCAT, tq), lambda b, i: (b, 0, 0))],
        out_specs=pl.BlockSpec((None, tq, MLA_HEADS * KV_LORA), lambda b, i: (b, i, 0)),
        out_shape=jax.ShapeDtypeStruct((B, S, MLA_HEADS * KV_LORA), BF16),
        scratch_shapes=[pltpu.VMEM((rows, LANES), F32), pltpu.VMEM((rows, LANES), F32), pltpu.VMEM((rows, KV_LORA), F32)],
        compiler_params=_params(2),
        name="mla_prompt",
    )(qcat, kcat, kcat_t)


def _mla_sample_kernel(pt_ref, q_ref, knew_ref, lat_hbm, rope_hbm, o_ref, lat_buf, rope_buf, lat_bf, rope_bf, sems):
    b = pl.program_id(0)
    n_pages, page = lat_buf.shape[1], lat_buf.shape[2]

    def page_copies(sample, slot, i):
        pg = pt_ref[sample, i]
        return (pltpu.make_async_copy(lat_hbm.at[pg], lat_buf.at[slot, i], sems.at[0, slot]),
                pltpu.make_async_copy(rope_hbm.at[pg], rope_buf.at[slot, i], sems.at[1, slot]))

    def for_pages(sample, slot, act):
        def body(i, carry):
            for cp in page_copies(sample, slot, i):
                act(cp)
            return carry
        lax.fori_loop(0, n_pages, body, 0)

    slot = b % 2

    @pl.when(b == 0)
    def _():
        for_pages(0, 0, lambda cp: cp.start())

    @pl.when(b + 1 < pl.num_programs(0))
    def _():
        for_pages(b + 1, 1 - slot, lambda cp: cp.start())

    for_pages(b, slot, lambda cp: cp.wait())

    lat_bf[...] = lat_buf[slot].reshape(n_pages * page, KV_LORA).astype(BF16)
    for i in range(n_pages):
        rope_bf[:, i * page:(i + 1) * page] = rope_buf[slot, i].astype(BF16)
    q = q_ref[...].reshape(MLA_HEADS, QCAT)
    knew = knew_ref[...].astype(F32)
    s = (_dot_nt(q[:, :KV_LORA], lat_bf[...]) + _dot(q[:, KV_LORA:KV_LORA + QK_ROPE], rope_bf[...])) * ATTN_SCALE
    s_new = jnp.sum(q.astype(F32) * knew, axis=-1, keepdims=True) * ATTN_SCALE
    m = jnp.maximum(jnp.max(s, axis=-1, keepdims=True), s_new)
    p = jnp.exp(s - m)
    p_new = jnp.exp(s_new - m)
    denom = jnp.sum(p, axis=-1, keepdims=True) + p_new
    o = _dot(p, lat_bf[...]) + p_new * knew[:, :KV_LORA]
    o_ref[...] = (o / denom).astype(o_ref.dtype)


def _mla_sample(page_table, qcat, knew, cache_lat, cache_rope_t):
    Bd, n_pages = page_table.shape
    page = cache_lat.shape[1]
    grid_spec = pltpu.PrefetchScalarGridSpec(
        num_scalar_prefetch=1,
        grid=(Bd,),
        in_specs=[pl.BlockSpec((MLA_HEADS, None, 1, QCAT), lambda b, pt: (0, b, 0, 0)),
                  pl.BlockSpec((None, 1, QCAT), lambda b, pt: (b, 0, 0)),
                  pl.BlockSpec(memory_space=pl.ANY), pl.BlockSpec(memory_space=pl.ANY)],
        out_specs=pl.BlockSpec((None, MLA_HEADS, KV_LORA), lambda b, pt: (b, 0, 0)),
        scratch_shapes=[pltpu.VMEM((2, n_pages, page, KV_LORA), F32), pltpu.VMEM((2, n_pages, QK_ROPE, page), F32),
                        pltpu.VMEM((n_pages * page, KV_LORA), BF16), pltpu.VMEM((QK_ROPE, n_pages * page), BF16),
                        pltpu.SemaphoreType.DMA((2, 2))],
    )
    return pl.pallas_call(
        _mla_sample_kernel,
        grid_spec=grid_spec,
        out_shape=jax.ShapeDtypeStruct((Bd, MLA_HEADS, KV_LORA), BF16),
        compiler_params=_params(1),
        name="mla_sample",
    )(page_table, qcat.reshape(MLA_HEADS, Bd, 1, QCAT), knew.reshape(Bd, 1, QCAT), cache_lat, cache_rope_t)


def _gla_prompt_kernel(q_ref, k_ref, v_ref, la_ref, o_ref, state_ref, st_ref):
    C = GLA_CHUNK
    S = q_ref.shape[0]
    row = lax.broadcasted_iota(jnp.int32, (C, 2 * C), 0)
    col = lax.broadcasted_iota(jnp.int32, (C, 2 * C), 1)
    causal = row >= col
    tri = causal[:, :C].astype(BF16)
    pad_k = jnp.zeros((C, GLA_DK), F32)
    pad_v = jnp.zeros((C, GLA_DV), F32)
    st_ref[...] = jnp.zeros_like(st_ref)

    group = min(GLA_GROUP, S // C)

    def chunks(c, carry):
        parts = []
        for g in range(group):
            rows = pl.ds(pl.multiple_of((c * group + g) * C, C), C)
            split = jnp.dot(tri, jnp.concatenate(_split3(la_ref[rows, :]), axis=1), preferred_element_type=F32)
            b_all = (split[:, 2 * GLA_K_WIDTH:] + split[:, GLA_K_WIDTH:2 * GLA_K_WIDTH]) + split[:, :GLA_K_WIDTH]
            for h in range(GLA_HEADS):
                kcols = slice(h * GLA_DK, (h + 1) * GLA_DK)
                vcols = slice(h * GLA_DV, (h + 1) * GLA_DV)
                b = b_all[:, kcols]
                b_last = b[C - 1:C, :]
                q = q_ref[rows, kcols].astype(F32)
                k = k_ref[rows, kcols].astype(F32)
                q_in = (q * jnp.exp(b)).astype(BF16)
                k_in = jnp.concatenate([k * jnp.exp(-b), pad_k], axis=0)
                k_out = jnp.concatenate([k * jnp.exp(b_last - b), pad_k], axis=0)
                a = jnp.where(causal, _dot_nt(q_in, k_in), 0.0)
                v_t = jnp.concatenate([v_ref[rows, vcols].astype(F32), pad_v], axis=0).T.astype(BF16)
                lhs = jnp.concatenate([q_in, a.astype(BF16)], axis=1)
                parts.append((rows, h, vcols, lhs, v_t, jnp.exp(b_last), _dot(v_t, k_out)))
        for rows, h, vcols, lhs, v_t, decay, kv_t in parts:
            st = st_ref[h]
            o_ref[rows, vcols] = _dot_nt(lhs, jnp.concatenate([st.astype(BF16), v_t], axis=1)).astype(o_ref.dtype)
            st_ref[h] = st * decay + kv_t
        return carry

    lax.fori_loop(0, S // C // group, chunks, 0)
    for h in range(GLA_HEADS):
        state_ref[h] = st_ref[h].T


def _gla_prompt(q, k, v, la):
    B, S, _ = q.shape
    seq = lambda n: pl.BlockSpec((None, S, n), lambda b: (b, 0, 0))
    return pl.pallas_call(
        _gla_prompt_kernel,
        grid=(B,),
        in_specs=[seq(GLA_K_WIDTH), seq(GLA_K_WIDTH), seq(GLA_V_WIDTH), seq(GLA_K_WIDTH)],
        out_specs=[seq(GLA_V_WIDTH), pl.BlockSpec((None, GLA_HEADS, GLA_DK, GLA_DV), lambda b: (b, 0, 0, 0))],
        out_shape=[jax.ShapeDtypeStruct((B, S, GLA_V_WIDTH), BF16),
                   jax.ShapeDtypeStruct((B, GLA_HEADS, GLA_DK, GLA_DV), F32)],
        scratch_shapes=[pltpu.VMEM((GLA_HEADS, GLA_DV, GLA_DK), F32)],
        compiler_params=_params(1),
        name="gla_prompt",
    )(q, k, v, la)


def _gla_sample_kernel(q_ref, k_ref, v_ref, la_ref, st_ref, o_ref, sto_ref):
    nb = q_ref.shape[0]
    pad = jnp.zeros((GLA_DK - nb, GLA_DK), F32)
    for h in range(GLA_HEADS):
        kcols = slice(h * GLA_DK, (h + 1) * GLA_DK)
        vcols = slice(h * GLA_DV, (h + 1) * GLA_DV)
        col = lambda ref: jnp.concatenate([ref[:, kcols].astype(F32), pad], axis=0).T
        q_t, k_t, decay_t = col(q_ref), col(k_ref), jnp.exp(col(la_ref))
        for i in range(nb):
            new = decay_t[:, i:i + 1] * st_ref[i, h] + k_t[:, i:i + 1] * v_ref[i:i + 1, vcols].astype(F32)
            sto_ref[i, h] = new
            o_ref[i:i + 1, vcols] = jnp.sum(q_t[:, i:i + 1] * new, axis=0, keepdims=True).astype(o_ref.dtype)


def _gla_sample(q, k, v, la, state, nb=8):
    Bd = q.shape[0]
    rows = lambda n: pl.BlockSpec((nb, n), lambda i: (i, 0))
    st_spec = pl.BlockSpec((nb, GLA_HEADS, GLA_DK, GLA_DV), lambda i: (i, 0, 0, 0))
    return pl.pallas_call(
        _gla_sample_kernel,
        grid=(Bd // nb,),
        in_specs=[rows(GLA_K_WIDTH), rows(GLA_K_WIDTH), rows(GLA_V_WIDTH), rows(GLA_K_WIDTH), st_spec],
        out_specs=[rows(GLA_V_WIDTH), st_spec],
        out_shape=[jax.ShapeDtypeStruct((Bd, GLA_V_WIDTH), BF16), jax.ShapeDtypeStruct(state.shape, F32)],
        compiler_params=_params(1),
        name="gla_sample",
    )(q, k, v, la, state)


def _postmix_kernel(x_ref, olat_ref, og_ref, sg_ref, szm_ref, szg_ref, wuv_ref, w_o_mla_ref, g_gla_ref, w_o_gla_ref,
                    w_out_ref, g_cross_ref, w_mq_ref, h_ref, qx_ref):
    ov = jnp.concatenate(
        [_dot(olat_ref[:, h * KV_LORA:(h + 1) * KV_LORA], wuv_ref[h]).astype(BF16) for h in range(MLA_HEADS)], axis=1)
    o_mla = jnp.dot(ov, w_o_mla_ref[...], preferred_element_type=F32)
    g_gla = g_gla_ref[...]
    og = jnp.concatenate(
        [(_rms(og_ref[:, h * GLA_DV:(h + 1) * GLA_DV].astype(F32), g_gla)
          * sg_ref[:, h * GLA_DV:(h + 1) * GLA_DV].astype(F32)).astype(BF16) for h in range(GLA_HEADS)], axis=1)
    o_gla = jnp.dot(og, w_o_gla_ref[...], preferred_element_type=F32)
    merged = szm_ref[...].astype(F32) * o_mla + szg_ref[...].astype(F32) * o_gla
    h1 = x_ref[...] + _dot(merged, w_out_ref[...])
    h_ref[...] = h1
    qx_ref[...] = _dot(_rms(h1, g_cross_ref[...]), w_mq_ref[...]).astype(qx_ref.dtype)


def _postmix(x, olat, og, sg, szm, szg, w, tm):
    T = x.shape[0]
    row = lambda n: pl.BlockSpec((tm, n), lambda i: (i, 0))
    consts = [w["wuv"], w["w_o_mla"], w["g_gla"], w["w_o_gla"], w["w_out"], w["g_cross"], w["w_mq"]]
    return pl.pallas_call(
        _postmix_kernel,
        grid=(T // tm,),
        in_specs=[row(D_MODEL)] * 6 + [_full(c.shape) for c in consts],
        out_specs=[row(D_MODEL), row(MEM_WIDTH)],
        out_shape=[jax.ShapeDtypeStruct((T, D_MODEL), F32), jax.ShapeDtypeStruct((T, MEM_WIDTH), BF16)],
        compiler_params=_params(1),
        name="postmix",
    )(x, olat, og, sg, szm, szg, *consts)


def _memkv_kernel(mem_ref, g_ref, w_ref, k_ref, v_ref):
    kv = _dot(_rms(mem_ref[...], g_ref[...]), w_ref[...])
    k_ref[...] = kv[:, :MEM_WIDTH]
    v_ref[...] = kv[:, MEM_WIDTH:]


def _memkv(mem, g_mem, w_mkv, tm=512):
    T = mem.shape[0]
    row = lambda n: pl.BlockSpec((tm, n), lambda i: (i, 0))
    return pl.pallas_call(
        _memkv_kernel,
        grid=(T // tm,),
        in_specs=[row(D_MODEL), _full(g_mem.shape), _full(w_mkv.shape)],
        out_specs=[row(MEM_WIDTH), row(MEM_WIDTH)],
        out_shape=[jax.ShapeDtypeStruct((T, MEM_WIDTH), F32)] * 2,
        compiler_params=_params(1),
        name="memkv",
    )(mem, g_mem, w_mkv)


def _attend_memory(q, head_k, head_v):
    outs = []
    for h in range(MEM_HEADS):
        cols = slice(h * MEM_HEAD_DIM, (h + 1) * MEM_HEAD_DIM)
        s = _dot_nt(q[:, cols], head_k(h)) * MEM_HEAD_DIM ** -0.5
        p = jnp.exp(s - jnp.max(s, axis=-1, keepdims=True))
        p = p / jnp.sum(p, axis=-1, keepdims=True)
        outs.append(_dot(p, head_v(h)).astype(BF16))
    return jnp.concatenate(outs, axis=1)


def _route(o, h1, w_mo_ref, g_ffn_ref, w_router_ref, b_router_ref, h_ref, xn_ref, idx_ref, rank_ref, wt_ref, cnt_ref):
    h2 = h1 + jnp.dot(o, w_mo_ref[...], preferred_element_type=F32)
    h_ref[...] = h2
    xn = _rms(h2, g_ffn_ref[...])
    xn_ref[...] = xn.astype(xn_ref.dtype)
    tb = xn.shape[0]
    logits = _dot_f32_nt(w_router_ref[...], xn) + b_router_ref[...]
    expert = lax.broadcasted_iota(jnp.int32, logits.shape, 0)
    work = logits
    hits, firsts, exps = [], [], []
    top = None
    for _ in range(TOP_K):
        best = jnp.max(work, axis=0, keepdims=True)
        first = jnp.min(jnp.where(work == best, expert, N_EXPERTS), axis=0, keepdims=True)
        hit = expert == first
        top = best if top is None else top
        hits.append(hit)
        firsts.append(first)
        exps.append(jnp.exp(best - top))
        work = jnp.where(hit, -jnp.inf, work)
    denom = (exps[0] + exps[1]) + (exps[2] + exps[3])
    chosen = jnp.zeros(logits.shape, F32)
    for hit in hits:
        chosen = chosen + jnp.where(hit, 1.0, 0.0)
    before = (lax.broadcasted_iota(jnp.int32, (tb, tb), 0) < lax.broadcasted_iota(jnp.int32, (tb, tb), 1)).astype(BF16)
    rank = jnp.dot(chosen.astype(BF16), before, preferred_element_type=F32)
    idx_ref[...] = jnp.concatenate(firsts, axis=0)
    rank_ref[...] = jnp.concatenate(
        [jnp.sum(jnp.where(hit, rank, 0.0), axis=0, keepdims=True) for hit in hits], axis=0).astype(jnp.int32)
    wt_ref[...] = jnp.concatenate([e / denom for e in exps], axis=0)
    cnt_ref[...] = jnp.sum(chosen, axis=1, keepdims=True).astype(jnp.int32)


def _cross_prompt_kernel(qx_ref, h1_ref, mk_ref, mv_ref, *rest):
    head = lambda ref: lambda h: ref[:, h * MEM_HEAD_DIM:(h + 1) * MEM_HEAD_DIM]
    o = _attend_memory(qx_ref[...], head(mk_ref), head(mv_ref))
    _route(o, h1_ref[...], *rest)


def _cross_sample_kernel(qx_ref, mk_ref, mv_ref, o_ref):
    for i in range(qx_ref.shape[0]):
        q = jnp.broadcast_to(qx_ref[i:i + 1, :], (8, MEM_WIDTH))
        head = lambda ref: lambda h: ref[i, :, h, :]
        o_ref[i:i + 1, :] = _attend_memory(q, head(mk_ref), head(mv_ref))[:1]


def _route_kernel(o_ref, h1_ref, *rest):
    _route(o_ref[...], h1_ref[...], *rest)


def _route_specs(T, tb, w):
    consts = [w["w_mo"], w["g_ffn"], w["w_router_t"], w["b_router"]]
    row = lambda n: pl.BlockSpec((tb, n), lambda i: (i, 0))
    per_choice = pl.BlockSpec((TOP_K, tb), lambda i: (0, i))
    out_specs = [row(D_MODEL), row(D_MODEL), per_choice, per_choice, per_choice,
                 pl.BlockSpec((None, N_EXPERTS, 1), lambda i: (i, 0, 0))]
    out_shape = [jax.ShapeDtypeStruct((T, D_MODEL), F32), jax.ShapeDtypeStruct((T, D_MODEL), BF16),
                 jax.ShapeDtypeStruct((TOP_K, T), jnp.int32), jax.ShapeDtypeStruct((TOP_K, T), jnp.int32),
                 jax.ShapeDtypeStruct((TOP_K, T), F32), jax.ShapeDtypeStruct((T // tb, N_EXPERTS, 1), jnp.int32)]
    return consts, out_specs, out_shape


def _cross_prompt(qx, h1, mem_k, mem_v, w, tb):
    T = qx.shape[0]
    blocks_per_mem = T // mem_k.shape[0] // tb
    consts, out_specs, out_shape = _route_specs(T, tb, w)
    row = lambda n: pl.BlockSpec((tb, n), lambda i: (i, 0))
    mem_spec = pl.BlockSpec((None, N_MEM, MEM_WIDTH), lambda i: (i // blocks_per_mem, 0, 0))
    return pl.pallas_call(
        _cross_prompt_kernel,
        grid=(T // tb,),
        in_specs=[row(MEM_WIDTH), row(D_MODEL), mem_spec, mem_spec] + [_full(c.shape) for c in consts],
        out_specs=out_specs,
        out_shape=out_shape,
        compiler_params=_params(1),
        name="cross_prompt",
    )(qx, h1, mem_k, mem_v, *consts)


def _cross_sample(qx, h1, mem_k, mem_v, w, nb=8):
    T = qx.shape[0]
    mem_spec = pl.BlockSpec((None, nb, N_MEM, MEM_HEADS, MEM_HEAD_DIM), lambda i: (0, i, 0, 0, 0))
    rows = pl.BlockSpec((nb, MEM_WIDTH), lambda i: (i, 0))
    o = pl.pallas_call(
        _cross_sample_kernel,
        grid=(T // nb,),
        in_specs=[rows, mem_spec, mem_spec],
        out_specs=rows,
        out_shape=jax.ShapeDtypeStruct((T, MEM_WIDTH), BF16),
        compiler_params=_params(1),
        name="cross_sample",
    )(qx, mem_k, mem_v)
    consts, out_specs, out_shape = _route_specs(T, T, w)
    return pl.pallas_call(
        _route_kernel,
        grid=(1,),
        in_specs=[_full(o.shape), _full(h1.shape)] + [_full(c.shape) for c in consts],
        out_specs=out_specs,
        out_shape=out_shape,
        compiler_params=_params(1),
        name="route_sample",
    )(o, h1, *consts)


SEG_ALIGN = 16
SEL_CHUNK = 256
MOE_BLOCK = 512


def _block_rows(tb):
    worst = TOP_K * tb + N_EXPERTS * (SEG_ALIGN - 1)
    return -(-worst // SEL_CHUNK) * SEL_CHUNK


def _seg_sizes(limit):
    sizes, b = [], SEG_ALIGN
    while b <= limit:
        sizes.append(b)
        b *= 2
    return sizes[::-1]


def _for_segments(n, sizes, make_copy, act):
    def emit(group):
        for size in group:
            @pl.when((n & size) != 0)
            def _():
                act(make_copy(pl.multiple_of(n & (-2 * size), SEG_ALIGN), size))

    large = [size for size in sizes if size >= 8 * SEG_ALIGN]
    if len(large) > 1:
        pl.when(n >= large[-1])(lambda: emit(large))
    else:
        emit(large)
    emit([size for size in sizes if size < 8 * SEG_ALIGN])


def _slot_rows(idx, rank, loc_ref, j):
    pos = rank
    for e in range(N_EXPERTS):
        pos = pos + jnp.where(idx == e, loc_ref[j, e], 0)
    return pos


def _dispatch_kernel(loc_ref, pad_ref, goff_ref, total_ref, gap_off_ref, gap_len_ref, x_ref, idx_ref, rank_ref, xs_hbm,
                     buf, sem, *, tm):
    j = pl.program_id(0)
    last = pl.num_programs(0) - 1
    slot = j % 2
    tb = x_ref.shape[0]
    r_blk = _block_rows(tb)
    pos = _slot_rows(idx_ref[...], rank_ref[...], loc_ref, j)
    x = x_ref[...]
    for c in range(r_blk // SEL_CHUNK):
        r = lax.broadcasted_iota(jnp.int32, (SEL_CHUNK, tb), 0) + c * SEL_CHUNK
        sel = jnp.where(r == pos[0:1], 1.0, jnp.where(r == pos[1:2], 1.0, jnp.where(
            r == pos[2:3], 1.0, jnp.where(r == pos[3:4], 1.0, 0.0))))
        buf[slot, c * SEL_CHUNK:(c + 1) * SEL_CHUNK, :] = jnp.dot(
            sel.astype(BF16), x, preferred_element_type=F32).astype(BF16)

    sizes = _seg_sizes(tb)

    def start_segments(e, carry):
        src, dst = loc_ref[j, e], goff_ref[j, e]
        _for_segments(pad_ref[j, e], sizes, lambda off, size: pltpu.make_async_copy(
            buf.at[slot, pl.ds(pl.multiple_of(src + off, SEG_ALIGN), size)],
            xs_hbm.at[pl.ds(pl.multiple_of(dst + off, SEG_ALIGN), size)], sem.at[slot]), lambda cp: cp.start())
        return carry

    lax.fori_loop(0, N_EXPERTS, start_segments, 0)

    def wait_block(blk, slot):
        _for_segments(total_ref[blk], _seg_sizes(r_blk), lambda off, size: pltpu.make_async_copy(
            buf.at[slot, pl.ds(0, size)], xs_hbm.at[pl.ds(0, size)], sem.at[slot]), lambda cp: cp.wait())

    @pl.when(j > 0)
    def _():
        wait_block(j - 1, 1 - slot)

    @pl.when(j == last)
    def _():
        wait_block(j, slot)
        buf[1 - slot, 0:tm, :] = jnp.zeros((tm, D_MODEL), BF16)
        gap_sizes = _seg_sizes(tm - SEG_ALIGN)

        def for_gaps(act):
            def body(e, carry):
                dst = gap_off_ref[e]
                _for_segments(gap_len_ref[e], gap_sizes, lambda off, size: pltpu.make_async_copy(
                    buf.at[1 - slot, pl.ds(0, size)], xs_hbm.at[pl.ds(pl.multiple_of(dst + off, SEG_ALIGN), size)],
                    sem.at[1 - slot]), act)
                return carry
            lax.fori_loop(0, N_EXPERTS, body, 0)

        for_gaps(lambda cp: cp.start())
        for_gaps(lambda cp: cp.wait())


def _ffn_kernel(tile_expert_ref, n_valid_ref, x_ref, wgu_f32_ref, bgu_ref, wd_f32_ref, bd_ref, o_ref, wgu_ref, wd_ref):
    t = pl.program_id(0)

    @pl.when(t < n_valid_ref[0])
    def _():
        @pl.when((t == 0) | (tile_expert_ref[t] != tile_expert_ref[jnp.maximum(t - 1, 0)]))
        def _():
            wgu_ref[...] = wgu_f32_ref[...].astype(BF16)
            wd_ref[...] = wd_f32_ref[...].astype(BF16)

        x = x_ref[...]
        bgu = bgu_ref[...]
        gate = jnp.dot(x, wgu_ref[:, :D_FF], preferred_element_type=F32) + bgu[:, :D_FF]
        up = jnp.dot(x, wgu_ref[:, D_FF:], preferred_element_type=F32) + bgu[:, D_FF:]
        gate = jnp.minimum(gate, SWIGLU_LIMIT)
        up = jnp.clip(up, -SWIGLU_LIMIT, SWIGLU_LIMIT)
        hidden = (up + 1.0) * gate * jax.nn.sigmoid(SWIGLU_ALPHA * gate)
        o_ref[...] = (_dot(hidden, wd_ref[...]) + bd_ref[...]).astype(o_ref.dtype)


def _combine_kernel(loc_ref, pad_ref, goff_ref, total_ref, idx0_ref, rank0_ref, wt0_ref, idx_ref, rank_ref, wt_ref,
                    h_ref, g_final_ref, o_hbm, y_ref, buf, sel_ref, sem):
    j = pl.program_id(0)
    last = pl.num_programs(0) - 1
    slot = j % 2
    tb = h_ref.shape[0]
    r_blk = buf.shape[1]
    sizes = _seg_sizes(tb)

    def fetch(blk, slot):
        def body(e, carry):
            dst, src = loc_ref[blk, e], goff_ref[blk, e]
            _for_segments(pad_ref[blk, e], sizes, lambda off, size: pltpu.make_async_copy(
                o_hbm.at[pl.ds(pl.multiple_of(src + off, SEG_ALIGN), size)],
                buf.at[slot, pl.ds(pl.multiple_of(dst + off, SEG_ALIGN), size)], sem.at[slot]), lambda cp: cp.start())
            return carry
        lax.fori_loop(0, N_EXPERTS, body, 0)

    def build(blk, slot, idx, rank, wt):
        pos = _slot_rows(idx, rank, loc_ref, blk)
        for c in range(r_blk // SEL_CHUNK):
            r = lax.broadcasted_iota(jnp.int32, (SEL_CHUNK, tb), 0) + c * SEL_CHUNK
            sel = jnp.zeros((SEL_CHUNK, tb), F32)
            for k in range(TOP_K):
                sel = jnp.where(r == pos[k:k + 1], wt[k:k + 1], sel)
            sel_ref[slot, c * SEL_CHUNK:(c + 1) * SEL_CHUNK, :] = sel.astype(BF16)

    @pl.when(j == 0)
    def _():
        buf[...] = jnp.zeros_like(buf)
        fetch(0, 0)
        build(0, 0, idx0_ref[...], rank0_ref[...], wt0_ref[...])

    @pl.when(j < last)
    def _():
        fetch(j + 1, 1 - slot)

    _for_segments(total_ref[j], _seg_sizes(r_blk), lambda off, size: pltpu.make_async_copy(
        o_hbm.at[pl.ds(0, size)], buf.at[slot, pl.ds(0, size)], sem.at[slot]), lambda cp: cp.wait())
    moe = _dot_tn(sel_ref[slot], buf[slot])
    y_ref[...] = _rms(h_ref[...] + moe, g_final_ref[...])
    build(jnp.minimum(j + 1, last), 1 - slot, idx_ref[...], rank_ref[...], wt_ref[...])


def _moe(xn, idx, rank, wt, cnt, h2, w, tb, tm):
    T = xn.shape[0]
    nb = T // tb
    r_blk = _block_rows(tb)
    i32 = jnp.int32
    pad = (cnt + (SEG_ALIGN - 1)) // SEG_ALIGN * SEG_ALIGN
    loc = jnp.cumsum(pad, axis=1) - pad
    seg = jnp.sum(pad, axis=0)
    region = (seg + (tm - 1)) // tm * tm
    region_end = jnp.cumsum(region)
    region_start = region_end - region
    goff = region_start[None, :] + jnp.cumsum(pad, axis=0) - pad
    n_tiles = -(-(nb * (TOP_K * tb + N_EXPERTS * (SEG_ALIGN - 1)) + N_EXPERTS * (tm - 1)) // tm)
    n_valid = (region_end[-1] // tm).astype(i32).reshape(1)
    tile = jnp.minimum(jnp.arange(n_tiles, dtype=i32), n_valid - 1)
    tile_expert = jnp.minimum(jnp.sum((region_end[None, :] <= (tile * tm)[:, None]).astype(i32), axis=1), N_EXPERTS - 1)
    plan = [a.astype(i32) for a in (loc, pad, goff, jnp.sum(pad, axis=1))]
    gaps = [(region_start + seg).astype(i32), (region - seg).astype(i32)]

    choice = pl.BlockSpec((TOP_K, tb), lambda j, *_: (0, j))
    xs = pl.pallas_call(
        functools.partial(_dispatch_kernel, tm=tm),
        grid_spec=pltpu.PrefetchScalarGridSpec(
            num_scalar_prefetch=6, grid=(nb,),
            in_specs=[pl.BlockSpec((tb, D_MODEL), lambda j, *_: (j, 0)), choice, choice],
            out_specs=pl.BlockSpec(memory_space=pl.ANY),
            scratch_shapes=[pltpu.VMEM((2, r_blk, D_MODEL), BF16), pltpu.SemaphoreType.DMA((2,))]),
        out_shape=jax.ShapeDtypeStruct((n_tiles * tm, D_MODEL), BF16),
        compiler_params=_params(1),
        name="moe_dispatch",
    )(*plan, *gaps, xn, idx, rank)

    rows = pl.BlockSpec((tm, D_MODEL), lambda t, te, nv: (jnp.minimum(t, nv[0] - 1), 0))
    per_e = lambda a, b: pl.BlockSpec((None, a, b), lambda t, te, nv: (te[t], 0, 0))
    out = pl.pallas_call(
        _ffn_kernel,
        grid_spec=pltpu.PrefetchScalarGridSpec(
            num_scalar_prefetch=2, grid=(n_tiles,),
            in_specs=[rows, per_e(D_MODEL, 2 * D_FF), per_e(1, 2 * D_FF), per_e(D_FF, D_MODEL), per_e(1, D_MODEL)],
            out_specs=rows,
            scratch_shapes=[pltpu.VMEM((D_MODEL, 2 * D_FF), BF16), pltpu.VMEM((D_FF, D_MODEL), BF16)]),
        out_shape=jax.ShapeDtypeStruct((n_tiles * tm, D_MODEL), BF16),
        compiler_params=_params(1),
        name="moe_ffn",
    )(tile_expert, n_valid, xs, w["w_gate_up"], w["b_gate_up"], w["w_down"], w["b_down"])

    first = pl.BlockSpec((TOP_K, tb), lambda j, *_: (0, 0))
    ahead = pl.BlockSpec((TOP_K, tb), lambda j, *_: (0, jnp.minimum(j + 1, nb - 1)))
    return pl.pallas_call(
        _combine_kernel,
        grid_spec=pltpu.PrefetchScalarGridSpec(
            num_scalar_prefetch=4, grid=(nb,),
            in_specs=[first, first, first, ahead, ahead, ahead, pl.BlockSpec((tb, D_MODEL), lambda j, *_: (j, 0)),
                      pl.BlockSpec((1, D_MODEL), lambda j, *_: (0, 0)), pl.BlockSpec(memory_space=pl.ANY)],
            out_specs=pl.BlockSpec((tb, D_MODEL), lambda j, *_: (j, 0)),
            scratch_shapes=[pltpu.VMEM((2, r_blk, D_MODEL), BF16), pltpu.VMEM((2, r_blk, tb), BF16),
                            pltpu.SemaphoreType.DMA((2,))]),
        out_shape=jax.ShapeDtypeStruct((T, D_MODEL), F32),
        compiler_params=_params(1),
        name="moe_combine",
    )(*plan, idx, rank, wt, idx, rank, wt, h2, w["g_final"], out)


def _rope_table(pos):
    half = QK_ROPE // 2
    inv = ROPE_THETA ** (-jnp.arange(half, dtype=F32) * 2.0 / QK_ROPE)
    ang = pos[:, None] * inv[None, :]
    cos, sin, zero = jnp.cos(ang), jnp.sin(ang), jnp.zeros_like(ang)
    return jnp.concatenate([cos, cos, cos, cos, -sin, zero, -sin, zero, zero, sin, zero, sin], axis=1)


def _prepare_weights(g_attn, w_in, g_q, g_kv, w_uq, w_uk, w_uv, w_a2, b_a, g_gla, w_o_mla, w_o_gla, w_out,
                     g_mem, w_mk, w_mv, g_cross, w_mq, w_mo, g_ffn, w_router, b_router, w_gate_up, b_gate_up,
                     w_down, b_down, g_final):
    splits = np.cumsum((0,) + IN_SPLITS)
    parts = [w_in[0][:, splits[i]:splits[i + 1]] for i in range(len(IN_SPLITS))]
    parts = [jnp.pad(p, ((0, 0), (0, wd - p.shape[1]))) for p, wd in zip(parts, _PACKED_WIDTHS)]
    wq = w_uq[0].reshape(Q_LORA, MLA_HEADS, QK_NOPE + QK_ROPE)
    wq_rope = jnp.pad(wq[:, :, QK_NOPE:], ((0, 0), (0, 0), (0, LANES - QK_ROPE)))
    row = lambda v: v.reshape(1, -1)
    return dict(
        g_attn=row(g_attn[0]), w_in=jnp.concatenate(parts, axis=1).astype(BF16), g_q=row(g_q[0]), g_kv=row(g_kv[0]),
        w_uq=jnp.concatenate([_fold_latent_query(jnp.transpose(wq[:, :, :QK_NOPE], (1, 0, 2)),
                                                 jnp.transpose(w_uk[0], (1, 0, 2))),
                              wq_rope.reshape(Q_LORA, -1).astype(BF16)], axis=1),
        wuv=jnp.transpose(w_uv[0], (1, 0, 2)).astype(BF16),
        w_a2=jnp.pad(w_a2[0], ((0, LANES - GLA_RANK), (0, 0))).astype(BF16), b_a=row(b_a[0]),
        g_gla=row(g_gla[0]), w_o_mla=w_o_mla[0].astype(BF16), w_o_gla=w_o_gla[0].astype(BF16),
        w_out=w_out[0].astype(BF16), g_mem=row(g_mem[0]),
        w_mkv=jnp.concatenate([w_mk[0], w_mv[0]], axis=1).astype(BF16),
        g_cross=row(g_cross[0]), w_mq=w_mq[0].astype(BF16), w_mo=w_mo[0].astype(BF16), g_ffn=row(g_ffn[0]),
        w_router_t=w_router[0].T, b_router=b_router[0].reshape(-1, 1),
        w_gate_up=w_gate_up[0], b_gate_up=b_gate_up[0].reshape(N_EXPERTS, 1, 2 * D_FF),
        w_down=w_down[0], b_down=b_down[0].reshape(N_EXPERTS, 1, D_MODEL), g_final=row(g_final),
    )


def kernel(x_prompt, x_sample, cache_kv_latent, cache_k_rope, state_gla, cache_mem_k, cache_mem_v, page_table, mem_prompt, g_attn, w_in, g_q, g_kv, w_uq, w_uk, w_uv, w_a2, b_a, g_gla, w_o_mla, w_o_gla, w_out, g_mem, w_mk, w_mv, g_cross, w_mq, w_mo, g_ffn, w_router, b_router, w_gate_up, b_gate_up, w_down, b_down, g_final):
    w = _prepare_weights(g_attn, w_in, g_q, g_kv, w_uq, w_uk, w_uv, w_a2, b_a, g_gla, w_o_mla, w_o_gla, w_out,
                         g_mem, w_mk, w_mv, g_cross, w_mq, w_mo, g_ffn, w_router, b_router, w_gate_up, b_gate_up,
                         w_down, b_down, g_final)
    B, S, D = x_prompt.shape
    Bd = x_sample.shape[0]
    past_len = page_table.shape[1] * cache_kv_latent.shape[2]

    tm = min(256, S)
    xp = x_prompt.reshape(B * S, D)
    (qcat, ckv, krope, kcat, kcat_t, gq, gk, gv, sg, la, szm, szg) = _premix(
        xp, _rope_table(jnp.arange(S, dtype=F32)), w, min(PREMIX_ROWS, S), tm, BF16)
    olat = _mla_prompt(qcat.reshape(MLA_HEADS, B, S, QCAT), kcat.reshape(B, S, QCAT), kcat_t, tm)
    og, state_p = _gla_prompt(gq.reshape(B, S, -1), gk.reshape(B, S, -1), gv.reshape(B, S, -1), la.reshape(B, S, -1))
    h1, qx = _postmix(xp, olat.reshape(B * S, -1), og.reshape(B * S, -1), sg, szm, szg, w, tm)
    mem_k, mem_v = _memkv(mem_prompt.reshape(B * N_MEM, D), w["g_mem"], w["w_mkv"], min(512, B * N_MEM))
    tb = min(MOE_BLOCK, S)
    h2, xn2, idx, rank, wt, cnt = _cross_prompt(qx, h1, mem_k.reshape(B, N_MEM, MEM_WIDTH),
                                                mem_v.reshape(B, N_MEM, MEM_WIDTH), w, tb)
    y_prompt = _moe(xn2, idx, rank, wt, cnt.reshape(-1, N_EXPERTS), h2, w, tb, tb)

    xs = x_sample.reshape(Bd, D)
    (qcat_s, ckv_s, krope_s, kcat_s, _, gq_s, gk_s, gv_s, sg_s, la_s, szm_s, szg_s) = _premix(
        xs, _rope_table(jnp.full((Bd,), past_len, F32)), w, Bd, Bd, F32)
    olat_s = _mla_sample(page_table, qcat_s, kcat_s, cache_kv_latent[0], jnp.swapaxes(cache_k_rope[0], 1, 2))
    og_s, state_s = _gla_sample(gq_s, gk_s, gv_s, la_s, state_gla[0])
    h1_s, qx_s = _postmix(xs, olat_s.reshape(Bd, -1), og_s, sg_s, szm_s, szg_s, w, Bd)
    h2_s, xn2_s, idx_s, rank_s, wt_s, cnt_s = _cross_sample(
        qx_s, h1_s, cache_mem_k, cache_mem_v, w)
    y_sample = _moe(xn2_s, idx_s, rank_s, wt_s, cnt_s.reshape(-1, N_EXPERTS), h2_s, w, Bd, Bd)

    return (y_prompt.reshape(B, S, D), y_sample.reshape(Bd, 1, D),
            ckv.reshape(1, B, S, KV_LORA), krope.reshape(1, B, S, QK_ROPE), state_p[None],
            mem_k.reshape(1, B, N_MEM, MEM_HEADS, MEM_HEAD_DIM), mem_v.reshape(1, B, N_MEM, MEM_HEADS, MEM_HEAD_DIM),
            ckv_s.reshape(1, Bd, 1, KV_LORA), krope_s.reshape(1, Bd, 1, QK_ROPE), state_s[None])
```

```python
import functools

import jax
import jax.numpy as jnp
import numpy as np
from jax import lax
from jax.experimental import pallas as pl
from jax.experimental.pallas import tpu as pltpu

F32 = jnp.float32
BF16 = jnp.bfloat16

D_MODEL = 1024
MLA_HEADS = 8
Q_LORA = 256
KV_LORA = 128
QK_NOPE = 128
QK_ROPE = 64
ROPE_THETA = 10000.0
ATTN_SCALE = (QK_NOPE + QK_ROPE) ** -0.5
GLA_HEADS = 4
GLA_DK = 128
GLA_DV = 256
GLA_RANK = 16
GLA_TAU = 16.0
GLA_CHUNK = 64
GLA_GROUP = 4
N_MEM = 256
MEM_HEADS = 4
MEM_HEAD_DIM = 128
N_EXPERTS = 32
TOP_K = 4
D_FF = 1024
SWIGLU_LIMIT = 7.0
SWIGLU_ALPHA = 1.702
EPS = 1e-6

GLA_K_WIDTH = GLA_HEADS * GLA_DK
GLA_V_WIDTH = GLA_HEADS * GLA_DV
MEM_WIDTH = MEM_HEADS * MEM_HEAD_DIM
IN_SPLITS = (Q_LORA, KV_LORA, QK_ROPE, GLA_K_WIDTH, GLA_K_WIDTH, GLA_V_WIDTH, GLA_V_WIDTH, GLA_RANK, D_MODEL, D_MODEL)

LANES = 128
_PACKED_WIDTHS = (Q_LORA, KV_LORA, LANES, GLA_K_WIDTH, GLA_K_WIDTH, GLA_V_WIDTH, GLA_V_WIDTH, LANES, D_MODEL, D_MODEL)
_OFF = tuple(int(v) for v in np.cumsum((0,) + _PACKED_WIDTHS))
QCAT = 2 * LANES

VMEM_LIMIT = 56 << 20
PREMIX_ROWS = 512


def _params(n_axes):
    return pltpu.CompilerParams(dimension_semantics=("arbitrary",) * n_axes, vmem_limit_bytes=VMEM_LIMIT)


def _rms(x, g):
    var = jnp.mean(x * x, axis=-1, keepdims=True)
    return x * lax.rsqrt(var + EPS) * g


def _dot(a, b):
    return jnp.dot(a.astype(BF16), b.astype(BF16), preferred_element_type=F32)


def _dot_nt(a, b):
    return lax.dot_general(a.astype(BF16), b.astype(BF16), (((1,), (1,)), ((), ())), preferred_element_type=F32)


def _dot_tn(a, b):
    return lax.dot_general(a.astype(BF16), b.astype(BF16), (((0,), (0,)), ((), ())), preferred_element_type=F32)


def _split3(x):
    hi = x.astype(BF16)
    r1 = x - hi.astype(F32)
    mid = r1.astype(BF16)
    lo = (r1 - mid.astype(F32)).astype(BF16)
    return hi, mid, lo


def _dot_f32_nt(a, b):
    a0, a1, a2 = _split3(a)
    b0, b1, b2 = _split3(b)
    d = lambda u, v: lax.dot_general(u, v, (((1,), (1,)), ((), ())), preferred_element_type=F32)
    return ((d(a1, b1) + d(a0, b2) + d(a2, b0)) + (d(a0, b1) + d(a1, b0))) + d(a0, b0)


def _full(shape):
    n = len(shape)
    return pl.BlockSpec(shape, lambda *_: (0,) * n)


def _premix_kernel(x_ref, tab_ref, g_attn_ref, w_in_ref, g_q_ref, g_kv_ref, w_uq_ref, w_a2_ref, b_a_ref,
                   qcat_ref, ckv_ref, krope_ref, kcat_ref, kcat_t_ref, gq_ref, gk_ref, gv_ref, sg_ref, la_ref, szm_ref, szg_ref):
    xn = _rms(x_ref[...], g_attn_ref[...]).astype(BF16)

    def proj(lo, hi):
        y = jnp.dot(xn, w_in_ref[:, _OFF[lo]:_OFF[hi]], preferred_element_type=F32)
        return [y[:, _OFF[i] - _OFF[lo]:_OFF[i + 1] - _OFF[lo]] for i in range(lo, hi)]

    tab = tab_ref[...]
    cos, sin_lo, sin_hi = tab[:, :LANES], tab[:, LANES:2 * LANES], tab[:, 2 * LANES:]

    def rope(t):
        return t * cos + pltpu.roll(t, LANES - QK_ROPE // 2, 1) * sin_lo + pltpu.roll(t, QK_ROPE // 2, 1) * sin_hi

    cq, ckv, kr = proj(0, 3)
    c_q = _rms(cq, g_q_ref[...]).astype(BF16)
    ckv = _rms(ckv, g_kv_ref[...])
    kr = rope(kr)
    ckv_ref[...] = ckv
    krope_ref[...] = kr[:, :QK_ROPE]
    kcat = jnp.concatenate([ckv, kr], axis=1)
    kcat_ref[...] = kcat.astype(BF16)
    tk = kcat_t_ref.shape[2]
    for t in range(kcat_t_ref.shape[0]):
        kcat_t_ref[t] = kcat[t * tk:(t + 1) * tk].T.astype(BF16)
    q_all = jnp.dot(c_q, w_uq_ref[...], preferred_element_type=F32)
    for h in range(MLA_HEADS):
        qcat_ref[h, :, :LANES] = q_all[:, h * LANES:(h + 1) * LANES].astype(qcat_ref.dtype)
        qcat_ref[h, :, LANES:] = rope(q_all[:, (MLA_HEADS + h) * LANES:(MLA_HEADS + h + 1) * LANES]).astype(qcat_ref.dtype)
    gq, gk = proj(3, 5)
    gq_ref[...] = (gq * GLA_DK ** -0.5).astype(gq_ref.dtype)
    gk_ref[...] = gk.astype(gk_ref.dtype)
    gv_ref[...] = proj(5, 6)[0].astype(gv_ref.dtype)
    gg, ga = proj(6, 8)
    sg_ref[...] = (gg * jax.nn.sigmoid(gg)).astype(sg_ref.dtype)
    z = _dot(ga, w_a2_ref[...]) + b_a_ref[...]
    la_ref[...] = -(jnp.maximum(-z, 0.0) + jnp.log1p(jnp.exp(-jnp.abs(z)))) * (1.0 / GLA_TAU)
    szm_ref[...] = jax.nn.sigmoid(proj(8, 9)[0]).astype(szm_ref.dtype)
    szg_ref[...] = jax.nn.sigmoid(proj(9, 10)[0]).astype(szg_ref.dtype)


def _fold_kernel(wq_ref, wk_ref, o_ref):
    o_ref[...] = _dot_f32_nt(wq_ref[...], wk_ref[...]).astype(o_ref.dtype)


def _fold_latent_query(w_uq_nope, w_uk):
    return pl.pallas_call(
        _fold_kernel,
        grid=(MLA_HEADS,),
        in_specs=[pl.BlockSpec((None, Q_LORA, QK_NOPE), lambda h: (h, 0, 0)),
                  pl.BlockSpec((None, KV_LORA, QK_NOPE), lambda h: (h, 0, 0))],
        out_specs=pl.BlockSpec((Q_LORA, KV_LORA), lambda h: (0, h)),
        out_shape=jax.ShapeDtypeStruct((Q_LORA, MLA_HEADS * KV_LORA), BF16),
        compiler_params=_params(1),
        name="fold_latent_query",
    )(w_uq_nope, w_uk)


def _premix(x, tab, w, tm, tk, gdtype):
    T = x.shape[0]
    n_tab = tab.shape[0] // tm
    row = lambda n: pl.BlockSpec((tm, n), lambda i: (i, 0))
    outs = [(KV_LORA, F32), (QK_ROPE, F32), (QCAT, BF16), None, (GLA_K_WIDTH, gdtype),
            (GLA_K_WIDTH, gdtype), (GLA_V_WIDTH, gdtype), (GLA_V_WIDTH, BF16), (GLA_K_WIDTH, F32),
            (D_MODEL, BF16), (D_MODEL, BF16)]
    out_specs = [row(o[0]) if o else pl.BlockSpec((tm // tk, QCAT, tk), lambda i: (i, 0, 0)) for o in outs]
    out_shape = [jax.ShapeDtypeStruct((T, o[0]) if o else (T // tk, QCAT, tk), o[1] if o else BF16) for o in outs]
    consts = [w["g_attn"], w["w_in"], w["g_q"], w["g_kv"], w["w_uq"], w["w_a2"], w["b_a"]]
    const_specs = [pl.BlockSpec(c.shape, lambda i, n=len(c.shape): (0,) * n,
                                pipeline_mode=pl.Buffered(1) if c is w["w_in"] else None) for c in consts]
    return pl.pallas_call(
        _premix_kernel,
        grid=(T // tm,),
        in_specs=[row(D_MODEL), pl.BlockSpec((tm, 3 * LANES), lambda i: (i % n_tab, 0))] + const_specs,
        out_specs=[pl.BlockSpec((MLA_HEADS, tm, QCAT), lambda i: (0, i, 0))] + out_specs,
        out_shape=[jax.ShapeDtypeStruct((MLA_HEADS, T, QCAT), BF16)] + out_shape,
        compiler_params=_params(1),
        name="premix",
    )(x, tab, *consts)


def _mla_prompt_kernel(q_ref, k_ref, kt_ref, o_ref, m_ref, l_ref, acc_ref):
    tq = q_ref.shape[1]
    qi = pl.program_id(1)
    q = q_ref[...].reshape(MLA_HEADS * tq, QCAT)
    row = lax.broadcasted_iota(jnp.int32, (MLA_HEADS, tq, tq), 1).reshape(MLA_HEADS * tq, tq)
    col = lax.broadcasted_iota(jnp.int32, (MLA_HEADS * tq, tq), 1)
    m_ref[...] = jnp.full_like(m_ref, -jnp.inf)
    l_ref[...] = jnp.zeros_like(l_ref)
    acc_ref[...] = jnp.zeros_like(acc_ref)
    c = ATTN_SCALE * np.log2(np.e)

    def step(j, n, mask_last):
        keys = pl.ds(pl.multiple_of(j * tq, tq), n * tq)
        parts = [jnp.dot(q, kt_ref[j + t], preferred_element_type=F32) for t in range(n)]
        if mask_last:
            parts[-1] = jnp.where(col <= row, parts[-1], -jnp.inf)
        s = jnp.concatenate(parts, axis=1)
        m = m_ref[...]
        m_new = jnp.maximum(m, jnp.max(s, axis=-1, keepdims=True))
        alpha = jnp.exp2((m - m_new) * c)
        p = jnp.exp2((s - jnp.concatenate([m_new] * (n * tq // LANES), axis=1)) * c)
        p_lanes = p[:, :LANES]
        for t in range(1, n * tq // LANES):
            p_lanes = p_lanes + p[:, t * LANES:(t + 1) * LANES]
        l_ref[...] = alpha * l_ref[...] + p_lanes
        acc_ref[...] = alpha * acc_ref[...] + _dot(p, k_ref[keys, :KV_LORA])
        m_ref[...] = m_new

    def body(jj, carry):
        step(2 * jj, 2, False)
        return carry

    lax.fori_loop(0, qi // 2, body, 0)

    @pl.when(qi % 2 == 0)
    def _():
        step(qi, 1, True)

    @pl.when(qi % 2 == 1)
    def _():
        step(qi - 1, 2, True)
    o = acc_ref[...] / jnp.sum(l_ref[...], axis=-1, keepdims=True)
    for h in range(MLA_HEADS):
        o_ref[:, h * KV_LORA:(h + 1) * KV_LORA] = o[h * tq:(h + 1) * tq].astype(o_ref.dtype)


def _mla_prompt(qcat, kcat, kcat_t, tq):
    _, B, S, _ = qcat.shape
    rows = MLA_HEADS * tq
    return pl.pallas_call(
        _mla_prompt_kernel,
        grid=(B, S // tq),
        in_specs=[pl.BlockSpec((MLA_HEADS, None, tq, QCAT), lambda b, i: (0, b, i, 0)),
                  pl.BlockSpec((None, S, QCAT), lambda b, i: (b, 0, 0)),
                  pl.BlockSpec((S // tq, QCAT, tq), lambda b, i: (b, 0, 0))],
        out_specs=pl.BlockSpec((None, tq, MLA_HEADS * KV_LORA), lambda b, i: (b, i, 0)),
        out_shape=jax.ShapeDtypeStruct((B, S, MLA_HEADS * KV_LORA), BF16),
        scratch_shapes=[pltpu.VMEM((rows, LANES), F32), pltpu.VMEM((rows, LANES), F32), pltpu.VMEM((rows, KV_LORA), F32)],
        compiler_params=_params(2),
        name="mla_prompt",
    )(qcat, kcat, kcat_t)


def _mla_sample_kernel(pt_ref, q_ref, knew_ref, lat_hbm, rope_hbm, o_ref, lat_buf, rope_buf, lat_bf, rope_bf, sems):
    b = pl.program_id(0)
    n_pages, page = lat_buf.shape[1], lat_buf.shape[2]

    def page_copies(sample, slot, i):
        pg = pt_ref[sample, i]
        return (pltpu.make_async_copy(lat_hbm.at[pg], lat_buf.at[slot, i], sems.at[0, slot]),
                pltpu.make_async_copy(rope_hbm.at[pg], rope_buf.at[slot, i], sems.at[1, slot]))

    def start_pages(sample, slot):
        def body(i, carry):
            for cp in page_copies(sample, slot, i):
                cp.start()
            return carry
        lax.fori_loop(0, n_pages, body, 0)

    slot = b % 2

    @pl.when(b == 0)
    def _():
        start_pages(0, 0)

    @pl.when(b + 1 < pl.num_programs(0))
    def _():
        start_pages(b + 1, 1 - slot)

    pltpu.make_async_copy(lat_hbm.at[pl.ds(0, n_pages)], lat_buf.at[slot], sems.at[0, slot]).wait()
    pltpu.make_async_copy(rope_hbm.at[pl.ds(0, n_pages)], rope_buf.at[slot], sems.at[1, slot]).wait()

    lat_bf[...] = lat_buf[slot].reshape(n_pages * page, KV_LORA).astype(BF16)
    for i in range(n_pages):
        rope_bf[:, i * page:(i + 1) * page] = rope_buf[slot, i].astype(BF16)
    q = q_ref[...].reshape(MLA_HEADS, QCAT)
    knew = knew_ref[...].astype(F32)
    s = (_dot_nt(q[:, :KV_LORA], lat_bf[...]) + _dot(q[:, KV_LORA:KV_LORA + QK_ROPE], rope_bf[...])) * ATTN_SCALE
    s_new = jnp.sum(q.astype(F32) * knew, axis=-1, keepdims=True) * ATTN_SCALE
    m = jnp.maximum(jnp.max(s, axis=-1, keepdims=True), s_new)
    p = jnp.exp(s - m)
    p_new = jnp.exp(s_new - m)
    denom = jnp.sum(p, axis=-1, keepdims=True) + p_new
    o = _dot(p, lat_bf[...]) + p_new * knew[:, :KV_LORA]
    o_ref[...] = (o / denom).astype(o_ref.dtype)


def _mla_sample(page_table, qcat, knew, cache_lat, cache_rope_t):
    Bd, n_pages = page_table.shape
    page = cache_lat.shape[1]
    grid_spec = pltpu.PrefetchScalarGridSpec(
        num_scalar_prefetch=1,
        grid=(Bd,),
        in_specs=[pl.BlockSpec((MLA_HEADS, None, 1, QCAT), lambda b, pt: (0, b, 0, 0)),
                  pl.BlockSpec((None, 1, QCAT), lambda b, pt: (b, 0, 0)),
                  pl.BlockSpec(memory_space=pl.ANY), pl.BlockSpec(memory_space=pl.ANY)],
        out_specs=pl.BlockSpec((None, MLA_HEADS, KV_LORA), lambda b, pt: (b, 0, 0)),
        scratch_shapes=[pltpu.VMEM((2, n_pages, page, KV_LORA), F32), pltpu.VMEM((2, n_pages, QK_ROPE, page), F32),
                        pltpu.VMEM((n_pages * page, KV_LORA), BF16), pltpu.VMEM((QK_ROPE, n_pages * page), BF16),
                        pltpu.SemaphoreType.DMA((2, 2))],
    )
    return pl.pallas_call(
        _mla_sample_kernel,
        grid_spec=grid_spec,
        out_shape=jax.ShapeDtypeStruct((Bd, MLA_HEADS, KV_LORA), BF16),
        compiler_params=_params(1),
        name="mla_sample",
    )(page_table, qcat.reshape(MLA_HEADS, Bd, 1, QCAT), knew.reshape(Bd, 1, QCAT), cache_lat, cache_rope_t)


def _gla_prompt_kernel(q_ref, k_ref, v_ref, la_ref, o_ref, state_ref, st_ref):
    C = GLA_CHUNK
    S = q_ref.shape[0]
    row = lax.broadcasted_iota(jnp.int32, (C, 2 * C), 0)
    col = lax.broadcasted_iota(jnp.int32, (C, 2 * C), 1)
    causal = row >= col
    tri = causal[:, :C].astype(BF16)
    pad_k = jnp.zeros((C, GLA_DK), F32)
    pad_v = jnp.zeros((C, GLA_DV), F32)
    st_ref[...] = jnp.zeros_like(st_ref)

    group = min(GLA_GROUP, S // C)

    def chunks(c, carry):
        parts = []
        for g in range(group):
            rows = pl.ds(pl.multiple_of((c * group + g) * C, C), C)
            split = jnp.dot(tri, jnp.concatenate(_split3(la_ref[rows, :]), axis=1), preferred_element_type=F32)
            b_all = (split[:, 2 * GLA_K_WIDTH:] + split[:, GLA_K_WIDTH:2 * GLA_K_WIDTH]) + split[:, :GLA_K_WIDTH]
            for h in range(GLA_HEADS):
                kcols = slice(h * GLA_DK, (h + 1) * GLA_DK)
                vcols = slice(h * GLA_DV, (h + 1) * GLA_DV)
                b = b_all[:, kcols]
                b_last = b[C - 1:C, :]
                q = q_ref[rows, kcols].astype(F32)
                k = k_ref[rows, kcols].astype(F32)
                q_in = (q * jnp.exp(b)).astype(BF16)
                k_in = jnp.concatenate([k * jnp.exp(-b), pad_k], axis=0)
                k_out = jnp.concatenate([k * jnp.exp(b_last - b), pad_k], axis=0)
                a = jnp.where(causal, _dot_nt(q_in, k_in), 0.0)
                v_t = jnp.concatenate([v_ref[rows, vcols].astype(F32), pad_v], axis=0).T.astype(BF16)
                lhs = jnp.concatenate([q_in, a.astype(BF16)], axis=1)
                parts.append((rows, h, vcols, lhs, v_t, jnp.exp(b_last), _dot(v_t, k_out)))
        for rows, h, vcols, lhs, v_t, decay, kv_t in parts:
            st = st_ref[h]
            o_ref[rows, vcols] = _dot_nt(lhs, jnp.concatenate([st.astype(BF16), v_t], axis=1)).astype(o_ref.dtype)
            st_ref[h] = st * decay + kv_t
        return carry

    lax.fori_loop(0, S // C // group, chunks, 0)
    for h in range(GLA_HEADS):
        state_ref[h] = st_ref[h].T


def _gla_prompt(q, k, v, la):
    B, S, _ = q.shape
    seq = lambda n: pl.BlockSpec((None, S, n), lambda b: (b, 0, 0))
    return pl.pallas_call(
        _gla_prompt_kernel,
        grid=(B,),
        in_specs=[seq(GLA_K_WIDTH), seq(GLA_K_WIDTH), seq(GLA_V_WIDTH), seq(GLA_K_WIDTH)],
        out_specs=[seq(GLA_V_WIDTH), pl.BlockSpec((None, GLA_HEADS, GLA_DK, GLA_DV), lambda b: (b, 0, 0, 0))],
        out_shape=[jax.ShapeDtypeStruct((B, S, GLA_V_WIDTH), BF16),
                   jax.ShapeDtypeStruct((B, GLA_HEADS, GLA_DK, GLA_DV), F32)],
        scratch_shapes=[pltpu.VMEM((GLA_HEADS, GLA_DV, GLA_DK), F32)],
        compiler_params=_params(1),
        name="gla_prompt",
    )(q, k, v, la)


def _gla_sample_kernel(q_ref, k_ref, v_ref, la_ref, st_ref, o_ref, sto_ref):
    nb = q_ref.shape[0]
    pad = jnp.zeros((GLA_DK - nb, GLA_DK), F32)
    for h in range(GLA_HEADS):
        kcols = slice(h * GLA_DK, (h + 1) * GLA_DK)
        vcols = slice(h * GLA_DV, (h + 1) * GLA_DV)
        col = lambda ref: jnp.concatenate([ref[:, kcols].astype(F32), pad], axis=0).T
        q_t, k_t, decay_t = col(q_ref), col(k_ref), jnp.exp(col(la_ref))
        for i in range(nb):
            new = decay_t[:, i:i + 1] * st_ref[i, h] + k_t[:, i:i + 1] * v_ref[i:i + 1, vcols].astype(F32)
            sto_ref[i, h] = new
            o_ref[i:i + 1, vcols] = jnp.sum(q_t[:, i:i + 1] * new, axis=0, keepdims=True).astype(o_ref.dtype)


def _gla_sample(q, k, v, la, state, nb=8):
    Bd = q.shape[0]
    rows = lambda n: pl.BlockSpec((nb, n), lambda i: (i, 0))
    st_spec = pl.BlockSpec((nb, GLA_HEADS, GLA_DK, GLA_DV), lambda i: (i, 0, 0, 0))
    return pl.pallas_call(
        _gla_sample_kernel,
        grid=(Bd // nb,),
        in_specs=[rows(GLA_K_WIDTH), rows(GLA_K_WIDTH), rows(GLA_V_WIDTH), rows(GLA_K_WIDTH), st_spec],
        out_specs=[rows(GLA_V_WIDTH), st_spec],
        out_shape=[jax.ShapeDtypeStruct((Bd, GLA_V_WIDTH), BF16), jax.ShapeDtypeStruct(state.shape, F32)],
        compiler_params=_params(1),
        name="gla_sample",
    )(q, k, v, la, state)


def _postmix_kernel(x_ref, olat_ref, og_ref, sg_ref, szm_ref, szg_ref, wuv_ref, w_o_mla_ref, g_gla_ref, w_o_gla_ref,
                    w_out_ref, g_cross_ref, w_mq_ref, h_ref, qx_ref):
    ov = jnp.concatenate(
        [_dot(olat_ref[:, h * KV_LORA:(h + 1) * KV_LORA], wuv_ref[h]).astype(BF16) for h in range(MLA_HEADS)], axis=1)
    o_mla = jnp.dot(ov, w_o_mla_ref[...], preferred_element_type=F32)
    g_gla = g_gla_ref[...]
    og = jnp.concatenate(
        [(_rms(og_ref[:, h * GLA_DV:(h + 1) * GLA_DV].astype(F32), g_gla)
          * sg_ref[:, h * GLA_DV:(h + 1) * GLA_DV].astype(F32)).astype(BF16) for h in range(GLA_HEADS)], axis=1)
    o_gla = jnp.dot(og, w_o_gla_ref[...], preferred_element_type=F32)
    merged = szm_ref[...].astype(F32) * o_mla + szg_ref[...].astype(F32) * o_gla
    h1 = x_ref[...] + _dot(merged, w_out_ref[...])
    h_ref[...] = h1
    qx_ref[...] = _dot(_rms(h1, g_cross_ref[...]), w_mq_ref[...]).astype(qx_ref.dtype)


def _postmix(x, olat, og, sg, szm, szg, w, tm):
    T = x.shape[0]
    row = lambda n: pl.BlockSpec((tm, n), lambda i: (i, 0))
    consts = [w["wuv"], w["w_o_mla"], w["g_gla"], w["w_o_gla"], w["w_out"], w["g_cross"], w["w_mq"]]
    return pl.pallas_call(
        _postmix_kernel,
        grid=(T // tm,),
        in_specs=[row(D_MODEL)] * 6 + [_full(c.shape) for c in consts],
        out_specs=[row(D_MODEL), row(MEM_WIDTH)],
        out_shape=[jax.ShapeDtypeStruct((T, D_MODEL), F32), jax.ShapeDtypeStruct((T, MEM_WIDTH), BF16)],
        compiler_params=_params(1),
        name="postmix",
    )(x, olat, og, sg, szm, szg, *consts)


def _memkv_kernel(mem_ref, g_ref, w_ref, k_ref, v_ref):
    kv = _dot(_rms(mem_ref[...], g_ref[...]), w_ref[...])
    k_ref[...] = kv[:, :MEM_WIDTH]
    v_ref[...] = kv[:, MEM_WIDTH:]


def _memkv(mem, g_mem, w_mkv, tm=512):
    T = mem.shape[0]
    row = lambda n: pl.BlockSpec((tm, n), lambda i: (i, 0))
    return pl.pallas_call(
        _memkv_kernel,
        grid=(T // tm,),
        in_specs=[row(D_MODEL), _full(g_mem.shape), _full(w_mkv.shape)],
        out_specs=[row(MEM_WIDTH), row(MEM_WIDTH)],
        out_shape=[jax.ShapeDtypeStruct((T, MEM_WIDTH), F32)] * 2,
        compiler_params=_params(1),
        name="memkv",
    )(mem, g_mem, w_mkv)


def _attend_many(problems):
    scores = [_dot_nt(q, k) * MEM_HEAD_DIM ** -0.5 for q, k, _ in problems]
    probs = []
    for s in scores:
        p = jnp.exp(s - jnp.max(s, axis=-1, keepdims=True))
        probs.append(p / jnp.sum(p, axis=-1, keepdims=True))
    return [_dot(p, v) for p, (_, _, v) in zip(probs, problems)]


def _head_cols(h):
    return slice(h * MEM_HEAD_DIM, (h + 1) * MEM_HEAD_DIM)


def _route(o, h1, w_mo_ref, g_ffn_ref, w_router_ref, b_router_ref, h_ref, xn_ref, idx_ref, rank_ref, wt_ref, cnt_ref):
    h2 = h1 + jnp.dot(o, w_mo_ref[...], preferred_element_type=F32)
    h_ref[...] = h2
    xn = _rms(h2, g_ffn_ref[...])
    xn_ref[...] = xn.astype(xn_ref.dtype)
    tb = xn.shape[0]
    logits = _dot_f32_nt(w_router_ref[...], xn) + b_router_ref[...]
    expert = lax.broadcasted_iota(jnp.int32, logits.shape, 0)
    work = logits
    hits, firsts, exps = [], [], []
    top = None
    for _ in range(TOP_K):
        best = jnp.max(work, axis=0, keepdims=True)
        first = jnp.min(jnp.where(work == best, expert, N_EXPERTS), axis=0, keepdims=True)
        hit = expert == first
        top = best if top is None else top
        hits.append(hit)
        firsts.append(first)
        exps.append(jnp.exp(best - top))
        work = jnp.where(hit, -jnp.inf, work)
    denom = (exps[0] + exps[1]) + (exps[2] + exps[3])
    chosen = jnp.zeros(logits.shape, F32)
    for hit in hits:
        chosen = chosen + jnp.where(hit, 1.0, 0.0)
    before = (lax.broadcasted_iota(jnp.int32, (tb, tb), 0) < lax.broadcasted_iota(jnp.int32, (tb, tb), 1)).astype(BF16)
    rank = jnp.dot(chosen.astype(BF16), before, preferred_element_type=F32)
    idx_ref[...] = jnp.concatenate(firsts, axis=0)
    rank_ref[...] = jnp.concatenate(
        [jnp.sum(jnp.where(hit, rank, 0.0), axis=0, keepdims=True) for hit in hits], axis=0).astype(jnp.int32)
    wt_ref[...] = jnp.concatenate([e / denom for e in exps], axis=0)
    cnt_ref[...] = jnp.sum(chosen, axis=1, keepdims=True).astype(jnp.int32)


def _cross_prompt_kernel(qx_ref, h1_ref, mk_ref, mv_ref, *rest):
    outs = _attend_many([(qx_ref[:, _head_cols(h)], mk_ref[:, _head_cols(h)], mv_ref[:, _head_cols(h)])
                         for h in range(MEM_HEADS)])
    _route(jnp.concatenate([o.astype(BF16) for o in outs], axis=1), h1_ref[...], *rest)


def _cross_sample_kernel(qx_ref, mk_ref, mv_ref, o_ref):
    rows = mk_ref.shape[1]
    pad = jnp.zeros((8 - MEM_HEADS, MEM_HEAD_DIM), qx_ref.dtype)
    own_head = (lax.broadcasted_iota(jnp.int32, (8, rows), 1) % MEM_HEADS) == lax.broadcasted_iota(jnp.int32, (8, rows), 0)
    samples = range(qx_ref.shape[0])
    scores = [_dot_nt(jnp.concatenate([qx_ref[i:i + 1, _head_cols(h)] for h in range(MEM_HEADS)] + [pad], axis=0),
                      mk_ref[i]) * MEM_HEAD_DIM ** -0.5 for i in samples]
    probs = []
    for s in scores:
        s = jnp.where(own_head, s, -jnp.inf)
        p = jnp.exp(s - jnp.max(s, axis=-1, keepdims=True))
        probs.append(p / jnp.sum(p, axis=-1, keepdims=True))
    for i, p in zip(samples, probs):
        o = _dot(p, mv_ref[i])
        o_ref[i:i + 1, :] = jnp.concatenate([o[h:h + 1] for h in range(MEM_HEADS)], axis=1).astype(o_ref.dtype)


def _route_kernel(o_ref, h1_ref, *rest):
    _route(o_ref[...], h1_ref[...], *rest)


def _route_specs(T, tb, w):
    consts = [w["w_mo"], w["g_ffn"], w["w_router_t"], w["b_router"]]
    row = lambda n: pl.BlockSpec((tb, n), lambda i: (i, 0))
    per_choice = pl.BlockSpec((TOP_K, tb), lambda i: (0, i))
    out_specs = [row(D_MODEL), row(D_MODEL), per_choice, per_choice, per_choice,
                 pl.BlockSpec((None, N_EXPERTS, 1), lambda i: (i, 0, 0))]
    out_shape = [jax.ShapeDtypeStruct((T, D_MODEL), F32), jax.ShapeDtypeStruct((T, D_MODEL), BF16),
                 jax.ShapeDtypeStruct((TOP_K, T), jnp.int32), jax.ShapeDtypeStruct((TOP_K, T), jnp.int32),
                 jax.ShapeDtypeStruct((TOP_K, T), F32), jax.ShapeDtypeStruct((T // tb, N_EXPERTS, 1), jnp.int32)]
    return consts, out_specs, out_shape


def _cross_prompt(qx, h1, mem_k, mem_v, w, tb):
    T = qx.shape[0]
    blocks_per_mem = T // mem_k.shape[0] // tb
    consts, out_specs, out_shape = _route_specs(T, tb, w)
    row = lambda n: pl.BlockSpec((tb, n), lambda i: (i, 0))
    mem_spec = pl.BlockSpec((None, N_MEM, MEM_WIDTH), lambda i: (i // blocks_per_mem, 0, 0))
    return pl.pallas_call(
        _cross_prompt_kernel,
        grid=(T // tb,),
        in_specs=[row(MEM_WIDTH), row(D_MODEL), mem_spec, mem_spec] + [_full(c.shape) for c in consts],
        out_specs=out_specs,
        out_shape=out_shape,
        compiler_params=_params(1),
        name="cross_prompt",
    )(qx, h1, mem_k, mem_v, *consts)


def _cross_sample(qx, h1, mem_k, mem_v, w, nb=8):
    T = qx.shape[0]
    mem_spec = pl.BlockSpec((nb, N_MEM * MEM_HEADS, MEM_HEAD_DIM), lambda i: (i, 0, 0))
    rows = pl.BlockSpec((nb, MEM_WIDTH), lambda i: (i, 0))
    o = pl.pallas_call(
        _cross_sample_kernel,
        grid=(T // nb,),
        in_specs=[rows, mem_spec, mem_spec],
        out_specs=rows,
        out_shape=jax.ShapeDtypeStruct((T, MEM_WIDTH), BF16),
        compiler_params=_params(1),
        name="cross_sample",
    )(qx, mem_k, mem_v)
    consts, out_specs, out_shape = _route_specs(T, T, w)
    return pl.pallas_call(
        _route_kernel,
        grid=(1,),
        in_specs=[_full(o.shape), _full(h1.shape)] + [_full(c.shape) for c in consts],
        out_specs=out_specs,
        out_shape=out_shape,
        compiler_params=_params(1),
        name="route_sample",
    )(o, h1, *consts)


SEG_ALIGN = 16
SEL_CHUNK = 256
MOE_BLOCK = 512


def _block_rows(tb):
    worst = TOP_K * tb + N_EXPERTS * (SEG_ALIGN - 1)
    return -(-worst // SEL_CHUNK) * SEL_CHUNK


def _seg_sizes(limit):
    sizes, b = [], SEG_ALIGN
    while b <= limit:
        sizes.append(b)
        b *= 2
    return sizes[::-1]


def _for_segments(n, sizes, make_copy, act):
    def emit(group):
        for size in group:
            @pl.when((n & size) != 0)
            def _():
                act(make_copy(pl.multiple_of(n & (-2 * size), SEG_ALIGN), size))

    large = [size for size in sizes if size >= 8 * SEG_ALIGN]
    if len(large) > 1:
        pl.when(n >= large[-1])(lambda: emit(large))
    else:
        emit(large)
    emit([size for size in sizes if size < 8 * SEG_ALIGN])


def _slot_rows(idx, rank, loc_ref, j):
    pos = rank
    for e in range(N_EXPERTS):
        pos = pos + jnp.where(idx == e, loc_ref[j, e], 0)
    return pos


def _dispatch_kernel(loc_ref, pad_ref, goff_ref, total_ref, gap_off_ref, gap_len_ref, x_ref, idx_ref, rank_ref, xs_hbm,
                     buf, sem, *, tm):
    j = pl.program_id(0)
    last = pl.num_programs(0) - 1
    slot = j % 2
    tb = x_ref.shape[0]
    r_blk = _block_rows(tb)
    pos = _slot_rows(idx_ref[...], rank_ref[...], loc_ref, j)
    x = x_ref[...]
    for c in range(r_blk // SEL_CHUNK):
        r = lax.broadcasted_iota(jnp.int32, (SEL_CHUNK, tb), 0) + c * SEL_CHUNK
        sel = jnp.where(r == pos[0:1], 1.0, jnp.where(r == pos[1:2], 1.0, jnp.where(
            r == pos[2:3], 1.0, jnp.where(r == pos[3:4], 1.0, 0.0))))
        buf[slot, c * SEL_CHUNK:(c + 1) * SEL_CHUNK, :] = jnp.dot(
            sel.astype(BF16), x, preferred_element_type=F32).astype(BF16)

    sizes = _seg_sizes(tb)

    def start_segments(e, carry):
        src, dst = loc_ref[j, e], goff_ref[j, e]
        _for_segments(pad_ref[j, e], sizes, lambda off, size: pltpu.make_async_copy(
            buf.at[slot, pl.ds(pl.multiple_of(src + off, SEG_ALIGN), size)],
            xs_hbm.at[pl.ds(pl.multiple_of(dst + off, SEG_ALIGN), size)], sem.at[slot]), lambda cp: cp.start())
        return carry

    lax.fori_loop(0, N_EXPERTS, start_segments, 0)

    def wait_block(blk, slot):
        _for_segments(total_ref[blk], _seg_sizes(r_blk), lambda off, size: pltpu.make_async_copy(
            buf.at[slot, pl.ds(0, size)], xs_hbm.at[pl.ds(0, size)], sem.at[slot]), lambda cp: cp.wait())

    @pl.when(j > 0)
    def _():
        wait_block(j - 1, 1 - slot)

    @pl.when(j == last)
    def _():
        wait_block(j, slot)
        buf[1 - slot, 0:tm, :] = jnp.zeros((tm, D_MODEL), BF16)
        gap_sizes = _seg_sizes(tm - SEG_ALIGN)

        def for_gaps(act):
            def body(e, carry):
                dst = gap_off_ref[e]
                _for_segments(gap_len_ref[e], gap_sizes, lambda off, size: pltpu.make_async_copy(
                    buf.at[1 - slot, pl.ds(0, size)], xs_hbm.at[pl.ds(pl.multiple_of(dst + off, SEG_ALIGN), size)],
                    sem.at[1 - slot]), act)
                return carry
            lax.fori_loop(0, N_EXPERTS, body, 0)

        for_gaps(lambda cp: cp.start())
        for_gaps(lambda cp: cp.wait())


def _ffn_kernel(tile_expert_ref, n_valid_ref, x_ref, wgu_f32_ref, bgu_ref, wd_f32_ref, bd_ref, o_ref, wgu_ref, wd_ref):
    t = pl.program_id(0)

    @pl.when(t < n_valid_ref[0])
    def _():
        @pl.when((t == 0) | (tile_expert_ref[t] != tile_expert_ref[jnp.maximum(t - 1, 0)]))
        def _():
            wgu_ref[...] = wgu_f32_ref[...].astype(BF16)
            wd_ref[...] = wd_f32_ref[...].astype(BF16)

        x = x_ref[...]
        bgu = bgu_ref[...]
        gate = jnp.dot(x, wgu_ref[:, :D_FF], preferred_element_type=F32) + bgu[:, :D_FF]
        up = jnp.dot(x, wgu_ref[:, D_FF:], preferred_element_type=F32) + bgu[:, D_FF:]
        gate = jnp.minimum(gate, SWIGLU_LIMIT)
        up = jnp.clip(up, -SWIGLU_LIMIT, SWIGLU_LIMIT)
        hidden = (up + 1.0) * gate * jax.nn.sigmoid(SWIGLU_ALPHA * gate)
        o_ref[...] = (_dot(hidden, wd_ref[...]) + bd_ref[...]).astype(o_ref.dtype)


def _combine_kernel(loc_ref, pad_ref, goff_ref, total_ref, idx0_ref, rank0_ref, wt0_ref, idx_ref, rank_ref, wt_ref,
                    h_ref, g_final_ref, o_hbm, y_ref, buf, sel_ref, sem):
    j = pl.program_id(0)
    last = pl.num_programs(0) - 1
    slot = j % 2
    tb = h_ref.shape[0]
    r_blk = buf.shape[1]
    sizes = _seg_sizes(tb)

    def fetch(blk, slot):
        def body(e, carry):
            dst, src = loc_ref[blk, e], goff_ref[blk, e]
            _for_segments(pad_ref[blk, e], sizes, lambda off, size: pltpu.make_async_copy(
                o_hbm.at[pl.ds(pl.multiple_of(src + off, SEG_ALIGN), size)],
                buf.at[slot, pl.ds(pl.multiple_of(dst + off, SEG_ALIGN), size)], sem.at[slot]), lambda cp: cp.start())
            return carry
        lax.fori_loop(0, N_EXPERTS, body, 0)

    def build(blk, slot, idx, rank, wt):
        pos = _slot_rows(idx, rank, loc_ref, blk)
        for c in range(r_blk // SEL_CHUNK):
            r = lax.broadcasted_iota(jnp.int32, (SEL_CHUNK, tb), 0) + c * SEL_CHUNK
            sel = jnp.zeros((SEL_CHUNK, tb), F32)
            for k in range(TOP_K):
                sel = jnp.where(r == pos[k:k + 1], wt[k:k + 1], sel)
            sel_ref[slot, c * SEL_CHUNK:(c + 1) * SEL_CHUNK, :] = sel.astype(BF16)

    @pl.when(j == 0)
    def _():
        buf[...] = jnp.zeros_like(buf)
        fetch(0, 0)
        build(0, 0, idx0_ref[...], rank0_ref[...], wt0_ref[...])

    @pl.when(j < last)
    def _():
        fetch(j + 1, 1 - slot)

    _for_segments(total_ref[j], _seg_sizes(r_blk), lambda off, size: pltpu.make_async_copy(
        o_hbm.at[pl.ds(0, size)], buf.at[slot, pl.ds(0, size)], sem.at[slot]), lambda cp: cp.wait())
    moe = _dot_tn(sel_ref[slot], buf[slot])
    y_ref[...] = _rms(h_ref[...] + moe, g_final_ref[...])
    build(jnp.minimum(j + 1, last), 1 - slot, idx_ref[...], rank_ref[...], wt_ref[...])


def _moe(xn, idx, rank, wt, cnt, h2, w, tb, tm):
    T = xn.shape[0]
    nb = T // tb
    r_blk = _block_rows(tb)
    i32 = jnp.int32
    pad = (cnt + (SEG_ALIGN - 1)) // SEG_ALIGN * SEG_ALIGN
    loc = jnp.cumsum(pad, axis=1) - pad
    seg = jnp.sum(pad, axis=0)
    region = (seg + (tm - 1)) // tm * tm
    region_end = jnp.cumsum(region)
    region_start = region_end - region
    goff = region_start[None, :] + jnp.cumsum(pad, axis=0) - pad
    n_tiles = -(-(nb * (TOP_K * tb + N_EXPERTS * (SEG_ALIGN - 1)) + N_EXPERTS * (tm - 1)) // tm)
    n_valid = (region_end[-1] // tm).astype(i32).reshape(1)
    tile = jnp.minimum(jnp.arange(n_tiles, dtype=i32), n_valid - 1)
    tile_expert = jnp.minimum(jnp.sum((region_end[None, :] <= (tile * tm)[:, None]).astype(i32), axis=1), N_EXPERTS - 1)
    plan = [a.astype(i32) for a in (loc, pad, goff, jnp.sum(pad, axis=1))]
    gaps = [(region_start + seg).astype(i32), (region - seg).astype(i32)]

    choice = pl.BlockSpec((TOP_K, tb), lambda j, *_: (0, j))
    xs = pl.pallas_call(
        functools.partial(_dispatch_kernel, tm=tm),
        grid_spec=pltpu.PrefetchScalarGridSpec(
            num_scalar_prefetch=6, grid=(nb,),
            in_specs=[pl.BlockSpec((tb, D_MODEL), lambda j, *_: (j, 0)), choice, choice],
            out_specs=pl.BlockSpec(memory_space=pl.ANY),
            scratch_shapes=[pltpu.VMEM((2, r_blk, D_MODEL), BF16), pltpu.SemaphoreType.DMA((2,))]),
        out_shape=jax.ShapeDtypeStruct((n_tiles * tm, D_MODEL), BF16),
        compiler_params=_params(1),
        name="moe_dispatch",
    )(*plan, *gaps, xn, idx, rank)

    rows = pl.BlockSpec((tm, D_MODEL), lambda t, te, nv: (jnp.minimum(t, nv[0] - 1), 0))
    per_e = lambda a, b: pl.BlockSpec((None, a, b), lambda t, te, nv: (te[t], 0, 0))
    out = pl.pallas_call(
        _ffn_kernel,
        grid_spec=pltpu.PrefetchScalarGridSpec(
            num_scalar_prefetch=2, grid=(n_tiles,),
            in_specs=[rows, per_e(D_MODEL, 2 * D_FF), per_e(1, 2 * D_FF), per_e(D_FF, D_MODEL), per_e(1, D_MODEL)],
            out_specs=rows,
            scratch_shapes=[pltpu.VMEM((D_MODEL, 2 * D_FF), BF16), pltpu.VMEM((D_FF, D_MODEL), BF16)]),
        out_shape=jax.ShapeDtypeStruct((n_tiles * tm, D_MODEL), BF16),
        compiler_params=_params(1),
        name="moe_ffn",
    )(tile_expert, n_valid, xs, w["w_gate_up"], w["b_gate_up"], w["w_down"], w["b_down"])

    first = pl.BlockSpec((TOP_K, tb), lambda j, *_: (0, 0))
    ahead = pl.BlockSpec((TOP_K, tb), lambda j, *_: (0, jnp.minimum(j + 1, nb - 1)))
    return pl.pallas_call(
        _combine_kernel,
        grid_spec=pltpu.PrefetchScalarGridSpec(
            num_scalar_prefetch=4, grid=(nb,),
            in_specs=[first, first, first, ahead, ahead, ahead, pl.BlockSpec((tb, D_MODEL), lambda j, *_: (j, 0)),
                      pl.BlockSpec((1, D_MODEL), lambda j, *_: (0, 0)), pl.BlockSpec(memory_space=pl.ANY)],
            out_specs=pl.BlockSpec((tb, D_MODEL), lambda j, *_: (j, 0)),
            scratch_shapes=[pltpu.VMEM((2, r_blk, D_MODEL), BF16), pltpu.VMEM((2, r_blk, tb), BF16),
                            pltpu.SemaphoreType.DMA((2,))]),
        out_shape=jax.ShapeDtypeStruct((T, D_MODEL), F32),
        compiler_params=_params(1),
        name="moe_combine",
    )(*plan, idx, rank, wt, idx, rank, wt, h2, w["g_final"], out)


def _rope_table(pos):
    half = QK_ROPE // 2
    inv = ROPE_THETA ** (-jnp.arange(half, dtype=F32) * 2.0 / QK_ROPE)
    ang = pos[:, None] * inv[None, :]
    cos, sin, zero = jnp.cos(ang), jnp.sin(ang), jnp.zeros_like(ang)
    return jnp.concatenate([cos, cos, cos, cos, -sin, zero, -sin, zero, zero, sin, zero, sin], axis=1)


def _prepare_weights(g_attn, w_in, g_q, g_kv, w_uq, w_uk, w_uv, w_a2, b_a, g_gla, w_o_mla, w_o_gla, w_out,
                     g_mem, w_mk, w_mv, g_cross, w_mq, w_mo, g_ffn, w_router, b_router, w_gate_up, b_gate_up,
                     w_down, b_down, g_final):
    splits = np.cumsum((0,) + IN_SPLITS)
    parts = [w_in[0][:, splits[i]:splits[i + 1]] for i in range(len(IN_SPLITS))]
    parts = [jnp.pad(p, ((0, 0), (0, wd - p.shape[1]))) for p, wd in zip(parts, _PACKED_WIDTHS)]
    wq = w_uq[0].reshape(Q_LORA, MLA_HEADS, QK_NOPE + QK_ROPE)
    wq_rope = jnp.pad(wq[:, :, QK_NOPE:], ((0, 0), (0, 0), (0, LANES - QK_ROPE)))
    row = lambda v: v.reshape(1, -1)
    return dict(
        g_attn=row(g_attn[0]), w_in=jnp.concatenate(parts, axis=1).astype(BF16), g_q=row(g_q[0]), g_kv=row(g_kv[0]),
        w_uq=jnp.concatenate([_fold_latent_query(jnp.transpose(wq[:, :, :QK_NOPE], (1, 0, 2)),
                                                 jnp.transpose(w_uk[0], (1, 0, 2))),
                              wq_rope.reshape(Q_LORA, -1).astype(BF16)], axis=1),
        wuv=jnp.transpose(w_uv[0], (1, 0, 2)).astype(BF16),
        w_a2=jnp.pad(w_a2[0], ((0, LANES - GLA_RANK), (0, 0))).astype(BF16), b_a=row(b_a[0]),
        g_gla=row(g_gla[0]), w_o_mla=w_o_mla[0].astype(BF16), w_o_gla=w_o_gla[0].astype(BF16),
        w_out=w_out[0].astype(BF16), g_mem=row(g_mem[0]),
        w_mkv=jnp.concatenate([w_mk[0], w_mv[0]], axis=1).astype(BF16),
        g_cross=row(g_cross[0]), w_mq=w_mq[0].astype(BF16), w_mo=w_mo[0].astype(BF16), g_ffn=row(g_ffn[0]),
        w_router_t=w_router[0].T, b_router=b_router[0].reshape(-1, 1),
        w_gate_up=w_gate_up[0], b_gate_up=b_gate_up[0].reshape(N_EXPERTS, 1, 2 * D_FF),
        w_down=w_down[0], b_down=b_down[0].reshape(N_EXPERTS, 1, D_MODEL), g_final=row(g_final),
    )


def kernel(x_prompt, x_sample, cache_kv_latent, cache_k_rope, state_gla, cache_mem_k, cache_mem_v, page_table, mem_prompt, g_attn, w_in, g_q, g_kv, w_uq, w_uk, w_uv, w_a2, b_a, g_gla, w_o_mla, w_o_gla, w_out, g_mem, w_mk, w_mv, g_cross, w_mq, w_mo, g_ffn, w_router, b_router, w_gate_up, b_gate_up, w_down, b_down, g_final):
    w = _prepare_weights(g_attn, w_in, g_q, g_kv, w_uq, w_uk, w_uv, w_a2, b_a, g_gla, w_o_mla, w_o_gla, w_out,
                         g_mem, w_mk, w_mv, g_cross, w_mq, w_mo, g_ffn, w_router, b_router, w_gate_up, b_gate_up,
                         w_down, b_down, g_final)
    B, S, D = x_prompt.shape
    Bd = x_sample.shape[0]
    past_len = page_table.shape[1] * cache_kv_latent.shape[2]

    tm = min(256, S)
    xp = x_prompt.reshape(B * S, D)
    (qcat, ckv, krope, kcat, kcat_t, gq, gk, gv, sg, la, szm, szg) = _premix(
        xp, _rope_table(jnp.arange(S, dtype=F32)), w, min(PREMIX_ROWS, S), tm, BF16)
    olat = _mla_prompt(qcat.reshape(MLA_HEADS, B, S, QCAT), kcat.reshape(B, S, QCAT), kcat_t, tm)
    og, state_p = _gla_prompt(gq.reshape(B, S, -1), gk.reshape(B, S, -1), gv.reshape(B, S, -1), la.reshape(B, S, -1))
    h1, qx = _postmix(xp, olat.reshape(B * S, -1), og.reshape(B * S, -1), sg, szm, szg, w, tm)
    mem_k, mem_v = _memkv(mem_prompt.reshape(B * N_MEM, D), w["g_mem"], w["w_mkv"], min(512, B * N_MEM))
    tb = min(MOE_BLOCK, S)
    h2, xn2, idx, rank, wt, cnt = _cross_prompt(qx, h1, mem_k.reshape(B, N_MEM, MEM_WIDTH),
                                                mem_v.reshape(B, N_MEM, MEM_WIDTH), w, tb)
    y_prompt = _moe(xn2, idx, rank, wt, cnt.reshape(-1, N_EXPERTS), h2, w, tb, tb)

    xs = x_sample.reshape(Bd, D)
    (qcat_s, ckv_s, krope_s, kcat_s, _, gq_s, gk_s, gv_s, sg_s, la_s, szm_s, szg_s) = _premix(
        xs, _rope_table(jnp.full((Bd,), past_len, F32)), w, Bd, Bd, F32)
    olat_s = _mla_sample(page_table, qcat_s, kcat_s, cache_kv_latent[0], jnp.swapaxes(cache_k_rope[0], 1, 2))
    og_s, state_s = _gla_sample(gq_s, gk_s, gv_s, la_s, state_gla[0])
    h1_s, qx_s = _postmix(xs, olat_s.reshape(Bd, -1), og_s, sg_s, szm_s, szg_s, w, Bd)
    h2_s, xn2_s, idx_s, rank_s, wt_s, cnt_s = _cross_sample(
        qx_s, h1_s, cache_mem_k.reshape(Bd, N_MEM * MEM_HEADS, MEM_HEAD_DIM),
        cache_mem_v.reshape(Bd, N_MEM * MEM_HEADS, MEM_HEAD_DIM), w)
    y_sample = _moe(xn2_s, idx_s, rank_s, wt_s, cnt_s.reshape(-1, N_EXPERTS), h2_s, w, Bd, Bd)

    return (y_prompt.reshape(B, S, D), y_sample.reshape(Bd, 1, D),
            ckv.reshape(1, B, S, KV_LORA), krope.reshape(1, B, S, QK_ROPE), state_p[None],
            mem_k.reshape(1, B, N_MEM, MEM_HEADS, MEM_HEAD_DIM), mem_v.reshape(1, B, N_MEM, MEM_HEADS, MEM_HEAD_DIM),
            ckv_s.reshape(1, Bd, 1, KV_LORA), krope_s.reshape(1, Bd, 1, QK_ROPE), state_s[None])
```

```python
import functools

import jax
import jax.numpy as jnp
import numpy as np
from jax import lax
from jax.experimental import pallas as pl
from jax.experimental.pallas import tpu as pltpu

F32 = jnp.float32
BF16 = jnp.bfloat16

D_MODEL = 1024
MLA_HEADS = 8
Q_LORA = 256
KV_LORA = 128
QK_NOPE = 128
QK_ROPE = 64
ROPE_THETA = 10000.0
ATTN_SCALE = (QK_NOPE + QK_ROPE) ** -0.5
GLA_HEADS = 4
GLA_DK = 128
GLA_DV = 256
GLA_RANK = 16
GLA_TAU = 16.0
GLA_CHUNK = 64
GLA_GROUP = 4
N_MEM = 256
MEM_HEADS = 4
MEM_HEAD_DIM = 128
N_EXPERTS = 32
TOP_K = 4
D_FF = 1024
SWIGLU_LIMIT = 7.0
SWIGLU_ALPHA = 1.702
EPS = 1e-6

GLA_K_WIDTH = GLA_HEADS * GLA_DK
GLA_V_WIDTH = GLA_HEADS * GLA_DV
MEM_WIDTH = MEM_HEADS * MEM_HEAD_DIM
IN_SPLITS = (Q_LORA, KV_LORA, QK_ROPE, GLA_K_WIDTH, GLA_K_WIDTH, GLA_V_WIDTH, GLA_V_WIDTH, GLA_RANK, D_MODEL, D_MODEL)

LANES = 128
_PACKED_WIDTHS = (Q_LORA, KV_LORA, LANES, GLA_K_WIDTH, GLA_K_WIDTH, GLA_V_WIDTH, GLA_V_WIDTH, LANES, D_MODEL, D_MODEL)
_OFF = tuple(int(v) for v in np.cumsum((0,) + _PACKED_WIDTHS))
QCAT = 2 * LANES

VMEM_LIMIT = 56 << 20
PREMIX_ROWS = 512
POSTMIX_ROWS = 512


def _params(n_axes):
    return pltpu.CompilerParams(dimension_semantics=("arbitrary",) * n_axes, vmem_limit_bytes=VMEM_LIMIT)


def _rms(x, g):
    var = jnp.mean(x * x, axis=-1, keepdims=True)
    return x * lax.rsqrt(var + EPS) * g


def _dot(a, b):
    return jnp.dot(a.astype(BF16), b.astype(BF16), preferred_element_type=F32)


def _dot_nt(a, b):
    return lax.dot_general(a.astype(BF16), b.astype(BF16), (((1,), (1,)), ((), ())), preferred_element_type=F32)


def _dot_tn(a, b):
    return lax.dot_general(a.astype(BF16), b.astype(BF16), (((0,), (0,)), ((), ())), preferred_element_type=F32)


def _split3(x):
    hi = x.astype(BF16)
    r1 = x - hi.astype(F32)
    mid = r1.astype(BF16)
    lo = (r1 - mid.astype(F32)).astype(BF16)
    return hi, mid, lo


def _dot_f32_nt(a, b):
    a0, a1, a2 = _split3(a)
    b0, b1, b2 = _split3(b)
    d = lambda u, v: lax.dot_general(u, v, (((1,), (1,)), ((), ())), preferred_element_type=F32)
    return ((d(a1, b1) + d(a0, b2) + d(a2, b0)) + (d(a0, b1) + d(a1, b0))) + d(a0, b0)


def _dot_hi_nt(a, b):
    a0 = a.astype(BF16)
    a1 = (a - a0.astype(F32)).astype(BF16)
    b0 = b.astype(BF16)
    b1 = (b - b0.astype(F32)).astype(BF16)
    d = lambda u, v: lax.dot_general(u, v, (((1,), (1,)), ((), ())), preferred_element_type=F32)
    return (d(a0, b1) + d(a1, b0)) + d(a0, b0)


def _full(shape, buffers=None):
    n = len(shape)
    return pl.BlockSpec(shape, lambda *_: (0,) * n, pipeline_mode=pl.Buffered(buffers) if buffers else None)


def _premix_kernel(x_ref, tab_ref, g_attn_ref, w_in_ref, g_q_ref, g_kv_ref, w_uq_ref, w_a2_ref, b_a_ref,
                   qcat_ref, ckv_ref, krope_ref, kcat_ref, kcat_t_ref, gq_ref, gk_ref, gv_ref, sg_ref, la_ref, szm_ref, szg_ref):
    xn = _rms(x_ref[...], g_attn_ref[...]).astype(BF16)

    def proj(lo, hi):
        y = jnp.dot(xn, w_in_ref[:, _OFF[lo]:_OFF[hi]], preferred_element_type=F32)
        return [y[:, _OFF[i] - _OFF[lo]:_OFF[i + 1] - _OFF[lo]] for i in range(lo, hi)]

    tab = tab_ref[...]
    cos, sin_lo, sin_hi = tab[:, :LANES], tab[:, LANES:2 * LANES], tab[:, 2 * LANES:]

    def rope(t):
        return t * cos + pltpu.roll(t, LANES - QK_ROPE // 2, 1) * sin_lo + pltpu.roll(t, QK_ROPE // 2, 1) * sin_hi

    cq, ckv, kr = proj(0, 3)
    c_q = _rms(cq, g_q_ref[...]).astype(BF16)
    ckv = _rms(ckv, g_kv_ref[...])
    kr = rope(kr)
    ckv_ref[...] = ckv
    krope_ref[...] = kr[:, :QK_ROPE]
    kcat = jnp.concatenate([ckv, kr], axis=1)
    kcat_ref[...] = kcat.astype(BF16)
    tk = kcat_t_ref.shape[2]
    for t in range(kcat_t_ref.shape[0]):
        kcat_t_ref[t] = kcat[t * tk:(t + 1) * tk].T.astype(BF16)
    q_all = jnp.dot(c_q, w_uq_ref[...], preferred_element_type=F32)
    for h in range(MLA_HEADS):
        qcat_ref[h, :, :LANES] = q_all[:, h * LANES:(h + 1) * LANES].astype(qcat_ref.dtype)
        qcat_ref[h, :, LANES:] = rope(q_all[:, (MLA_HEADS + h) * LANES:(MLA_HEADS + h + 1) * LANES]).astype(qcat_ref.dtype)
    gq, gk = proj(3, 5)
    gq_ref[...] = (gq * GLA_DK ** -0.5).astype(gq_ref.dtype)
    gk_ref[...] = gk.astype(gk_ref.dtype)
    gv_ref[...] = proj(5, 6)[0].astype(gv_ref.dtype)
    gg, ga = proj(6, 8)
    sg_ref[...] = (gg * jax.nn.sigmoid(gg)).astype(sg_ref.dtype)
    z = _dot(ga, w_a2_ref[...]) + b_a_ref[...]
    la_ref[...] = -(jnp.maximum(-z, 0.0) + jnp.log1p(jnp.exp(-jnp.abs(z)))) * (1.0 / GLA_TAU)
    szm_ref[...] = jax.nn.sigmoid(proj(8, 9)[0]).astype(szm_ref.dtype)
    szg_ref[...] = jax.nn.sigmoid(proj(9, 10)[0]).astype(szg_ref.dtype)


def _fold_kernel(wq_ref, wk_ref, o_ref):
    o_ref[...] = _dot_f32_nt(wq_ref[...], wk_ref[...]).astype(o_ref.dtype)


def _fold_latent_query(w_uq_nope, w_uk):
    return pl.pallas_call(
        _fold_kernel,
        grid=(MLA_HEADS,),
        in_specs=[pl.BlockSpec((None, Q_LORA, QK_NOPE), lambda h: (h, 0, 0)),
                  pl.BlockSpec((None, KV_LORA, QK_NOPE), lambda h: (h, 0, 0))],
        out_specs=pl.BlockSpec((Q_LORA, KV_LORA), lambda h: (0, h)),
        out_shape=jax.ShapeDtypeStruct((Q_LORA, MLA_HEADS * KV_LORA), BF16),
        compiler_params=_params(1),
        name="fold_latent_query",
    )(w_uq_nope, w_uk)


def _premix(x, tab, w, tm, tk, gdtype):
    T = x.shape[0]
    n_tab = tab.shape[0] // tm
    row = lambda n: pl.BlockSpec((tm, n), lambda i: (i, 0))
    outs = [(KV_LORA, F32), (QK_ROPE, F32), (QCAT, BF16), None, (GLA_K_WIDTH, gdtype),
            (GLA_K_WIDTH, gdtype), (GLA_V_WIDTH, gdtype), (GLA_V_WIDTH, BF16), (GLA_K_WIDTH, F32),
            (D_MODEL, BF16), (D_MODEL, BF16)]
    out_specs = [row(o[0]) if o else pl.BlockSpec((tm // tk, QCAT, tk), lambda i: (i, 0, 0)) for o in outs]
    out_shape = [jax.ShapeDtypeStruct((T, o[0]) if o else (T // tk, QCAT, tk), o[1] if o else BF16) for o in outs]
    consts = [w["g_attn"], w["w_in"], w["g_q"], w["g_kv"], w["w_uq"], w["w_a2"], w["b_a"]]
    const_specs = [pl.BlockSpec(c.shape, lambda i, n=len(c.shape): (0,) * n,
                                pipeline_mode=pl.Buffered(1) if c is w["w_in"] else None) for c in consts]
    return pl.pallas_call(
        _premix_kernel,
        grid=(T // tm,),
        in_specs=[row(D_MODEL), pl.BlockSpec((tm, 3 * LANES), lambda i: (i % n_tab, 0))] + const_specs,
        out_specs=[pl.BlockSpec((MLA_HEADS, tm, QCAT), lambda i: (0, i, 0))] + out_specs,
        out_shape=[jax.ShapeDtypeStruct((MLA_HEADS, T, QCAT), BF16)] + out_shape,
        compiler_params=_params(1),
        name="premix",
    )(x, tab, *consts)


def _mla_prompt_kernel(q_ref, k_ref, kt_ref, o_ref, m_ref, l_ref, acc_ref):
    tq = q_ref.shape[1]
    qi = pl.program_id(1)
    q = q_ref[...].reshape(MLA_HEADS * tq, QCAT)
    row = lax.broadcasted_iota(jnp.int32, (MLA_HEADS, tq, tq), 1).reshape(MLA_HEADS * tq, tq)
    col = lax.broadcasted_iota(jnp.int32, (MLA_HEADS * tq, tq), 1)
    m_ref[...] = jnp.full_like(m_ref, -jnp.inf)
    l_ref[...] = jnp.zeros_like(l_ref)
    acc_ref[...] = jnp.zeros_like(acc_ref)
    c = ATTN_SCALE * np.log2(np.e)

    def step(j, n, mask_last):
        keys = pl.ds(pl.multiple_of(j * tq, tq), n * tq)
        parts = [jnp.dot(q, kt_ref[j + t], preferred_element_type=F32) for t in range(n)]
        if mask_last:
            parts[-1] = jnp.where(col <= row, parts[-1], -jnp.inf)
        s = jnp.concatenate(parts, axis=1)
        m = m_ref[...]
        m_new = jnp.maximum(m, jnp.max(s, axis=-1, keepdims=True))
        alpha = jnp.exp2((m - m_new) * c)
        p = jnp.exp2((s - jnp.concatenate([m_new] * (n * tq // LANES), axis=1)) * c)
        p_lanes = p[:, :LANES]
        for t in range(1, n * tq // LANES):
            p_lanes = p_lanes + p[:, t * LANES:(t + 1) * LANES]
        l_ref[...] = alpha * l_ref[...] + p_lanes
        acc_ref[...] = alpha * acc_ref[...] + _dot(p, k_ref[keys, :KV_LORA])
        m_ref[...] = m_new

    def body(jj, carry):
        step(2 * jj, 2, False)
        return carry

    lax.fori_loop(0, qi // 2, body, 0)

    @pl.when(qi % 2 == 0)
    def _():
        step(qi, 1, True)

    @pl.when(qi % 2 == 1)
    def _():
        step(qi - 1, 2, True)
    o = acc_ref[...] / jnp.sum(l_ref[...], axis=-1, keepdims=True)
    for h in range(MLA_HEADS):
        o_ref[:, h * KV_LORA:(h + 1) * KV_LORA] = o[h * tq:(h + 1) * tq].astype(o_ref.dtype)


def _mla_prompt(qcat, kcat, kcat_t, tq):
    _, B, S, _ = qcat.shape
    rows = MLA_HEADS * tq
    return pl.pallas_call(
        _mla_prompt_kernel,
        grid=(B, S // tq),
        in_specs=[pl.BlockSpec((MLA_HEADS, None, tq, QCAT), lambda b, i: (0, b, i, 0)),
                  pl.BlockSpec((None, S, QCAT), lambda b, i: (b, 0, 0)),
                  pl.BlockSpec((S // tq, QCAT, tq), lambda b, i: (b, 0, 0))],
        out_specs=pl.BlockSpec((None, tq, MLA_HEADS * KV_LORA), lambda b, i: (b, i, 0)),
        out_shape=jax.ShapeDtypeStruct((B, S, MLA_HEADS * KV_LORA), BF16),
        scratch_shapes=[pltpu.VMEM((rows, LANES), F32), pltpu.VMEM((rows, LANES), F32), pltpu.VMEM((rows, KV_LORA), F32)],
        compiler_params=_params(2),
        name="mla_prompt",
    )(qcat, kcat, kcat_t)


def _mla_sample_kernel(pt_ref, q_ref, knew_ref, lat_hbm, rope_hbm, o_ref, lat_buf, rope_buf, lat_bf, rope_bf, sems):
    b = pl.program_id(0)
    n_pages, page = lat_buf.shape[1], lat_buf.shape[2]

    def page_copies(sample, slot, i):
        pg = pt_ref[sample, i]
        return (pltpu.make_async_copy(lat_hbm.at[pg], lat_buf.at[slot, i], sems.at[0, slot]),
                pltpu.make_async_copy(rope_hbm.at[pg], rope_buf.at[slot, i], sems.at[1, slot]))

    def start_pages(sample, slot):
        def body(i, carry):
            for cp in page_copies(sample, slot, i):
                cp.start()
            return carry
        lax.fori_loop(0, n_pages, body, 0)

    slot = b % 2

    @pl.when(b == 0)
    def _():
        start_pages(0, 0)

    @pl.when(b + 1 < pl.num_programs(0))
    def _():
        start_pages(b + 1, 1 - slot)

    pltpu.make_async_copy(lat_hbm.at[pl.ds(0, n_pages)], lat_buf.at[slot], sems.at[0, slot]).wait()
    pltpu.make_async_copy(rope_hbm.at[pl.ds(0, n_pages)], rope_buf.at[slot], sems.at[1, slot]).wait()

    lat_bf[...] = lat_buf[slot].reshape(n_pages * page, KV_LORA).astype(BF16)
    for i in range(n_pages):
        rope_bf[:, i * page:(i + 1) * page] = rope_buf[slot, i].astype(BF16)
    q = q_ref[...].reshape(MLA_HEADS, QCAT)
    knew = knew_ref[...].astype(F32)
    s = (_dot_nt(q[:, :KV_LORA], lat_bf[...]) + _dot(q[:, KV_LORA:KV_LORA + QK_ROPE], rope_bf[...])) * ATTN_SCALE
    s_new = jnp.sum(q.astype(F32) * knew, axis=-1, keepdims=True) * ATTN_SCALE
    m = jnp.maximum(jnp.max(s, axis=-1, keepdims=True), s_new)
    p = jnp.exp(s - m)
    p_new = jnp.exp(s_new - m)
    denom = jnp.sum(p, axis=-1, keepdims=True) + p_new
    o = _dot(p, lat_bf[...]) + p_new * knew[:, :KV_LORA]
    o_ref[...] = (o / denom).astype(o_ref.dtype)


def _mla_sample(page_table, qcat, knew, cache_lat, cache_rope_t):
    Bd, n_pages = page_table.shape
    page = cache_lat.shape[1]
    grid_spec = pltpu.PrefetchScalarGridSpec(
        num_scalar_prefetch=1,
        grid=(Bd,),
        in_specs=[pl.BlockSpec((MLA_HEADS, None, 1, QCAT), lambda b, pt: (0, b, 0, 0)),
                  pl.BlockSpec((None, 1, QCAT), lambda b, pt: (b, 0, 0)),
                  pl.BlockSpec(memory_space=pl.ANY), pl.BlockSpec(memory_space=pl.ANY)],
        out_specs=pl.BlockSpec((None, MLA_HEADS, KV_LORA), lambda b, pt: (b, 0, 0)),
        scratch_shapes=[pltpu.VMEM((2, n_pages, page, KV_LORA), F32), pltpu.VMEM((2, n_pages, QK_ROPE, page), F32),
                        pltpu.VMEM((n_pages * page, KV_LORA), BF16), pltpu.VMEM((QK_ROPE, n_pages * page), BF16),
                        pltpu.SemaphoreType.DMA((2, 2))],
    )
    return pl.pallas_call(
        _mla_sample_kernel,
        grid_spec=grid_spec,
        out_shape=jax.ShapeDtypeStruct((Bd, MLA_HEADS, KV_LORA), BF16),
        compiler_params=_params(1),
        name="mla_sample",
    )(page_table, qcat.reshape(MLA_HEADS, Bd, 1, QCAT), knew.reshape(Bd, 1, QCAT), cache_lat, cache_rope_t)


def _gla_prompt_kernel(q_ref, k_ref, v_ref, la_ref, o_ref, state_ref, st_ref):
    C = GLA_CHUNK
    S = q_ref.shape[0]
    row = lax.broadcasted_iota(jnp.int32, (C, 2 * C), 0)
    col = lax.broadcasted_iota(jnp.int32, (C, 2 * C), 1)
    causal = row >= col
    tri = causal[:, :C].astype(BF16)
    pad_k = jnp.zeros((C, GLA_DK), F32)
    pad_v = jnp.zeros((C, GLA_DV), F32)
    st_ref[...] = jnp.zeros_like(st_ref)

    group = min(GLA_GROUP, S // C)

    def chunks(c, carry):
        parts = []
        for g in range(group):
            rows = pl.ds(pl.multiple_of((c * group + g) * C, C), C)
            split = jnp.dot(tri, jnp.concatenate(_split3(la_ref[rows, :]), axis=1), preferred_element_type=F32)
            b_all = (split[:, 2 * GLA_K_WIDTH:] + split[:, GLA_K_WIDTH:2 * GLA_K_WIDTH]) + split[:, :GLA_K_WIDTH]
            for h in range(GLA_HEADS):
                kcols = slice(h * GLA_DK, (h + 1) * GLA_DK)
                vcols = slice(h * GLA_DV, (h + 1) * GLA_DV)
                b = b_all[:, kcols]
                b_last = b[C - 1:C, :]
                q = q_ref[rows, kcols].astype(F32)
                k = k_ref[rows, kcols].astype(F32)
                q_in = (q * jnp.exp(b)).astype(BF16)
                k_in = jnp.concatenate([k * jnp.exp(-b), pad_k], axis=0)
                k_out = jnp.concatenate([k * jnp.exp(b_last - b), pad_k], axis=0)
                a = jnp.where(causal, _dot_nt(q_in, k_in), 0.0)
                v_t = jnp.concatenate([v_ref[rows, vcols].astype(F32), pad_v], axis=0).T.astype(BF16)
                lhs = jnp.concatenate([q_in, a.astype(BF16)], axis=1)
                parts.append((rows, h, vcols, lhs, v_t, jnp.exp(b_last), _dot(v_t, k_out)))
        for rows, h, vcols, lhs, v_t, decay, kv_t in parts:
            st = st_ref[h]
            o_ref[rows, vcols] = _dot_nt(lhs, jnp.concatenate([st.astype(BF16), v_t], axis=1)).astype(o_ref.dtype)
            st_ref[h] = st * decay + kv_t
        return carry

    lax.fori_loop(0, S // C // group, chunks, 0)
    for h in range(GLA_HEADS):
        state_ref[h] = st_ref[h].T


def _gla_prompt(q, k, v, la):
    B, S, _ = q.shape
    seq = lambda n: pl.BlockSpec((None, S, n), lambda b: (b, 0, 0))
    return pl.pallas_call(
        _gla_prompt_kernel,
        grid=(B,),
        in_specs=[seq(GLA_K_WIDTH), seq(GLA_K_WIDTH), seq(GLA_V_WIDTH), seq(GLA_K_WIDTH)],
        out_specs=[seq(GLA_V_WIDTH), pl.BlockSpec((None, GLA_HEADS, GLA_DK, GLA_DV), lambda b: (b, 0, 0, 0))],
        out_shape=[jax.ShapeDtypeStruct((B, S, GLA_V_WIDTH), BF16),
                   jax.ShapeDtypeStruct((B, GLA_HEADS, GLA_DK, GLA_DV), F32)],
        scratch_shapes=[pltpu.VMEM((GLA_HEADS, GLA_DV, GLA_DK), F32)],
        compiler_params=_params(1),
        name="gla_prompt",
    )(q, k, v, la)


def _gla_sample_kernel(q_ref, k_ref, v_ref, la_ref, st_ref, o_ref, sto_ref):
    nb = q_ref.shape[0]
    pad = jnp.zeros((GLA_DK - nb, GLA_DK), F32)
    for h in range(GLA_HEADS):
        kcols = slice(h * GLA_DK, (h + 1) * GLA_DK)
        vcols = slice(h * GLA_DV, (h + 1) * GLA_DV)
        col = lambda ref: jnp.concatenate([ref[:, kcols].astype(F32), pad], axis=0).T
        q_t, k_t, decay_t = col(q_ref), col(k_ref), jnp.exp(col(la_ref))
        for i in range(nb):
            new = decay_t[:, i:i + 1] * st_ref[i, h] + k_t[:, i:i + 1] * v_ref[i:i + 1, vcols].astype(F32)
            sto_ref[i, h] = new
            o_ref[i:i + 1, vcols] = jnp.sum(q_t[:, i:i + 1] * new, axis=0, keepdims=True).astype(o_ref.dtype)


def _gla_sample(q, k, v, la, state, nb=8):
    Bd = q.shape[0]
    rows = lambda n: pl.BlockSpec((nb, n), lambda i: (i, 0))
    st_spec = pl.BlockSpec((nb, GLA_HEADS, GLA_DK, GLA_DV), lambda i: (i, 0, 0, 0))
    return pl.pallas_call(
        _gla_sample_kernel,
        grid=(Bd // nb,),
        in_specs=[rows(GLA_K_WIDTH), rows(GLA_K_WIDTH), rows(GLA_V_WIDTH), rows(GLA_K_WIDTH), st_spec],
        out_specs=[rows(GLA_V_WIDTH), st_spec],
        out_shape=[jax.ShapeDtypeStruct((Bd, GLA_V_WIDTH), BF16), jax.ShapeDtypeStruct(state.shape, F32)],
        compiler_params=_params(1),
        name="gla_sample",
    )(q, k, v, la, state)


def _postmix_kernel(x_ref, olat_ref, og_ref, sg_ref, szm_ref, szg_ref, wuv_ref, w_o_mla_ref, g_gla_ref, w_o_gla_ref,
                    w_out_ref, g_cross_ref, w_mq_ref, h_ref, qx_ref):
    ov = jnp.concatenate(
        [_dot(olat_ref[:, h * KV_LORA:(h + 1) * KV_LORA], wuv_ref[h]).astype(BF16) for h in range(MLA_HEADS)], axis=1)
    o_mla = jnp.dot(ov, w_o_mla_ref[...], preferred_element_type=F32)
    g_gla = g_gla_ref[...]
    og = jnp.concatenate(
        [(_rms(og_ref[:, h * GLA_DV:(h + 1) * GLA_DV].astype(F32), g_gla)
          * sg_ref[:, h * GLA_DV:(h + 1) * GLA_DV].astype(F32)).astype(BF16) for h in range(GLA_HEADS)], axis=1)
    o_gla = jnp.dot(og, w_o_gla_ref[...], preferred_element_type=F32)
    merged = szm_ref[...].astype(F32) * o_mla + szg_ref[...].astype(F32) * o_gla
    h1 = x_ref[...] + _dot(merged, w_out_ref[...])
    h_ref[...] = h1
    qx_ref[...] = _dot(_rms(h1, g_cross_ref[...]), w_mq_ref[...]).astype(qx_ref.dtype)


def _postmix(x, olat, og, sg, szm, szg, w, tm):
    T = x.shape[0]
    row = lambda n: pl.BlockSpec((tm, n), lambda i: (i, 0))
    consts = [w["wuv"], w["w_o_mla"], w["g_gla"], w["w_o_gla"], w["w_out"], w["g_cross"], w["w_mq"]]
    return pl.pallas_call(
        _postmix_kernel,
        grid=(T // tm,),
        in_specs=[row(D_MODEL)] * 6 + [_full(c.shape, buffers=1) for c in consts],
        out_specs=[row(D_MODEL), row(MEM_WIDTH)],
        out_shape=[jax.ShapeDtypeStruct((T, D_MODEL), F32), jax.ShapeDtypeStruct((T, MEM_WIDTH), BF16)],
        compiler_params=_params(1),
        name="postmix",
    )(x, olat, og, sg, szm, szg, *consts)


def _memkv_kernel(mem_ref, g_ref, w_ref, k_ref, v_ref):
    kv = _dot(_rms(mem_ref[...], g_ref[...]), w_ref[...])
    k_ref[...] = kv[:, :MEM_WIDTH]
    v_ref[...] = kv[:, MEM_WIDTH:]


def _memkv(mem, g_mem, w_mkv, tm=512):
    T = mem.shape[0]
    row = lambda n: pl.BlockSpec((tm, n), lambda i: (i, 0))
    return pl.pallas_call(
        _memkv_kernel,
        grid=(T // tm,),
        in_specs=[row(D_MODEL), _full(g_mem.shape), _full(w_mkv.shape)],
        out_specs=[row(MEM_WIDTH), row(MEM_WIDTH)],
        out_shape=[jax.ShapeDtypeStruct((T, MEM_WIDTH), F32)] * 2,
        compiler_params=_params(1),
        name="memkv",
    )(mem, g_mem, w_mkv)


def _attend_many(problems):
    scores = [_dot_nt(q, k) * MEM_HEAD_DIM ** -0.5 for q, k, _ in problems]
    probs = []
    for s in scores:
        p = jnp.exp(s - jnp.max(s, axis=-1, keepdims=True))
        probs.append(p / jnp.sum(p, axis=-1, keepdims=True))
    return [_dot(p, v) for p, (_, _, v) in zip(probs, problems)]


def _head_cols(h):
    return slice(h * MEM_HEAD_DIM, (h + 1) * MEM_HEAD_DIM)


def _route(o, h1, w_mo_ref, g_ffn_ref, w_router_ref, b_router_ref, h_ref, xn_ref, idx_ref, rank_ref, wt_ref, cnt_ref):
    h2 = h1 + jnp.dot(o, w_mo_ref[...], preferred_element_type=F32)
    h_ref[...] = h2
    xn = _rms(h2, g_ffn_ref[...])
    xn_ref[...] = xn.astype(xn_ref.dtype)
    tb = xn.shape[0]
    logits = _dot_hi_nt(w_router_ref[...], xn) + b_router_ref[...]
    expert = lax.broadcasted_iota(jnp.int32, logits.shape, 0)
    work = logits
    hits, firsts, exps = [], [], []
    top = None
    for _ in range(TOP_K):
        best = jnp.max(work, axis=0, keepdims=True)
        first = jnp.min(jnp.where(work == best, expert, N_EXPERTS), axis=0, keepdims=True)
        hit = expert == first
        top = best if top is None else top
        hits.append(hit)
        firsts.append(first)
        exps.append(jnp.exp(best - top))
        work = jnp.where(hit, -jnp.inf, work)
    denom = (exps[0] + exps[1]) + (exps[2] + exps[3])
    chosen = jnp.zeros(logits.shape, F32)
    for hit in hits:
        chosen = chosen + jnp.where(hit, 1.0, 0.0)
    before = (lax.broadcasted_iota(jnp.int32, (tb, tb), 0) < lax.broadcasted_iota(jnp.int32, (tb, tb), 1)).astype(BF16)
    rank = jnp.dot(chosen.astype(BF16), before, preferred_element_type=F32)
    idx_ref[...] = jnp.concatenate(firsts, axis=0)
    rank_ref[...] = jnp.concatenate(
        [jnp.sum(jnp.where(hit, rank, 0.0), axis=0, keepdims=True) for hit in hits], axis=0).astype(jnp.int32)
    wt_ref[...] = jnp.concatenate([e / denom for e in exps], axis=0)
    cnt_ref[...] = jnp.sum(chosen, axis=1, keepdims=True).astype(jnp.int32)


def _cross_prompt_kernel(qx_ref, h1_ref, mk_ref, mv_ref, *rest):
    outs = _attend_many([(qx_ref[:, _head_cols(h)], mk_ref[:, _head_cols(h)], mv_ref[:, _head_cols(h)])
                         for h in range(MEM_HEADS)])
    _route(jnp.concatenate([o.astype(BF16) for o in outs], axis=1), h1_ref[...], *rest)


def _cross_sample_kernel(qx_ref, mk_ref, mv_ref, o_ref):
    rows = mk_ref.shape[1]
    pad = jnp.zeros((8 - MEM_HEADS, MEM_HEAD_DIM), qx_ref.dtype)
    own_head = (lax.broadcasted_iota(jnp.int32, (8, rows), 1) % MEM_HEADS) == lax.broadcasted_iota(jnp.int32, (8, rows), 0)
    samples = range(qx_ref.shape[0])
    scores = [_dot_nt(jnp.concatenate([qx_ref[i:i + 1, _head_cols(h)] for h in range(MEM_HEADS)] + [pad], axis=0),
                      mk_ref[i]) * MEM_HEAD_DIM ** -0.5 for i in samples]
    probs = []
    for s in scores:
        s = jnp.where(own_head, s, -jnp.inf)
        p = jnp.exp(s - jnp.max(s, axis=-1, keepdims=True))
        probs.append(p / jnp.sum(p, axis=-1, keepdims=True))
    for i, p in zip(samples, probs):
        o = _dot(p, mv_ref[i])
        o_ref[i:i + 1, :] = jnp.concatenate([o[h:h + 1] for h in range(MEM_HEADS)], axis=1).astype(o_ref.dtype)


def _route_kernel(o_ref, h1_ref, *rest):
    _route(o_ref[...], h1_ref[...], *rest)


def _route_specs(T, tb, w):
    consts = [w["w_mo"], w["g_ffn"], w["w_router_t"], w["b_router"]]
    row = lambda n: pl.BlockSpec((tb, n), lambda i: (i, 0))
    per_choice = pl.BlockSpec((TOP_K, tb), lambda i: (0, i))
    out_specs = [row(D_MODEL), row(D_MODEL), per_choice, per_choice, per_choice,
                 pl.BlockSpec((None, N_EXPERTS, 1), lambda i: (i, 0, 0))]
    out_shape = [jax.ShapeDtypeStruct((T, D_MODEL), F32), jax.ShapeDtypeStruct((T, D_MODEL), BF16),
                 jax.ShapeDtypeStruct((TOP_K, T), jnp.int32), jax.ShapeDtypeStruct((TOP_K, T), jnp.int32),
                 jax.ShapeDtypeStruct((TOP_K, T), F32), jax.ShapeDtypeStruct((T // tb, N_EXPERTS, 1), jnp.int32)]
    return consts, out_specs, out_shape


def _cross_prompt(qx, h1, mem_k, mem_v, w, tb):
    T = qx.shape[0]
    blocks_per_mem = T // mem_k.shape[0] // tb
    consts, out_specs, out_shape = _route_specs(T, tb, w)
    row = lambda n: pl.BlockSpec((tb, n), lambda i: (i, 0))
    mem_spec = pl.BlockSpec((None, N_MEM, MEM_WIDTH), lambda i: (i // blocks_per_mem, 0, 0))
    return pl.pallas_call(
        _cross_prompt_kernel,
        grid=(T // tb,),
        in_specs=[row(MEM_WIDTH), row(D_MODEL), mem_spec, mem_spec] + [_full(c.shape) for c in consts],
        out_specs=out_specs,
        out_shape=out_shape,
        compiler_params=_params(1),
        name="cross_prompt",
    )(qx, h1, mem_k, mem_v, *consts)


def _cross_sample(qx, h1, mem_k, mem_v, w, nb=8):
    T = qx.shape[0]
    mem_spec = pl.BlockSpec((nb, N_MEM * MEM_HEADS, MEM_HEAD_DIM), lambda i: (i, 0, 0))
    rows = pl.BlockSpec((nb, MEM_WIDTH), lambda i: (i, 0))
    o = pl.pallas_call(
        _cross_sample_kernel,
        grid=(T // nb,),
        in_specs=[rows, mem_spec, mem_spec],
        out_specs=rows,
        out_shape=jax.ShapeDtypeStruct((T, MEM_WIDTH), BF16),
        compiler_params=_params(1),
        name="cross_sample",
    )(qx, mem_k, mem_v)
    consts, out_specs, out_shape = _route_specs(T, T, w)
    return pl.pallas_call(
        _route_kernel,
        grid=(1,),
        in_specs=[_full(o.shape), _full(h1.shape)] + [_full(c.shape) for c in consts],
        out_specs=out_specs,
        out_shape=out_shape,
        compiler_params=_params(1),
        name="route_sample",
    )(o, h1, *consts)


SEG_ALIGN = 16
SEL_CHUNK = 256
MOE_BLOCK = 512


def _block_rows(tb):
    worst = TOP_K * tb + N_EXPERTS * (SEG_ALIGN - 1)
    return -(-worst // SEL_CHUNK) * SEL_CHUNK


def _seg_sizes(limit):
    sizes, b = [], SEG_ALIGN
    while b <= limit:
        sizes.append(b)
        b *= 2
    return sizes[::-1]


def _for_segments(n, sizes, make_copy, act):
    def emit(group):
        for size in group:
            @pl.when((n & size) != 0)
            def _():
                act(make_copy(pl.multiple_of(n & (-2 * size), SEG_ALIGN), size))

    large = [size for size in sizes if size >= 8 * SEG_ALIGN]
    if len(large) > 1:
        pl.when(n >= large[-1])(lambda: emit(large))
    else:
        emit(large)
    emit([size for size in sizes if size < 8 * SEG_ALIGN])


def _slot_rows(idx, rank, loc_ref, j):
    pos = rank
    for e in range(N_EXPERTS):
        pos = pos + jnp.where(idx == e, loc_ref[j, e], 0)
    return pos


def _dispatch_kernel(loc_ref, pad_ref, goff_ref, total_ref, gap_off_ref, gap_len_ref, x_ref, x_last_ref, idx_ref, rank_ref,
                     xs_hbm, buf, sem, *, tm):
    j = pl.program_id(0)
    last = pl.num_programs(0) - 1
    slot = j % 2
    tb = x_ref.shape[0]
    r_blk = _block_rows(tb)
    pos = _slot_rows(idx_ref[...], rank_ref[...], loc_ref, j)
    x = jnp.where(j == last, x_last_ref[...], x_ref[...])
    for c in range(r_blk // SEL_CHUNK):
        r = lax.broadcasted_iota(jnp.int32, (SEL_CHUNK, tb), 0) + c * SEL_CHUNK
        sel = jnp.where(r == pos[0:1], 1.0, jnp.where(r == pos[1:2], 1.0, jnp.where(
            r == pos[2:3], 1.0, jnp.where(r == pos[3:4], 1.0, 0.0))))
        buf[slot, c * SEL_CHUNK:(c + 1) * SEL_CHUNK, :] = jnp.dot(
            sel.astype(BF16), x, preferred_element_type=F32).astype(BF16)

    sizes = _seg_sizes(tb)

    def start_segments(e, carry):
        src, dst = loc_ref[j, e], goff_ref[j, e]
        _for_segments(pad_ref[j, e], sizes, lambda off, size: pltpu.make_async_copy(
            buf.at[slot, pl.ds(pl.multiple_of(src + off, SEG_ALIGN), size)],
            xs_hbm.at[pl.ds(pl.multiple_of(dst + off, SEG_ALIGN), size)], sem.at[slot]), lambda cp: cp.start())
        return carry

    lax.fori_loop(0, N_EXPERTS, start_segments, 0)

    def wait_block(blk, slot):
        _for_segments(total_ref[blk], _seg_sizes(r_blk), lambda off, size: pltpu.make_async_copy(
            buf.at[slot, pl.ds(0, size)], xs_hbm.at[pl.ds(0, size)], sem.at[slot]), lambda cp: cp.wait())

    @pl.when(j > 0)
    def _():
        wait_block(j - 1, 1 - slot)

    @pl.when(j == last)
    def _():
        wait_block(j, slot)
        buf[1 - slot, 0:tm, :] = jnp.zeros((tm, D_MODEL), BF16)
        gap_sizes = _seg_sizes(tm - SEG_ALIGN)

        def for_gaps(act):
            def body(e, carry):
                dst = gap_off_ref[e]
                _for_segments(gap_len_ref[e], gap_sizes, lambda off, size: pltpu.make_async_copy(
                    buf.at[1 - slot, pl.ds(0, size)], xs_hbm.at[pl.ds(pl.multiple_of(dst + off, SEG_ALIGN), size)],
                    sem.at[1 - slot]), act)
                return carry
            lax.fori_loop(0, N_EXPERTS, body, 0)

        for_gaps(lambda cp: cp.start())
        for_gaps(lambda cp: cp.wait())


def _ffn_kernel(tile_expert_ref, n_valid_ref, x_ref, wgu_f32_ref, bgu_ref, wd_f32_ref, bd_ref, o_ref, wgu_ref, wd_ref):
    t = pl.program_id(0)

    @pl.when(t < n_valid_ref[0])
    def _():
        @pl.when((t == 0) | (tile_expert_ref[t] != tile_expert_ref[jnp.maximum(t - 1, 0)]))
        def _():
            wgu_ref[...] = wgu_f32_ref[...].astype(BF16)
            wd_ref[...] = wd_f32_ref[...].astype(BF16)

        x = x_ref[...]
        bgu = bgu_ref[...]
        gate = jnp.dot(x, wgu_ref[:, :D_FF], preferred_element_type=F32) + bgu[:, :D_FF]
        up = jnp.dot(x, wgu_ref[:, D_FF:], preferred_element_type=F32) + bgu[:, D_FF:]
        gate = jnp.minimum(gate, SWIGLU_LIMIT)
        up = jnp.clip(up, -SWIGLU_LIMIT, SWIGLU_LIMIT)
        hidden = (up + 1.0) * gate * jax.nn.sigmoid(SWIGLU_ALPHA * gate)
        o_ref[...] = (_dot(hidden, wd_ref[...]) + bd_ref[...]).astype(o_ref.dtype)


def _combine_kernel(loc_ref, pad_ref, goff_ref, total_ref, idx0_ref, rank0_ref, wt0_ref, idx_ref, rank_ref, wt_ref,
                    h_ref, h_last_ref, g_final_ref, o_hbm, y_ref, y_last_ref, buf, sel_ref, sem):
    j = pl.program_id(0)
    last = pl.num_programs(0) - 1
    slot = j % 2
    tb = h_ref.shape[0]
    r_blk = buf.shape[1]
    sizes = _seg_sizes(tb)

    def fetch(blk, slot):
        def body(e, carry):
            dst, src = loc_ref[blk, e], goff_ref[blk, e]
            _for_segments(pad_ref[blk, e], sizes, lambda off, size: pltpu.make_async_copy(
                o_hbm.at[pl.ds(pl.multiple_of(src + off, SEG_ALIGN), size)],
                buf.at[slot, pl.ds(pl.multiple_of(dst + off, SEG_ALIGN), size)], sem.at[slot]), lambda cp: cp.start())
            return carry
        lax.fori_loop(0, N_EXPERTS, body, 0)

    def build(blk, slot, idx, rank, wt):
        pos = _slot_rows(idx, rank, loc_ref, blk)
        for c in range(r_blk // SEL_CHUNK):
            r = lax.broadcasted_iota(jnp.int32, (SEL_CHUNK, tb), 0) + c * SEL_CHUNK
            sel = jnp.zeros((SEL_CHUNK, tb), F32)
            for k in range(TOP_K):
                sel = jnp.where(r == pos[k:k + 1], wt[k:k + 1], sel)
            sel_ref[slot, c * SEL_CHUNK:(c + 1) * SEL_CHUNK, :] = sel.astype(BF16)

    @pl.when(j == 0)
    def _():
        buf[...] = jnp.zeros_like(buf)
        fetch(0, 0)
        build(0, 0, idx0_ref[...], rank0_ref[...], wt0_ref[...])

    @pl.when(j < last)
    def _():
        fetch(j + 1, 1 - slot)

    _for_segments(total_ref[j], _seg_sizes(r_blk), lambda off, size: pltpu.make_async_copy(
        o_hbm.at[pl.ds(0, size)], buf.at[slot, pl.ds(0, size)], sem.at[slot]), lambda cp: cp.wait())
    moe = _dot_tn(sel_ref[slot], buf[slot])
    y = _rms(jnp.where(j == last, h_last_ref[...], h_ref[...]) + moe, g_final_ref[...])
    build(jnp.minimum(j + 1, last), 1 - slot, idx_ref[...], rank_ref[...], wt_ref[...])

    @pl.when(j < last)
    def _():
        y_ref[...] = y

    @pl.when(j == last)
    def _():
        y_last_ref[...] = y


def _moe(xn, xn_last, idx, rank, wt, cnt, h2, h2_last, w, tb, tm):
    T = xn.shape[0]
    nb = T // tb + 1
    r_blk = _block_rows(tb)
    i32 = jnp.int32
    pad = (cnt + (SEG_ALIGN - 1)) // SEG_ALIGN * SEG_ALIGN
    loc = jnp.cumsum(pad, axis=1) - pad
    seg = jnp.sum(pad, axis=0)
    region = (seg + (tm - 1)) // tm * tm
    region_end = jnp.cumsum(region)
    region_start = region_end - region
    goff = region_start[None, :] + jnp.cumsum(pad, axis=0) - pad
    n_tiles = -(-(nb * (TOP_K * tb + N_EXPERTS * (SEG_ALIGN - 1)) + N_EXPERTS * (tm - 1)) // tm)
    n_valid = (region_end[-1] // tm).astype(i32).reshape(1)
    tile = jnp.minimum(jnp.arange(n_tiles, dtype=i32), n_valid - 1)
    tile_expert = jnp.minimum(jnp.sum((region_end[None, :] <= (tile * tm)[:, None]).astype(i32), axis=1), N_EXPERTS - 1)
    plan = [a.astype(i32) for a in (loc, pad, goff, jnp.sum(pad, axis=1))]
    gaps = [(region_start + seg).astype(i32), (region - seg).astype(i32)]

    choice = pl.BlockSpec((TOP_K, tb), lambda j, *_: (0, j))
    body_rows = pl.BlockSpec((tb, D_MODEL), lambda j, *_: (jnp.minimum(j, nb - 2), 0))
    last_rows = pl.BlockSpec((tb, D_MODEL), lambda j, *_: (0, 0))
    xs = pl.pallas_call(
        functools.partial(_dispatch_kernel, tm=tm),
        grid_spec=pltpu.PrefetchScalarGridSpec(
            num_scalar_prefetch=6, grid=(nb,),
            in_specs=[body_rows, last_rows, choice, choice],
            out_specs=pl.BlockSpec(memory_space=pl.ANY),
            scratch_shapes=[pltpu.VMEM((2, r_blk, D_MODEL), BF16), pltpu.SemaphoreType.DMA((2,))]),
        out_shape=jax.ShapeDtypeStruct((n_tiles * tm, D_MODEL), BF16),
        compiler_params=_params(1),
        name="moe_dispatch",
    )(*plan, *gaps, xn, xn_last, idx, rank)

    rows = pl.BlockSpec((tm, D_MODEL), lambda t, te, nv: (jnp.minimum(t, nv[0] - 1), 0))
    per_e = lambda a, b: pl.BlockSpec((None, a, b), lambda t, te, nv: (te[t], 0, 0))
    out = pl.pallas_call(
        _ffn_kernel,
        grid_spec=pltpu.PrefetchScalarGridSpec(
            num_scalar_prefetch=2, grid=(n_tiles,),
            in_specs=[rows, per_e(D_MODEL, 2 * D_FF), per_e(1, 2 * D_FF), per_e(D_FF, D_MODEL), per_e(1, D_MODEL)],
            out_specs=rows,
            scratch_shapes=[pltpu.VMEM((D_MODEL, 2 * D_FF), BF16), pltpu.VMEM((D_FF, D_MODEL), BF16)]),
        out_shape=jax.ShapeDtypeStruct((n_tiles * tm, D_MODEL), BF16),
        compiler_params=_params(1),
        name="moe_ffn",
    )(tile_expert, n_valid, xs, w["w_gate_up"], w["b_gate_up"], w["w_down"], w["b_down"])

    first = pl.BlockSpec((TOP_K, tb), lambda j, *_: (0, 0))
    ahead = pl.BlockSpec((TOP_K, tb), lambda j, *_: (0, jnp.minimum(j + 1, nb - 1)))
    return pl.pallas_call(
        _combine_kernel,
        grid_spec=pltpu.PrefetchScalarGridSpec(
            num_scalar_prefetch=4, grid=(nb,),
            in_specs=[first, first, first, ahead, ahead, ahead, body_rows, last_rows,
                      pl.BlockSpec((1, D_MODEL), lambda j, *_: (0, 0)), pl.BlockSpec(memory_space=pl.ANY)],
            out_specs=[body_rows, last_rows],
            scratch_shapes=[pltpu.VMEM((2, r_blk, D_MODEL), BF16), pltpu.VMEM((2, r_blk, tb), BF16),
                            pltpu.SemaphoreType.DMA((2,))]),
        out_shape=[jax.ShapeDtypeStruct((T, D_MODEL), F32), jax.ShapeDtypeStruct((tb, D_MODEL), F32)],
        compiler_params=_params(1),
        name="moe_combine",
    )(*plan, idx, rank, wt, idx, rank, wt, h2, h2_last, w["g_final"], out)


def _rope_table(pos):
    half = QK_ROPE // 2
    inv = ROPE_THETA ** (-jnp.arange(half, dtype=F32) * 2.0 / QK_ROPE)
    ang = pos[:, None] * inv[None, :]
    cos, sin, zero = jnp.cos(ang), jnp.sin(ang), jnp.zeros_like(ang)
    return jnp.concatenate([cos, cos, cos, cos, -sin, zero, -sin, zero, zero, sin, zero, sin], axis=1)


def _prepare_weights(g_attn, w_in, g_q, g_kv, w_uq, w_uk, w_uv, w_a2, b_a, g_gla, w_o_mla, w_o_gla, w_out,
                     g_mem, w_mk, w_mv, g_cross, w_mq, w_mo, g_ffn, w_router, b_router, w_gate_up, b_gate_up,
                     w_down, b_down, g_final):
    splits = np.cumsum((0,) + IN_SPLITS)
    parts = [w_in[0][:, splits[i]:splits[i + 1]] for i in range(len(IN_SPLITS))]
    parts = [jnp.pad(p, ((0, 0), (0, wd - p.shape[1]))) for p, wd in zip(parts, _PACKED_WIDTHS)]
    wq = w_uq[0].reshape(Q_LORA, MLA_HEADS, QK_NOPE + QK_ROPE)
    wq_rope = jnp.pad(wq[:, :, QK_NOPE:], ((0, 0), (0, 0), (0, LANES - QK_ROPE)))
    row = lambda v: v.reshape(1, -1)
    return dict(
        g_attn=row(g_attn[0]), w_in=jnp.concatenate(parts, axis=1).astype(BF16), g_q=row(g_q[0]), g_kv=row(g_kv[0]),
        w_uq=jnp.concatenate([_fold_latent_query(jnp.transpose(wq[:, :, :QK_NOPE], (1, 0, 2)),
                                                 jnp.transpose(w_uk[0], (1, 0, 2))),
                              wq_rope.reshape(Q_LORA, -1).astype(BF16)], axis=1),
        wuv=jnp.transpose(w_uv[0], (1, 0, 2)).astype(BF16),
        w_a2=jnp.pad(w_a2[0], ((0, LANES - GLA_RANK), (0, 0))).astype(BF16), b_a=row(b_a[0]),
        g_gla=row(g_gla[0]), w_o_mla=w_o_mla[0].astype(BF16), w_o_gla=w_o_gla[0].astype(BF16),
        w_out=w_out[0].astype(BF16), g_mem=row(g_mem[0]),
        w_mkv=jnp.concatenate([w_mk[0], w_mv[0]], axis=1).astype(BF16),
        g_cross=row(g_cross[0]), w_mq=w_mq[0].astype(BF16), w_mo=w_mo[0].astype(BF16), g_ffn=row(g_ffn[0]),
        w_router_t=w_router[0].T, b_router=b_router[0].reshape(-1, 1),
        w_gate_up=w_gate_up[0], b_gate_up=b_gate_up[0].reshape(N_EXPERTS, 1, 2 * D_FF),
        w_down=w_down[0], b_down=b_down[0].reshape(N_EXPERTS, 1, D_MODEL), g_final=row(g_final),
    )


def kernel(x_prompt, x_sample, cache_kv_latent, cache_k_rope, state_gla, cache_mem_k, cache_mem_v, page_table, mem_prompt, g_attn, w_in, g_q, g_kv, w_uq, w_uk, w_uv, w_a2, b_a, g_gla, w_o_mla, w_o_gla, w_out, g_mem, w_mk, w_mv, g_cross, w_mq, w_mo, g_ffn, w_router, b_router, w_gate_up, b_gate_up, w_down, b_down, g_final):
    w = _prepare_weights(g_attn, w_in, g_q, g_kv, w_uq, w_uk, w_uv, w_a2, b_a, g_gla, w_o_mla, w_o_gla, w_out,
                         g_mem, w_mk, w_mv, g_cross, w_mq, w_mo, g_ffn, w_router, b_router, w_gate_up, b_gate_up,
                         w_down, b_down, g_final)
    B, S, D = x_prompt.shape
    Bd = x_sample.shape[0]
    past_len = page_table.shape[1] * cache_kv_latent.shape[2]

    tm = min(256, S)
    xp = x_prompt.reshape(B * S, D)
    (qcat, ckv, krope, kcat, kcat_t, gq, gk, gv, sg, la, szm, szg) = _premix(
        xp, _rope_table(jnp.arange(S, dtype=F32)), w, min(PREMIX_ROWS, S), tm, BF16)
    olat = _mla_prompt(qcat.reshape(MLA_HEADS, B, S, QCAT), kcat.reshape(B, S, QCAT), kcat_t, tm)
    og, state_p = _gla_prompt(gq.reshape(B, S, -1), gk.reshape(B, S, -1), gv.reshape(B, S, -1), la.reshape(B, S, -1))
    h1, qx = _postmix(xp, olat.reshape(B * S, -1), og.reshape(B * S, -1), sg, szm, szg, w, min(POSTMIX_ROWS, S))
    mem_k, mem_v = _memkv(mem_prompt.reshape(B * N_MEM, D), w["g_mem"], w["w_mkv"], min(512, B * N_MEM))
    tb = min(MOE_BLOCK, S)
    h2, xn2, idx, rank, wt, cnt = _cross_prompt(qx, h1, mem_k.reshape(B, N_MEM, MEM_WIDTH),
                                                mem_v.reshape(B, N_MEM, MEM_WIDTH), w, tb)

    xs = x_sample.reshape(Bd, D)
    (qcat_s, ckv_s, krope_s, kcat_s, _, gq_s, gk_s, gv_s, sg_s, la_s, szm_s, szg_s) = _premix(
        xs, _rope_table(jnp.full((Bd,), past_len, F32)), w, Bd, Bd, F32)
    olat_s = _mla_sample(page_table, qcat_s, kcat_s, cache_kv_latent[0], jnp.swapaxes(cache_k_rope[0], 1, 2))
    og_s, state_s = _gla_sample(gq_s, gk_s, gv_s, la_s, state_gla[0])
    h1_s, qx_s = _postmix(xs, olat_s.reshape(Bd, -1), og_s, sg_s, szm_s, szg_s, w, Bd)
    h2_s, xn2_s, idx_s, rank_s, wt_s, cnt_s = _cross_sample(
        qx_s, h1_s, cache_mem_k.reshape(Bd, N_MEM * MEM_HEADS, MEM_HEAD_DIM),
        cache_mem_v.reshape(Bd, N_MEM * MEM_HEADS, MEM_HEAD_DIM), w)

    fill = lambda a, v: jnp.pad(a, ((0, 0), (0, tb - Bd)), constant_values=v)
    rows = lambda a: jnp.pad(a, ((0, tb - Bd), (0, 0)))
    y_prompt, y_last = _moe(
        xn2, rows(xn2_s), jnp.concatenate([idx, fill(idx_s, N_EXPERTS)], axis=1),
        jnp.concatenate([rank, fill(rank_s, -1)], axis=1), jnp.concatenate([wt, fill(wt_s, 0.0)], axis=1),
        jnp.concatenate([cnt.reshape(-1, N_EXPERTS), cnt_s.reshape(-1, N_EXPERTS)], axis=0), h2, rows(h2_s), w, tb, tb)
    y_sample = y_last[:Bd]

    return (y_prompt.reshape(B, S, D), y_sample.reshape(Bd, 1, D),
            ckv.reshape(1, B, S, KV_LORA), krope.reshape(1, B, S, QK_ROPE), state_p[None],
            mem_k.reshape(1, B, N_MEM, MEM_HEADS, MEM_HEAD_DIM), mem_v.reshape(1, B, N_MEM, MEM_HEADS, MEM_HEAD_DIM),
            ckv_s.reshape(1, Bd, 1, KV_LORA), krope_s.reshape(1, Bd, 1, QK_ROPE), state_s[None])
```

```python
import functools

import jax
import jax.numpy as jnp
import numpy as np
from jax import lax
from jax.experimental import pallas as pl
from jax.experimental.pallas import tpu as pltpu

F32 = jnp.float32
BF16 = jnp.bfloat16

D_MODEL = 1024
MLA_HEADS = 8
Q_LORA = 256
KV_LORA = 128
QK_NOPE = 128
QK_ROPE = 64
ROPE_THETA = 10000.0
ATTN_SCALE = (QK_NOPE + QK_ROPE) ** -0.5
GLA_HEADS = 4
GLA_DK = 128
GLA_DV = 256
GLA_RANK = 16
GLA_TAU = 16.0
GLA_CHUNK = 64
GLA_GROUP = 4
N_MEM = 256
MEM_HEADS = 4
MEM_HEAD_DIM = 128
N_EXPERTS = 32
TOP_K = 4
D_FF = 1024
SWIGLU_LIMIT = 7.0
SWIGLU_ALPHA = 1.702
EPS = 1e-6

GLA_K_WIDTH = GLA_HEADS * GLA_DK
GLA_V_WIDTH = GLA_HEADS * GLA_DV
MEM_WIDTH = MEM_HEADS * MEM_HEAD_DIM
IN_SPLITS = (Q_LORA, KV_LORA, QK_ROPE, GLA_K_WIDTH, GLA_K_WIDTH, GLA_V_WIDTH, GLA_V_WIDTH, GLA_RANK, D_MODEL, D_MODEL)

LANES = 128
SUBLANES = 8
_PACKED_WIDTHS = (Q_LORA, KV_LORA, LANES, GLA_K_WIDTH, GLA_K_WIDTH, GLA_V_WIDTH, GLA_V_WIDTH, LANES, D_MODEL, D_MODEL)
_OFF = tuple(int(v) for v in np.cumsum((0,) + _PACKED_WIDTHS))
QCAT = 2 * LANES

VMEM_LIMIT = 56 << 20
PREMIX_ROWS = 512
POSTMIX_ROWS = 512
MLA_BLOCK = 256
MEMKV_ROWS = 512
SAMPLES_PER_STEP = 8


def _params(n_axes):
    return pltpu.CompilerParams(dimension_semantics=("arbitrary",) * n_axes, vmem_limit_bytes=VMEM_LIMIT)


def _rms(x, g):
    var = jnp.mean(x * x, axis=-1, keepdims=True)
    return x * lax.rsqrt(var + EPS) * g


def _dot(a, b):
    return jnp.dot(a.astype(BF16), b.astype(BF16), preferred_element_type=F32)


def _dot_nt(a, b):
    return lax.dot_general(a.astype(BF16), b.astype(BF16), (((1,), (1,)), ((), ())), preferred_element_type=F32)


def _dot_tn(a, b):
    return lax.dot_general(a.astype(BF16), b.astype(BF16), (((0,), (0,)), ((), ())), preferred_element_type=F32)


def _split3(x):
    hi = x.astype(BF16)
    r1 = x - hi.astype(F32)
    mid = r1.astype(BF16)
    lo = (r1 - mid.astype(F32)).astype(BF16)
    return hi, mid, lo


def _dot_f32_nt(a, b):
    a0, a1, a2 = _split3(a)
    b0, b1, b2 = _split3(b)
    d = lambda u, v: lax.dot_general(u, v, (((1,), (1,)), ((), ())), preferred_element_type=F32)
    return ((d(a1, b1) + d(a0, b2) + d(a2, b0)) + (d(a0, b1) + d(a1, b0))) + d(a0, b0)


def _dot_hi_nt(a, b):
    a0 = a.astype(BF16)
    a1 = (a - a0.astype(F32)).astype(BF16)
    b0 = b.astype(BF16)
    b1 = (b - b0.astype(F32)).astype(BF16)
    d = lambda u, v: lax.dot_general(u, v, (((1,), (1,)), ((), ())), preferred_element_type=F32)
    return (d(a0, b1) + d(a1, b0)) + d(a0, b0)


def _full(shape, buffers=None):
    n = len(shape)
    return pl.BlockSpec(shape, lambda *_: (0,) * n, pipeline_mode=pl.Buffered(buffers) if buffers else None)


def _premix_kernel(x_ref, tab_ref, g_attn_ref, w_in_ref, g_q_ref, g_kv_ref, w_uq_ref, w_a2_ref, b_a_ref,
                   qcat_ref, ckv_ref, krope_ref, kcat_ref, kcat_t_ref, gq_ref, gk_ref, gv_ref, sg_ref, la_ref, szm_ref, szg_ref):
    xn = _rms(x_ref[...], g_attn_ref[...]).astype(BF16)

    def proj(lo, hi):
        y = jnp.dot(xn, w_in_ref[:, _OFF[lo]:_OFF[hi]], preferred_element_type=F32)
        return [y[:, _OFF[i] - _OFF[lo]:_OFF[i + 1] - _OFF[lo]] for i in range(lo, hi)]

    tab = tab_ref[...]
    cos, sin_lo, sin_hi = tab[:, :LANES], tab[:, LANES:2 * LANES], tab[:, 2 * LANES:]

    def rope(t):
        return t * cos + pltpu.roll(t, LANES - QK_ROPE // 2, 1) * sin_lo + pltpu.roll(t, QK_ROPE // 2, 1) * sin_hi

    cq, ckv, kr = proj(0, 3)
    c_q = _rms(cq, g_q_ref[...]).astype(BF16)
    ckv = _rms(ckv, g_kv_ref[...])
    kr = rope(kr)
    ckv_ref[...] = ckv
    krope_ref[...] = kr[:, :QK_ROPE]
    kcat = jnp.concatenate([ckv, kr], axis=1)
    kcat_ref[...] = kcat.astype(BF16)
    tk = kcat_t_ref.shape[2]
    for t in range(kcat_t_ref.shape[0]):
        kcat_t_ref[t] = kcat[t * tk:(t + 1) * tk].T.astype(BF16)
    q_all = jnp.dot(c_q, w_uq_ref[...], preferred_element_type=F32)
    for h in range(MLA_HEADS):
        qcat_ref[h, :, :LANES] = q_all[:, h * LANES:(h + 1) * LANES].astype(qcat_ref.dtype)
        qcat_ref[h, :, LANES:] = rope(q_all[:, (MLA_HEADS + h) * LANES:(MLA_HEADS + h + 1) * LANES]).astype(qcat_ref.dtype)
    gq, gk = proj(3, 5)
    gq_ref[...] = (gq * GLA_DK ** -0.5).astype(gq_ref.dtype)
    gk_ref[...] = gk.astype(gk_ref.dtype)
    gv_ref[...] = proj(5, 6)[0].astype(gv_ref.dtype)
    gg, ga = proj(6, 8)
    sg_ref[...] = (gg * jax.nn.sigmoid(gg)).astype(sg_ref.dtype)
    z = _dot(ga, w_a2_ref[...]) + b_a_ref[...]
    la_ref[...] = -(jnp.maximum(-z, 0.0) + jnp.log1p(jnp.exp(-jnp.abs(z)))) * (1.0 / GLA_TAU)
    szm_ref[...] = jax.nn.sigmoid(proj(8, 9)[0]).astype(szm_ref.dtype)
    szg_ref[...] = jax.nn.sigmoid(proj(9, 10)[0]).astype(szg_ref.dtype)


def _fold_kernel(wq_ref, wk_ref, o_ref):
    o_ref[...] = _dot_f32_nt(wq_ref[...], wk_ref[...]).astype(o_ref.dtype)


def _fold_latent_query(w_uq_nope, w_uk):
    return pl.pallas_call(
        _fold_kernel,
        grid=(MLA_HEADS,),
        in_specs=[pl.BlockSpec((None, Q_LORA, QK_NOPE), lambda h: (h, 0, 0)),
                  pl.BlockSpec((None, KV_LORA, QK_NOPE), lambda h: (h, 0, 0))],
        out_specs=pl.BlockSpec((Q_LORA, KV_LORA), lambda h: (0, h)),
        out_shape=jax.ShapeDtypeStruct((Q_LORA, MLA_HEADS * KV_LORA), BF16),
        compiler_params=_params(1),
        name="fold_latent_query",
    )(w_uq_nope, w_uk)


def _premix(x, tab, w, tm, tk, gdtype):
    T = x.shape[0]
    n_tab = tab.shape[0] // tm
    row = lambda n: pl.BlockSpec((tm, n), lambda i: (i, 0))
    outs = [(KV_LORA, F32), (QK_ROPE, F32), (QCAT, BF16), None, (GLA_K_WIDTH, gdtype),
            (GLA_K_WIDTH, gdtype), (GLA_V_WIDTH, gdtype), (GLA_V_WIDTH, BF16), (GLA_K_WIDTH, F32),
            (D_MODEL, BF16), (D_MODEL, BF16)]
    out_specs = [row(o[0]) if o else pl.BlockSpec((tm // tk, QCAT, tk), lambda i: (i, 0, 0)) for o in outs]
    out_shape = [jax.ShapeDtypeStruct((T, o[0]) if o else (T // tk, QCAT, tk), o[1] if o else BF16) for o in outs]
    consts = [w["g_attn"], w["w_in"], w["g_q"], w["g_kv"], w["w_uq"], w["w_a2"], w["b_a"]]
    const_specs = [pl.BlockSpec(c.shape, lambda i, n=len(c.shape): (0,) * n,
                                pipeline_mode=pl.Buffered(1) if c is w["w_in"] else None) for c in consts]
    return pl.pallas_call(
        _premix_kernel,
        grid=(T // tm,),
        in_specs=[row(D_MODEL), pl.BlockSpec((tm, 3 * LANES), lambda i: (i % n_tab, 0))] + const_specs,
        out_specs=[pl.BlockSpec((MLA_HEADS, tm, QCAT), lambda i: (0, i, 0))] + out_specs,
        out_shape=[jax.ShapeDtypeStruct((MLA_HEADS, T, QCAT), BF16)] + out_shape,
        compiler_params=_params(1),
        name="premix",
    )(x, tab, *consts)


def _mla_prompt_kernel(q_ref, k_ref, kt_ref, o_ref, m_ref, l_ref, acc_ref):
    tq = q_ref.shape[1]
    qi = pl.program_id(1)
    q = q_ref[...].reshape(MLA_HEADS * tq, QCAT)
    row = lax.broadcasted_iota(jnp.int32, (MLA_HEADS, tq, tq), 1).reshape(MLA_HEADS * tq, tq)
    col = lax.broadcasted_iota(jnp.int32, (MLA_HEADS * tq, tq), 1)
    m_ref[...] = jnp.full_like(m_ref, -jnp.inf)
    l_ref[...] = jnp.zeros_like(l_ref)
    acc_ref[...] = jnp.zeros_like(acc_ref)
    c = ATTN_SCALE * np.log2(np.e)

    def step(j, n, mask_last):
        keys = pl.ds(pl.multiple_of(j * tq, tq), n * tq)
        parts = [jnp.dot(q, kt_ref[j + t], preferred_element_type=F32) for t in range(n)]
        if mask_last:
            parts[-1] = jnp.where(col <= row, parts[-1], -jnp.inf)
        s = jnp.concatenate(parts, axis=1)
        m = m_ref[...]
        m_new = jnp.maximum(m, jnp.max(s, axis=-1, keepdims=True))
        alpha = jnp.exp2((m - m_new) * c)
        p = jnp.exp2((s - jnp.concatenate([m_new] * (n * tq // LANES), axis=1)) * c)
        p_lanes = p[:, :LANES]
        for t in range(1, n * tq // LANES):
            p_lanes = p_lanes + p[:, t * LANES:(t + 1) * LANES]
        l_ref[...] = alpha * l_ref[...] + p_lanes
        acc_ref[...] = alpha * acc_ref[...] + _dot(p, k_ref[keys, :KV_LORA])
        m_ref[...] = m_new

    def body(jj, carry):
        step(2 * jj, 2, False)
        return carry

    lax.fori_loop(0, qi // 2, body, 0)

    @pl.when(qi % 2 == 0)
    def _():
        step(qi, 1, True)

    @pl.when(qi % 2 == 1)
    def _():
        step(qi - 1, 2, True)
    o = acc_ref[...] / jnp.sum(l_ref[...], axis=-1, keepdims=True)
    for h in range(MLA_HEADS):
        o_ref[:, h * KV_LORA:(h + 1) * KV_LORA] = o[h * tq:(h + 1) * tq].astype(o_ref.dtype)


def _mla_prompt(qcat, kcat, kcat_t, tq):
    _, B, S, _ = qcat.shape
    rows = MLA_HEADS * tq
    return pl.pallas_call(
        _mla_prompt_kernel,
        grid=(B, S // tq),
        in_specs=[pl.BlockSpec((MLA_HEADS, None, tq, QCAT), lambda b, i: (0, b, i, 0)),
                  pl.BlockSpec((None, S, QCAT), lambda b, i: (b, 0, 0)),
                  pl.BlockSpec((S // tq, QCAT, tq), lambda b, i: (b, 0, 0))],
        out_specs=pl.BlockSpec((None, tq, MLA_HEADS * KV_LORA), lambda b, i: (b, i, 0)),
        out_shape=jax.ShapeDtypeStruct((B, S, MLA_HEADS * KV_LORA), BF16),
        scratch_shapes=[pltpu.VMEM((rows, LANES), F32), pltpu.VMEM((rows, LANES), F32), pltpu.VMEM((rows, KV_LORA), F32)],
        compiler_params=_params(2),
        name="mla_prompt",
    )(qcat, kcat, kcat_t)


def _mla_sample_kernel(pt_ref, q_ref, knew_ref, lat_hbm, rope_hbm, o_ref, lat_buf, rope_buf, lat_bf, rope_bf, sems):
    b = pl.program_id(0)
    n_pages, page = lat_buf.shape[1], lat_buf.shape[2]

    def page_copies(sample, slot, i):
        pg = pt_ref[sample, i]
        return (pltpu.make_async_copy(lat_hbm.at[pg], lat_buf.at[slot, i], sems.at[0, slot]),
                pltpu.make_async_copy(rope_hbm.at[pg], rope_buf.at[slot, i], sems.at[1, slot]))

    def start_pages(sample, slot):
        def body(i, carry):
            for cp in page_copies(sample, slot, i):
                cp.start()
            return carry
        lax.fori_loop(0, n_pages, body, 0)

    slot = b % 2

    @pl.when(b == 0)
    def _():
        start_pages(0, 0)

    @pl.when(b + 1 < pl.num_programs(0))
    def _():
        start_pages(b + 1, 1 - slot)

    pltpu.make_async_copy(lat_hbm.at[pl.ds(0, n_pages)], lat_buf.at[slot], sems.at[0, slot]).wait()
    pltpu.make_async_copy(rope_hbm.at[pl.ds(0, n_pages)], rope_buf.at[slot], sems.at[1, slot]).wait()

    lat_bf[...] = lat_buf[slot].reshape(n_pages * page, KV_LORA).astype(BF16)
    for i in range(n_pages):
        rope_bf[:, i * page:(i + 1) * page] = rope_buf[slot, i].astype(BF16)
    q = q_ref[...].reshape(MLA_HEADS, QCAT)
    knew = knew_ref[...].astype(F32)
    s = (_dot_nt(q[:, :KV_LORA], lat_bf[...]) + _dot(q[:, KV_LORA:KV_LORA + QK_ROPE], rope_bf[...])) * ATTN_SCALE
    s_new = jnp.sum(q.astype(F32) * knew, axis=-1, keepdims=True) * ATTN_SCALE
    m = jnp.maximum(jnp.max(s, axis=-1, keepdims=True), s_new)
    p = jnp.exp(s - m)
    p_new = jnp.exp(s_new - m)
    denom = jnp.sum(p, axis=-1, keepdims=True) + p_new
    o = _dot(p, lat_bf[...]) + p_new * knew[:, :KV_LORA]
    o_ref[...] = (o / denom).astype(o_ref.dtype)


def _mla_sample(page_table, qcat, knew, cache_lat, cache_rope_t):
    Bd, n_pages = page_table.shape
    page = cache_lat.shape[1]
    grid_spec = pltpu.PrefetchScalarGridSpec(
        num_scalar_prefetch=1,
        grid=(Bd,),
        in_specs=[pl.BlockSpec((MLA_HEADS, None, 1, QCAT), lambda b, pt: (0, b, 0, 0)),
                  pl.BlockSpec((None, 1, QCAT), lambda b, pt: (b, 0, 0)),
                  pl.BlockSpec(memory_space=pl.ANY), pl.BlockSpec(memory_space=pl.ANY)],
        out_specs=pl.BlockSpec((None, MLA_HEADS, KV_LORA), lambda b, pt: (b, 0, 0)),
        scratch_shapes=[pltpu.VMEM((2, n_pages, page, KV_LORA), F32), pltpu.VMEM((2, n_pages, QK_ROPE, page), F32),
                        pltpu.VMEM((n_pages * page, KV_LORA), BF16), pltpu.VMEM((QK_ROPE, n_pages * page), BF16),
                        pltpu.SemaphoreType.DMA((2, 2))],
    )
    return pl.pallas_call(
        _mla_sample_kernel,
        grid_spec=grid_spec,
        out_shape=jax.ShapeDtypeStruct((Bd, MLA_HEADS, KV_LORA), BF16),
        compiler_params=_params(1),
        name="mla_sample",
    )(page_table, qcat.reshape(MLA_HEADS, Bd, 1, QCAT), knew.reshape(Bd, 1, QCAT), cache_lat, cache_rope_t)


def _gla_prompt_kernel(q_ref, k_ref, v_ref, la_ref, o_ref, state_ref, st_ref):
    C = GLA_CHUNK
    S = q_ref.shape[0]
    row = lax.broadcasted_iota(jnp.int32, (C, 2 * C), 0)
    col = lax.broadcasted_iota(jnp.int32, (C, 2 * C), 1)
    causal = row >= col
    tri = causal[:, :C].astype(BF16)
    pad_k = jnp.zeros((C, GLA_DK), F32)
    pad_v = jnp.zeros((C, GLA_DV), F32)
    st_ref[...] = jnp.zeros_like(st_ref)

    group = min(GLA_GROUP, S // C)

    def chunks(c, carry):
        parts = []
        for g in range(group):
            rows = pl.ds(pl.multiple_of((c * group + g) * C, C), C)
            split = jnp.dot(tri, jnp.concatenate(_split3(la_ref[rows, :]), axis=1), preferred_element_type=F32)
            b_all = (split[:, 2 * GLA_K_WIDTH:] + split[:, GLA_K_WIDTH:2 * GLA_K_WIDTH]) + split[:, :GLA_K_WIDTH]
            for h in range(GLA_HEADS):
                kcols = slice(h * GLA_DK, (h + 1) * GLA_DK)
                vcols = slice(h * GLA_DV, (h + 1) * GLA_DV)
                b = b_all[:, kcols]
                b_last = b[C - 1:C, :]
                q = q_ref[rows, kcols].astype(F32)
                k = k_ref[rows, kcols].astype(F32)
                q_in = (q * jnp.exp(b)).astype(BF16)
                k_in = jnp.concatenate([k * jnp.exp(-b), pad_k], axis=0)
                k_out = jnp.concatenate([k * jnp.exp(b_last - b), pad_k], axis=0)
                a = jnp.where(causal, _dot_nt(q_in, k_in), 0.0)
                v_t = jnp.concatenate([v_ref[rows, vcols].astype(F32), pad_v], axis=0).T.astype(BF16)
                lhs = jnp.concatenate([q_in, a.astype(BF16)], axis=1)
                parts.append((rows, h, vcols, lhs, v_t, jnp.exp(b_last), _dot(v_t, k_out)))
        for rows, h, vcols, lhs, v_t, decay, kv_t in parts:
            st = st_ref[h]
            o_ref[rows, vcols] = _dot_nt(lhs, jnp.concatenate([st.astype(BF16), v_t], axis=1)).astype(o_ref.dtype)
            st_ref[h] = st * decay + kv_t
        return carry

    lax.fori_loop(0, S // C // group, chunks, 0)
    for h in range(GLA_HEADS):
        state_ref[h] = st_ref[h].T


def _gla_prompt(q, k, v, la):
    B, S, _ = q.shape
    seq = lambda n: pl.BlockSpec((None, S, n), lambda b: (b, 0, 0))
    return pl.pallas_call(
        _gla_prompt_kernel,
        grid=(B,),
        in_specs=[seq(GLA_K_WIDTH), seq(GLA_K_WIDTH), seq(GLA_V_WIDTH), seq(GLA_K_WIDTH)],
        out_specs=[seq(GLA_V_WIDTH), pl.BlockSpec((None, GLA_HEADS, GLA_DK, GLA_DV), lambda b: (b, 0, 0, 0))],
        out_shape=[jax.ShapeDtypeStruct((B, S, GLA_V_WIDTH), BF16),
                   jax.ShapeDtypeStruct((B, GLA_HEADS, GLA_DK, GLA_DV), F32)],
        scratch_shapes=[pltpu.VMEM((GLA_HEADS, GLA_DV, GLA_DK), F32)],
        compiler_params=_params(1),
        name="gla_prompt",
    )(q, k, v, la)


def _gla_sample_kernel(q_ref, k_ref, v_ref, la_ref, st_ref, o_ref, sto_ref):
    nb = q_ref.shape[0]
    pad = jnp.zeros((GLA_DK - nb, GLA_DK), F32)
    for h in range(GLA_HEADS):
        kcols = slice(h * GLA_DK, (h + 1) * GLA_DK)
        vcols = slice(h * GLA_DV, (h + 1) * GLA_DV)
        col = lambda ref: jnp.concatenate([ref[:, kcols].astype(F32), pad], axis=0).T
        q_t, k_t, decay_t = col(q_ref), col(k_ref), jnp.exp(col(la_ref))
        for i in range(nb):
            new = decay_t[:, i:i + 1] * st_ref[i, h] + k_t[:, i:i + 1] * v_ref[i:i + 1, vcols].astype(F32)
            sto_ref[i, h] = new
            o_ref[i:i + 1, vcols] = jnp.sum(q_t[:, i:i + 1] * new, axis=0, keepdims=True).astype(o_ref.dtype)


def _gla_sample(q, k, v, la, state, nb=SAMPLES_PER_STEP):
    Bd = q.shape[0]
    rows = lambda n: pl.BlockSpec((nb, n), lambda i: (i, 0))
    st_spec = pl.BlockSpec((nb, GLA_HEADS, GLA_DK, GLA_DV), lambda i: (i, 0, 0, 0))
    return pl.pallas_call(
        _gla_sample_kernel,
        grid=(Bd // nb,),
        in_specs=[rows(GLA_K_WIDTH), rows(GLA_K_WIDTH), rows(GLA_V_WIDTH), rows(GLA_K_WIDTH), st_spec],
        out_specs=[rows(GLA_V_WIDTH), st_spec],
        out_shape=[jax.ShapeDtypeStruct((Bd, GLA_V_WIDTH), BF16), jax.ShapeDtypeStruct(state.shape, F32)],
        compiler_params=_params(1),
        name="gla_sample",
    )(q, k, v, la, state)


def _postmix_kernel(x_ref, olat_ref, og_ref, sg_ref, szm_ref, szg_ref, wuv_ref, w_o_mla_ref, g_gla_ref, w_o_gla_ref,
                    w_out_ref, g_cross_ref, w_mq_ref, h_ref, qx_ref):
    ov = jnp.concatenate(
        [_dot(olat_ref[:, h * KV_LORA:(h + 1) * KV_LORA], wuv_ref[h]).astype(BF16) for h in range(MLA_HEADS)], axis=1)
    o_mla = jnp.dot(ov, w_o_mla_ref[...], preferred_element_type=F32)
    g_gla = g_gla_ref[...]
    og = jnp.concatenate(
        [(_rms(og_ref[:, h * GLA_DV:(h + 1) * GLA_DV].astype(F32), g_gla)
          * sg_ref[:, h * GLA_DV:(h + 1) * GLA_DV].astype(F32)).astype(BF16) for h in range(GLA_HEADS)], axis=1)
    o_gla = jnp.dot(og, w_o_gla_ref[...], preferred_element_type=F32)
    merged = szm_ref[...].astype(F32) * o_mla + szg_ref[...].astype(F32) * o_gla
    h1 = x_ref[...] + _dot(merged, w_out_ref[...])
    h_ref[...] = h1
    qx_ref[...] = _dot(_rms(h1, g_cross_ref[...]), w_mq_ref[...]).astype(qx_ref.dtype)


def _postmix(x, olat, og, sg, szm, szg, w, tm):
    T = x.shape[0]
    row = lambda n: pl.BlockSpec((tm, n), lambda i: (i, 0))
    consts = [w["wuv"], w["w_o_mla"], w["g_gla"], w["w_o_gla"], w["w_out"], w["g_cross"], w["w_mq"]]
    return pl.pallas_call(
        _postmix_kernel,
        grid=(T // tm,),
        in_specs=[row(D_MODEL)] * 6 + [_full(c.shape, buffers=1) for c in consts],
        out_specs=[row(D_MODEL), row(MEM_WIDTH)],
        out_shape=[jax.ShapeDtypeStruct((T, D_MODEL), F32), jax.ShapeDtypeStruct((T, MEM_WIDTH), BF16)],
        compiler_params=_params(1),
        name="postmix",
    )(x, olat, og, sg, szm, szg, *consts)


def _memkv_kernel(mem_ref, g_ref, w_ref, k_ref, v_ref):
    kv = _dot(_rms(mem_ref[...], g_ref[...]), w_ref[...])
    k_ref[...] = kv[:, :MEM_WIDTH]
    v_ref[...] = kv[:, MEM_WIDTH:]


def _memkv(mem, g_mem, w_mkv, tm):
    T = mem.shape[0]
    row = lambda n: pl.BlockSpec((tm, n), lambda i: (i, 0))
    return pl.pallas_call(
        _memkv_kernel,
        grid=(T // tm,),
        in_specs=[row(D_MODEL), _full(g_mem.shape), _full(w_mkv.shape)],
        out_specs=[row(MEM_WIDTH), row(MEM_WIDTH)],
        out_shape=[jax.ShapeDtypeStruct((T, MEM_WIDTH), F32)] * 2,
        compiler_params=_params(1),
        name="memkv",
    )(mem, g_mem, w_mkv)


def _attend_many(problems):
    scores = [_dot_nt(q, k) * MEM_HEAD_DIM ** -0.5 for q, k, _ in problems]
    probs = []
    for s in scores:
        p = jnp.exp(s - jnp.max(s, axis=-1, keepdims=True))
        probs.append(p / jnp.sum(p, axis=-1, keepdims=True))
    return [_dot(p, v) for p, (_, _, v) in zip(probs, problems)]


def _head_cols(h):
    return slice(h * MEM_HEAD_DIM, (h + 1) * MEM_HEAD_DIM)


def _route(o, h1, w_mo_ref, g_ffn_ref, w_router_ref, b_router_ref, h_ref, xn_ref, idx_ref, rank_ref, wt_ref, cnt_ref):
    h2 = h1 + jnp.dot(o, w_mo_ref[...], preferred_element_type=F32)
    h_ref[...] = h2
    xn = _rms(h2, g_ffn_ref[...])
    xn_ref[...] = xn.astype(xn_ref.dtype)
    tb = xn.shape[0]
    logits = _dot_hi_nt(w_router_ref[...], xn) + b_router_ref[...]
    expert = lax.broadcasted_iota(jnp.int32, logits.shape, 0)
    work = logits
    hits, firsts, exps = [], [], []
    top = None
    for _ in range(TOP_K):
        best = jnp.max(work, axis=0, keepdims=True)
        first = jnp.min(jnp.where(work == best, expert, N_EXPERTS), axis=0, keepdims=True)
        hit = expert == first
        top = best if top is None else top
        hits.append(hit)
        firsts.append(first)
        exps.append(jnp.exp(best - top))
        work = jnp.where(hit, -jnp.inf, work)
    denom = (exps[0] + exps[1]) + (exps[2] + exps[3])
    chosen = jnp.zeros(logits.shape, F32)
    for hit in hits:
        chosen = chosen + jnp.where(hit, 1.0, 0.0)
    before = (lax.broadcasted_iota(jnp.int32, (tb, tb), 0) < lax.broadcasted_iota(jnp.int32, (tb, tb), 1)).astype(BF16)
    rank = jnp.dot(chosen.astype(BF16), before, preferred_element_type=F32)
    idx_ref[...] = jnp.concatenate(firsts, axis=0)
    rank_ref[...] = jnp.concatenate(
        [jnp.sum(jnp.where(hit, rank, 0.0), axis=0, keepdims=True) for hit in hits], axis=0).astype(jnp.int32)
    wt_ref[...] = jnp.concatenate([e / denom for e in exps], axis=0)
    cnt_ref[...] = jnp.sum(chosen, axis=1, keepdims=True).astype(jnp.int32)


def _cross_prompt_kernel(qx_ref, h1_ref, mk_ref, mv_ref, *rest):
    outs = _attend_many([(qx_ref[:, _head_cols(h)], mk_ref[:, _head_cols(h)], mv_ref[:, _head_cols(h)])
                         for h in range(MEM_HEADS)])
    _route(jnp.concatenate([o.astype(BF16) for o in outs], axis=1), h1_ref[...], *rest)


def _cross_sample_kernel(qx_ref, mk_ref, mv_ref, o_ref):
    rows = mk_ref.shape[1]
    pad = jnp.zeros((SUBLANES - MEM_HEADS, MEM_HEAD_DIM), qx_ref.dtype)
    head_of_row = lax.broadcasted_iota(jnp.int32, (SUBLANES, rows), 1) % MEM_HEADS
    own_head = head_of_row == lax.broadcasted_iota(jnp.int32, (SUBLANES, rows), 0)
    samples = range(qx_ref.shape[0])
    scores = [_dot_nt(jnp.concatenate([qx_ref[i:i + 1, _head_cols(h)] for h in range(MEM_HEADS)] + [pad], axis=0),
                      mk_ref[i]) * MEM_HEAD_DIM ** -0.5 for i in samples]
    probs = []
    for s in scores:
        s = jnp.where(own_head, s, -jnp.inf)
        p = jnp.exp(s - jnp.max(s, axis=-1, keepdims=True))
        probs.append(p / jnp.sum(p, axis=-1, keepdims=True))
    for i, p in zip(samples, probs):
        o = _dot(p, mv_ref[i])
        o_ref[i:i + 1, :] = jnp.concatenate([o[h:h + 1] for h in range(MEM_HEADS)], axis=1).astype(o_ref.dtype)


def _route_kernel(o_ref, h1_ref, *rest):
    _route(o_ref[...], h1_ref[...], *rest)


def _route_specs(T, tb, w):
    consts = [w["w_mo"], w["g_ffn"], w["w_router_t"], w["b_router"]]
    row = lambda n: pl.BlockSpec((tb, n), lambda i: (i, 0))
    per_choice = pl.BlockSpec((TOP_K, tb), lambda i: (0, i))
    out_specs = [row(D_MODEL), row(D_MODEL), per_choice, per_choice, per_choice,
                 pl.BlockSpec((None, N_EXPERTS, 1), lambda i: (i, 0, 0))]
    out_shape = [jax.ShapeDtypeStruct((T, D_MODEL), F32), jax.ShapeDtypeStruct((T, D_MODEL), BF16),
                 jax.ShapeDtypeStruct((TOP_K, T), jnp.int32), jax.ShapeDtypeStruct((TOP_K, T), jnp.int32),
                 jax.ShapeDtypeStruct((TOP_K, T), F32), jax.ShapeDtypeStruct((T // tb, N_EXPERTS, 1), jnp.int32)]
    return consts, out_specs, out_shape


def _cross_prompt(qx, h1, mem_k, mem_v, w, tb):
    T = qx.shape[0]
    blocks_per_mem = T // mem_k.shape[0] // tb
    consts, out_specs, out_shape = _route_specs(T, tb, w)
    row = lambda n: pl.BlockSpec((tb, n), lambda i: (i, 0))
    mem_spec = pl.BlockSpec((None, N_MEM, MEM_WIDTH), lambda i: (i // blocks_per_mem, 0, 0))
    return pl.pallas_call(
        _cross_prompt_kernel,
        grid=(T // tb,),
        in_specs=[row(MEM_WIDTH), row(D_MODEL), mem_spec, mem_spec] + [_full(c.shape) for c in consts],
        out_specs=out_specs,
        out_shape=out_shape,
        compiler_params=_params(1),
        name="cross_prompt",
    )(qx, h1, mem_k, mem_v, *consts)


def _cross_sample(qx, h1, mem_k, mem_v, w, nb=SAMPLES_PER_STEP):
    T = qx.shape[0]
    mem_spec = pl.BlockSpec((nb, N_MEM * MEM_HEADS, MEM_HEAD_DIM), lambda i: (i, 0, 0))
    rows = pl.BlockSpec((nb, MEM_WIDTH), lambda i: (i, 0))
    o = pl.pallas_call(
        _cross_sample_kernel,
        grid=(T // nb,),
        in_specs=[rows, mem_spec, mem_spec],
        out_specs=rows,
        out_shape=jax.ShapeDtypeStruct((T, MEM_WIDTH), BF16),
        compiler_params=_params(1),
        name="cross_sample",
    )(qx, mem_k, mem_v)
    consts, out_specs, out_shape = _route_specs(T, T, w)
    return pl.pallas_call(
        _route_kernel,
        grid=(1,),
        in_specs=[_full(o.shape), _full(h1.shape)] + [_full(c.shape) for c in consts],
        out_specs=out_specs,
        out_shape=out_shape,
        compiler_params=_params(1),
        name="route_sample",
    )(o, h1, *consts)


SEG_ALIGN = 16
SEG_LOOP = 8 * SEG_ALIGN
SEL_CHUNK = 256
MOE_BLOCK = 512


def _block_rows(tb):
    worst = TOP_K * tb + N_EXPERTS * (SEG_ALIGN - 1)
    return -(-worst // SEL_CHUNK) * SEL_CHUNK


def _seg_sizes(limit):
    sizes, b = [], SEG_ALIGN
    while b <= limit:
        sizes.append(b)
        b *= 2
    return sizes[::-1]


def _for_segments(n, sizes, make_copy, act):
    if sizes and sizes[0] >= SEG_LOOP:
        def body(i, carry):
            act(make_copy(pl.multiple_of(i * SEG_LOOP, SEG_LOOP), SEG_LOOP))
            return carry
        lax.fori_loop(0, n // SEG_LOOP, body, 0)
    for size in sizes:
        if size < SEG_LOOP:
            @pl.when((n & size) != 0)
            def _():
                act(make_copy(pl.multiple_of(n & (-2 * size), SEG_ALIGN), size))


def _slot_rows(idx, rank, loc_ref, j):
    pos = rank
    for e in range(N_EXPERTS):
        pos = pos + jnp.where(idx == e, loc_ref[j, e], 0)
    return pos


def _dispatch_kernel(loc_ref, pad_ref, goff_ref, total_ref, gap_off_ref, gap_len_ref, x_ref, x_last_ref, idx_ref, rank_ref,
                     xs_hbm, buf, sem, *, tm):
    j = pl.program_id(0)
    last = pl.num_programs(0) - 1
    slot = j % 2
    tb = x_ref.shape[0]
    r_blk = _block_rows(tb)
    pos = _slot_rows(idx_ref[...], rank_ref[...], loc_ref, j)
    x = jnp.where(j == last, x_last_ref[...], x_ref[...])
    for c in range(r_blk // SEL_CHUNK):
        r = lax.broadcasted_iota(jnp.int32, (SEL_CHUNK, tb), 0) + c * SEL_CHUNK
        sel = jnp.where(r == pos[0:1], 1.0, jnp.where(r == pos[1:2], 1.0, jnp.where(
            r == pos[2:3], 1.0, jnp.where(r == pos[3:4], 1.0, 0.0))))
        buf[slot, c * SEL_CHUNK:(c + 1) * SEL_CHUNK, :] = jnp.dot(
            sel.astype(BF16), x, preferred_element_type=F32).astype(BF16)

    sizes = _seg_sizes(tb)

    def start_segments(e, carry):
        src, dst = loc_ref[j, e], goff_ref[j, e]
        _for_segments(pad_ref[j, e], sizes, lambda off, size: pltpu.make_async_copy(
            buf.at[slot, pl.ds(pl.multiple_of(src + off, SEG_ALIGN), size)],
            xs_hbm.at[pl.ds(pl.multiple_of(dst + off, SEG_ALIGN), size)], sem.at[slot]), lambda cp: cp.start())
        return carry

    lax.fori_loop(0, N_EXPERTS, start_segments, 0)

    def wait_block(blk, slot):
        _for_segments(total_ref[blk], _seg_sizes(r_blk), lambda off, size: pltpu.make_async_copy(
            buf.at[slot, pl.ds(0, size)], xs_hbm.at[pl.ds(0, size)], sem.at[slot]), lambda cp: cp.wait())

    @pl.when(j > 0)
    def _():
        wait_block(j - 1, 1 - slot)

    @pl.when(j == last)
    def _():
        wait_block(j, slot)
        buf[1 - slot, 0:tm, :] = jnp.zeros((tm, D_MODEL), BF16)
        gap_sizes = _seg_sizes(tm - SEG_ALIGN)

        def for_gaps(act):
            def body(e, carry):
                dst = gap_off_ref[e]
                _for_segments(gap_len_ref[e], gap_sizes, lambda off, size: pltpu.make_async_copy(
                    buf.at[1 - slot, pl.ds(0, size)], xs_hbm.at[pl.ds(pl.multiple_of(dst + off, SEG_ALIGN), size)],
                    sem.at[1 - slot]), act)
                return carry
            lax.fori_loop(0, N_EXPERTS, body, 0)

        for_gaps(lambda cp: cp.start())
        for_gaps(lambda cp: cp.wait())


def _ffn_kernel(tile_expert_ref, n_valid_ref, x_ref, wgu_f32_ref, bgu_ref, wd_f32_ref, bd_ref, o_ref, wgu_ref, wd_ref):
    t = pl.program_id(0)

    @pl.when(t < n_valid_ref[0])
    def _():
        @pl.when((t == 0) | (tile_expert_ref[t] != tile_expert_ref[jnp.maximum(t - 1, 0)]))
        def _():
            wgu_ref[...] = wgu_f32_ref[...].astype(BF16)
            wd_ref[...] = wd_f32_ref[...].astype(BF16)

        x = x_ref[...]
        bgu = bgu_ref[...]
        gate = jnp.dot(x, wgu_ref[:, :D_FF], preferred_element_type=F32) + bgu[:, :D_FF]
        up = jnp.dot(x, wgu_ref[:, D_FF:], preferred_element_type=F32) + bgu[:, D_FF:]
        gate = jnp.minimum(gate, SWIGLU_LIMIT)
        up = jnp.clip(up, -SWIGLU_LIMIT, SWIGLU_LIMIT)
        hidden = (up + 1.0) * gate * jax.nn.sigmoid(SWIGLU_ALPHA * gate)
        o_ref[...] = (_dot(hidden, wd_ref[...]) + bd_ref[...]).astype(o_ref.dtype)


def _combine_kernel(loc_ref, pad_ref, goff_ref, total_ref, idx0_ref, rank0_ref, wt0_ref, idx_ref, rank_ref, wt_ref,
                    h_ref, h_last_ref, g_final_ref, o_hbm, y_ref, y_last_ref, buf, sel_ref, sem):
    j = pl.program_id(0)
    last = pl.num_programs(0) - 1
    slot = j % 2
    tb = h_ref.shape[0]
    r_blk = buf.shape[1]
    sizes = _seg_sizes(tb)

    def fetch(blk, slot):
        def body(e, carry):
            dst, src = loc_ref[blk, e], goff_ref[blk, e]
            _for_segments(pad_ref[blk, e], sizes, lambda off, size: pltpu.make_async_copy(
                o_hbm.at[pl.ds(pl.multiple_of(src + off, SEG_ALIGN), size)],
                buf.at[slot, pl.ds(pl.multiple_of(dst + off, SEG_ALIGN), size)], sem.at[slot]), lambda cp: cp.start())
            return carry
        lax.fori_loop(0, N_EXPERTS, body, 0)

    def build(blk, slot, idx, rank, wt):
        pos = _slot_rows(idx, rank, loc_ref, blk)
        for c in range(r_blk // SEL_CHUNK):
            r = lax.broadcasted_iota(jnp.int32, (SEL_CHUNK, tb), 0) + c * SEL_CHUNK
            sel = jnp.zeros((SEL_CHUNK, tb), F32)
            for k in range(TOP_K):
                sel = jnp.where(r == pos[k:k + 1], wt[k:k + 1], sel)
            sel_ref[slot, c * SEL_CHUNK:(c + 1) * SEL_CHUNK, :] = sel.astype(BF16)

    @pl.when(j == 0)
    def _():
        buf[...] = jnp.zeros_like(buf)
        fetch(0, 0)
        build(0, 0, idx0_ref[...], rank0_ref[...], wt0_ref[...])

    @pl.when(j < last)
    def _():
        fetch(j + 1, 1 - slot)

    _for_segments(total_ref[j], _seg_sizes(r_blk), lambda off, size: pltpu.make_async_copy(
        o_hbm.at[pl.ds(0, size)], buf.at[slot, pl.ds(0, size)], sem.at[slot]), lambda cp: cp.wait())
    moe = _dot_tn(sel_ref[slot], buf[slot])
    y = _rms(jnp.where(j == last, h_last_ref[...], h_ref[...]) + moe, g_final_ref[...])
    build(jnp.minimum(j + 1, last), 1 - slot, idx_ref[...], rank_ref[...], wt_ref[...])

    @pl.when(j < last)
    def _():
        y_ref[...] = y

    @pl.when(j == last)
    def _():
        y_last_ref[...] = y


def _moe(xn, xn_last, idx, rank, wt, cnt, h2, h2_last, w, tb, tm):
    T = xn.shape[0]
    nb = T // tb + 1
    r_blk = _block_rows(tb)
    i32 = jnp.int32
    pad = (cnt + (SEG_ALIGN - 1)) // SEG_ALIGN * SEG_ALIGN
    loc = jnp.cumsum(pad, axis=1) - pad
    seg = jnp.sum(pad, axis=0)
    region = (seg + (tm - 1)) // tm * tm
    region_end = jnp.cumsum(region)
    region_start = region_end - region
    goff = region_start[None, :] + jnp.cumsum(pad, axis=0) - pad
    n_tiles = -(-(nb * (TOP_K * tb + N_EXPERTS * (SEG_ALIGN - 1)) + N_EXPERTS * (tm - 1)) // tm)
    n_valid = (region_end[-1] // tm).astype(i32).reshape(1)
    tile = jnp.minimum(jnp.arange(n_tiles, dtype=i32), n_valid - 1)
    tile_expert = jnp.minimum(jnp.sum((region_end[None, :] <= (tile * tm)[:, None]).astype(i32), axis=1), N_EXPERTS - 1)
    plan = [a.astype(i32) for a in (loc, pad, goff, jnp.sum(pad, axis=1))]
    gaps = [(region_start + seg).astype(i32), (region - seg).astype(i32)]

    choice = pl.BlockSpec((TOP_K, tb), lambda j, *_: (0, j))
    body_rows = pl.BlockSpec((tb, D_MODEL), lambda j, *_: (jnp.minimum(j, nb - 2), 0))
    last_rows = pl.BlockSpec((tb, D_MODEL), lambda j, *_: (0, 0))
    xs = pl.pallas_call(
        functools.partial(_dispatch_kernel, tm=tm),
        grid_spec=pltpu.PrefetchScalarGridSpec(
            num_scalar_prefetch=6, grid=(nb,),
            in_specs=[body_rows, last_rows, choice, choice],
            out_specs=pl.BlockSpec(memory_space=pl.ANY),
            scratch_shapes=[pltpu.VMEM((2, r_blk, D_MODEL), BF16), pltpu.SemaphoreType.DMA((2,))]),
        out_shape=jax.ShapeDtypeStruct((n_tiles * tm, D_MODEL), BF16),
        compiler_params=_params(1),
        name="moe_dispatch",
    )(*plan, *gaps, xn, xn_last, idx, rank)

    rows = pl.BlockSpec((tm, D_MODEL), lambda t, te, nv: (jnp.minimum(t, nv[0] - 1), 0))
    per_e = lambda a, b: pl.BlockSpec((None, a, b), lambda t, te, nv: (te[t], 0, 0))
    out = pl.pallas_call(
        _ffn_kernel,
        grid_spec=pltpu.PrefetchScalarGridSpec(
            num_scalar_prefetch=2, grid=(n_tiles,),
            in_specs=[rows, per_e(D_MODEL, 2 * D_FF), per_e(1, 2 * D_FF), per_e(D_FF, D_MODEL), per_e(1, D_MODEL)],
            out_specs=rows,
            scratch_shapes=[pltpu.VMEM((D_MODEL, 2 * D_FF), BF16), pltpu.VMEM((D_FF, D_MODEL), BF16)]),
        out_shape=jax.ShapeDtypeStruct((n_tiles * tm, D_MODEL), BF16),
        compiler_params=_params(1),
        name="moe_ffn",
    )(tile_expert, n_valid, xs, w["w_gate_up"], w["b_gate_up"], w["w_down"], w["b_down"])

    first = pl.BlockSpec((TOP_K, tb), lambda j, *_: (0, 0))
    ahead = pl.BlockSpec((TOP_K, tb), lambda j, *_: (0, jnp.minimum(j + 1, nb - 1)))
    return pl.pallas_call(
        _combine_kernel,
        grid_spec=pltpu.PrefetchScalarGridSpec(
            num_scalar_prefetch=4, grid=(nb,),
            in_specs=[first, first, first, ahead, ahead, ahead, body_rows, last_rows,
                      pl.BlockSpec((1, D_MODEL), lambda j, *_: (0, 0)), pl.BlockSpec(memory_space=pl.ANY)],
            out_specs=[body_rows, last_rows],
            scratch_shapes=[pltpu.VMEM((2, r_blk, D_MODEL), BF16), pltpu.VMEM((2, r_blk, tb), BF16),
                            pltpu.SemaphoreType.DMA((2,))]),
        out_shape=[jax.ShapeDtypeStruct((T, D_MODEL), F32), jax.ShapeDtypeStruct((tb, D_MODEL), F32)],
        compiler_params=_params(1),
        name="moe_combine",
    )(*plan, idx, rank, wt, idx, rank, wt, h2, h2_last, w["g_final"], out)


def _rope_table(pos):
    half = QK_ROPE // 2
    inv = ROPE_THETA ** (-jnp.arange(half, dtype=F32) * 2.0 / QK_ROPE)
    ang = pos[:, None] * inv[None, :]
    cos, sin, zero = jnp.cos(ang), jnp.sin(ang), jnp.zeros_like(ang)
    return jnp.concatenate([cos, cos, cos, cos, -sin, zero, -sin, zero, zero, sin, zero, sin], axis=1)


def _prepare_weights(g_attn, w_in, g_q, g_kv, w_uq, w_uk, w_uv, w_a2, b_a, g_gla, w_o_mla, w_o_gla, w_out,
                     g_mem, w_mk, w_mv, g_cross, w_mq, w_mo, g_ffn, w_router, b_router, w_gate_up, b_gate_up,
                     w_down, b_down, g_final):
    splits = np.cumsum((0,) + IN_SPLITS)
    parts = [w_in[0][:, splits[i]:splits[i + 1]] for i in range(len(IN_SPLITS))]
    parts = [jnp.pad(p, ((0, 0), (0, wd - p.shape[1]))) for p, wd in zip(parts, _PACKED_WIDTHS)]
    wq = w_uq[0].reshape(Q_LORA, MLA_HEADS, QK_NOPE + QK_ROPE)
    wq_rope = jnp.pad(wq[:, :, QK_NOPE:], ((0, 0), (0, 0), (0, LANES - QK_ROPE)))
    row = lambda v: v.reshape(1, -1)
    return dict(
        g_attn=row(g_attn[0]), w_in=jnp.concatenate(parts, axis=1).astype(BF16), g_q=row(g_q[0]), g_kv=row(g_kv[0]),
        w_uq=jnp.concatenate([_fold_latent_query(jnp.transpose(wq[:, :, :QK_NOPE], (1, 0, 2)),
                                                 jnp.transpose(w_uk[0], (1, 0, 2))),
                              wq_rope.reshape(Q_LORA, -1).astype(BF16)], axis=1),
        wuv=jnp.transpose(w_uv[0], (1, 0, 2)).astype(BF16),
        w_a2=jnp.pad(w_a2[0], ((0, LANES - GLA_RANK), (0, 0))).astype(BF16), b_a=row(b_a[0]),
        g_gla=row(g_gla[0]), w_o_mla=w_o_mla[0].astype(BF16), w_o_gla=w_o_gla[0].astype(BF16),
        w_out=w_out[0].astype(BF16), g_mem=row(g_mem[0]),
        w_mkv=jnp.concatenate([w_mk[0], w_mv[0]], axis=1).astype(BF16),
        g_cross=row(g_cross[0]), w_mq=w_mq[0].astype(BF16), w_mo=w_mo[0].astype(BF16), g_ffn=row(g_ffn[0]),
        w_router_t=w_router[0].T, b_router=b_router[0].reshape(-1, 1),
        w_gate_up=w_gate_up[0], b_gate_up=b_gate_up[0].reshape(N_EXPERTS, 1, 2 * D_FF),
        w_down=w_down[0], b_down=b_down[0].reshape(N_EXPERTS, 1, D_MODEL), g_final=row(g_final),
    )


def kernel(x_prompt, x_sample, cache_kv_latent, cache_k_rope, state_gla, cache_mem_k, cache_mem_v, page_table, mem_prompt, g_attn, w_in, g_q, g_kv, w_uq, w_uk, w_uv, w_a2, b_a, g_gla, w_o_mla, w_o_gla, w_out, g_mem, w_mk, w_mv, g_cross, w_mq, w_mo, g_ffn, w_router, b_router, w_gate_up, b_gate_up, w_down, b_down, g_final):
    w = _prepare_weights(g_attn, w_in, g_q, g_kv, w_uq, w_uk, w_uv, w_a2, b_a, g_gla, w_o_mla, w_o_gla, w_out,
                         g_mem, w_mk, w_mv, g_cross, w_mq, w_mo, g_ffn, w_router, b_router, w_gate_up, b_gate_up,
                         w_down, b_down, g_final)
    B, S, D = x_prompt.shape
    Bd = x_sample.shape[0]
    past_len = page_table.shape[1] * cache_kv_latent.shape[2]

    tm = min(MLA_BLOCK, S)
    xp = x_prompt.reshape(B * S, D)
    (qcat, ckv, krope, kcat, kcat_t, gq, gk, gv, sg, la, szm, szg) = _premix(
        xp, _rope_table(jnp.arange(S, dtype=F32)), w, min(PREMIX_ROWS, S), tm, BF16)
    olat = _mla_prompt(qcat.reshape(MLA_HEADS, B, S, QCAT), kcat.reshape(B, S, QCAT), kcat_t, tm)
    og, state_p = _gla_prompt(gq.reshape(B, S, -1), gk.reshape(B, S, -1), gv.reshape(B, S, -1), la.reshape(B, S, -1))
    h1, qx = _postmix(xp, olat.reshape(B * S, -1), og.reshape(B * S, -1), sg, szm, szg, w, min(POSTMIX_ROWS, S))
    mem_k, mem_v = _memkv(mem_prompt.reshape(B * N_MEM, D), w["g_mem"], w["w_mkv"], min(MEMKV_ROWS, B * N_MEM))
    tb = min(MOE_BLOCK, S)
    h2, xn2, idx, rank, wt, cnt = _cross_prompt(qx, h1, mem_k.reshape(B, N_MEM, MEM_WIDTH),
                                                mem_v.reshape(B, N_MEM, MEM_WIDTH), w, tb)

    xs = x_sample.reshape(Bd, D)
    (qcat_s, ckv_s, krope_s, kcat_s, _, gq_s, gk_s, gv_s, sg_s, la_s, szm_s, szg_s) = _premix(
        xs, _rope_table(jnp.full((Bd,), past_len, F32)), w, Bd, Bd, F32)
    olat_s = _mla_sample(page_table, qcat_s, kcat_s, cache_kv_latent[0], jnp.swapaxes(cache_k_rope[0], 1, 2))
    og_s, state_s = _gla_sample(gq_s, gk_s, gv_s, la_s, state_gla[0])
    h1_s, qx_s = _postmix(xs, olat_s.reshape(Bd, -1), og_s, sg_s, szm_s, szg_s, w, Bd)
    h2_s, xn2_s, idx_s, rank_s, wt_s, cnt_s = _cross_sample(
        qx_s, h1_s, cache_mem_k.reshape(Bd, N_MEM * MEM_HEADS, MEM_HEAD_DIM),
        cache_mem_v.reshape(Bd, N_MEM * MEM_HEADS, MEM_HEAD_DIM), w)

    fill = lambda a, v: jnp.pad(a, ((0, 0), (0, tb - Bd)), constant_values=v)
    rows = lambda a: jnp.pad(a, ((0, tb - Bd), (0, 0)))
    y_prompt, y_last = _moe(
        xn2, rows(xn2_s), jnp.concatenate([idx, fill(idx_s, N_EXPERTS)], axis=1),
        jnp.concatenate([rank, fill(rank_s, -1)], axis=1), jnp.concatenate([wt, fill(wt_s, 0.0)], axis=1),
        jnp.concatenate([cnt.reshape(-1, N_EXPERTS), cnt_s.reshape(-1, N_EXPERTS)], axis=0), h2, rows(h2_s), w, tb, tb)
    y_sample = y_last[:Bd]

    return (y_prompt.reshape(B, S, D), y_sample.reshape(Bd, 1, D),
            ckv.reshape(1, B, S, KV_LORA), krope.reshape(1, B, S, QK_ROPE), state_p[None],
            mem_k.reshape(1, B, N_MEM, MEM_HEADS, MEM_HEAD_DIM), mem_v.reshape(1, B, N_MEM, MEM_HEADS, MEM_HEAD_DIM),
            ckv_s.reshape(1, Bd, 1, KV_LORA), krope_s.reshape(1, Bd, 1, QK_ROPE), state_s[None])
```

```python
import functools

import jax
import jax.numpy as jnp
import numpy as np
from jax import lax
from jax.experimental import pallas as pl
from jax.experimental.pallas import tpu as pltpu

F32 = jnp.float32
BF16 = jnp.bfloat16

D_MODEL = 1024
MLA_HEADS = 8
Q_LORA = 256
KV_LORA = 128
QK_NOPE = 128
QK_ROPE = 64
ROPE_THETA = 10000.0
ATTN_SCALE = (QK_NOPE + QK_ROPE) ** -0.5
GLA_HEADS = 4
GLA_DK = 128
GLA_DV = 256
GLA_RANK = 16
GLA_TAU = 16.0
GLA_CHUNK = 64
GLA_GROUP = 4
N_MEM = 256
MEM_HEADS = 4
MEM_HEAD_DIM = 128
N_EXPERTS = 32
TOP_K = 4
D_FF = 1024
SWIGLU_LIMIT = 7.0
SWIGLU_ALPHA = 1.702
EPS = 1e-6

GLA_K_WIDTH = GLA_HEADS * GLA_DK
GLA_V_WIDTH = GLA_HEADS * GLA_DV
MEM_WIDTH = MEM_HEADS * MEM_HEAD_DIM
IN_SPLITS = (Q_LORA, KV_LORA, QK_ROPE, GLA_K_WIDTH, GLA_K_WIDTH, GLA_V_WIDTH, GLA_V_WIDTH, GLA_RANK, D_MODEL, D_MODEL)

LANES = 128
SUBLANES = 8
_PACKED_WIDTHS = (Q_LORA, KV_LORA, LANES, GLA_K_WIDTH, GLA_K_WIDTH, GLA_V_WIDTH, GLA_V_WIDTH, LANES, D_MODEL, D_MODEL)
_OFF = tuple(int(v) for v in np.cumsum((0,) + _PACKED_WIDTHS))
QCAT = 2 * LANES

VMEM_LIMIT = 56 << 20
PREMIX_ROWS = 512
POSTMIX_ROWS = 512
MLA_BLOCK = 256
MEMKV_ROWS = 512
SAMPLES_PER_STEP = 8


def _params(n_axes):
    return pltpu.CompilerParams(dimension_semantics=("arbitrary",) * n_axes, vmem_limit_bytes=VMEM_LIMIT)


def _rms(x, g):
    var = jnp.mean(x * x, axis=-1, keepdims=True)
    return x * lax.rsqrt(var + EPS) * g


def _dot(a, b):
    return jnp.dot(a.astype(BF16), b.astype(BF16), preferred_element_type=F32)


def _dot_nt(a, b):
    return lax.dot_general(a.astype(BF16), b.astype(BF16), (((1,), (1,)), ((), ())), preferred_element_type=F32)


def _dot_tn(a, b):
    return lax.dot_general(a.astype(BF16), b.astype(BF16), (((0,), (0,)), ((), ())), preferred_element_type=F32)


def _split3(x):
    hi = x.astype(BF16)
    r1 = x - hi.astype(F32)
    mid = r1.astype(BF16)
    lo = (r1 - mid.astype(F32)).astype(BF16)
    return hi, mid, lo


def _dot_f32_nt(a, b):
    a0, a1, a2 = _split3(a)
    b0, b1, b2 = _split3(b)
    d = lambda u, v: lax.dot_general(u, v, (((1,), (1,)), ((), ())), preferred_element_type=F32)
    return ((d(a1, b1) + d(a0, b2) + d(a2, b0)) + (d(a0, b1) + d(a1, b0))) + d(a0, b0)


def _dot_hi_nt(a, b):
    a0 = a.astype(BF16)
    a1 = (a - a0.astype(F32)).astype(BF16)
    b0 = b.astype(BF16)
    b1 = (b - b0.astype(F32)).astype(BF16)
    d = lambda u, v: lax.dot_general(u, v, (((1,), (1,)), ((), ())), preferred_element_type=F32)
    return (d(a0, b1) + d(a1, b0)) + d(a0, b0)


def _full(shape, buffers=None):
    n = len(shape)
    return pl.BlockSpec(shape, lambda *_: (0,) * n, pipeline_mode=pl.Buffered(buffers) if buffers else None)


def _premix_kernel(x_ref, tab_ref, g_attn_ref, w_in_ref, g_q_ref, g_kv_ref, w_uq_ref, w_a2_ref, b_a_ref,
                   qcat_ref, ckv_ref, krope_ref, kcat_ref, kcat_t_ref, gq_ref, gk_ref, gv_ref, sg_ref, la_ref, szm_ref, szg_ref):
    xn = _rms(x_ref[...], g_attn_ref[...]).astype(BF16)

    def proj(lo, hi):
        y = jnp.dot(xn, w_in_ref[:, _OFF[lo]:_OFF[hi]], preferred_element_type=F32)
        return [y[:, _OFF[i] - _OFF[lo]:_OFF[i + 1] - _OFF[lo]] for i in range(lo, hi)]

    tab = tab_ref[...]
    cos, sin_lo, sin_hi = tab[:, :LANES], tab[:, LANES:2 * LANES], tab[:, 2 * LANES:]

    def rope(t):
        return t * cos + pltpu.roll(t, LANES - QK_ROPE // 2, 1) * sin_lo + pltpu.roll(t, QK_ROPE // 2, 1) * sin_hi

    cq, ckv, kr = proj(0, 3)
    c_q = _rms(cq, g_q_ref[...]).astype(BF16)
    ckv = _rms(ckv, g_kv_ref[...])
    kr = rope(kr)
    ckv_ref[...] = ckv
    krope_ref[...] = kr[:, :QK_ROPE]
    kcat = jnp.concatenate([ckv, kr], axis=1)
    kcat_ref[...] = kcat.astype(BF16)
    tk = kcat_t_ref.shape[2]
    for t in range(kcat_t_ref.shape[0]):
        kcat_t_ref[t] = kcat[t * tk:(t + 1) * tk].T.astype(BF16)
    q_all = jnp.dot(c_q, w_uq_ref[...], preferred_element_type=F32)
    for h in range(MLA_HEADS):
        qcat_ref[h, :, :LANES] = q_all[:, h * LANES:(h + 1) * LANES].astype(qcat_ref.dtype)
        qcat_ref[h, :, LANES:] = rope(q_all[:, (MLA_HEADS + h) * LANES:(MLA_HEADS + h + 1) * LANES]).astype(qcat_ref.dtype)
    gq, gk = proj(3, 5)
    gq_ref[...] = (gq * GLA_DK ** -0.5).astype(gq_ref.dtype)
    gk_ref[...] = gk.astype(gk_ref.dtype)
    gv_ref[...] = proj(5, 6)[0].astype(gv_ref.dtype)
    gg, ga = proj(6, 8)
    sg_ref[...] = (gg * jax.nn.sigmoid(gg)).astype(sg_ref.dtype)
    z = _dot(ga, w_a2_ref[...]) + b_a_ref[...]
    la_ref[...] = -(jnp.maximum(-z, 0.0) + jnp.log1p(jnp.exp(-jnp.abs(z)))) * (1.0 / GLA_TAU)
    szm_ref[...] = jax.nn.sigmoid(proj(8, 9)[0]).astype(szm_ref.dtype)
    szg_ref[...] = jax.nn.sigmoid(proj(9, 10)[0]).astype(szg_ref.dtype)


def _fold_kernel(wq_ref, wk_ref, o_ref):
    o_ref[...] = _dot_f32_nt(wq_ref[...], wk_ref[...]).astype(o_ref.dtype)


def _fold_latent_query(w_uq_nope, w_uk):
    return pl.pallas_call(
        _fold_kernel,
        grid=(MLA_HEADS,),
        in_specs=[pl.BlockSpec((None, Q_LORA, QK_NOPE), lambda h: (h, 0, 0)),
                  pl.BlockSpec((None, KV_LORA, QK_NOPE), lambda h: (h, 0, 0))],
        out_specs=pl.BlockSpec((Q_LORA, KV_LORA), lambda h: (0, h)),
        out_shape=jax.ShapeDtypeStruct((Q_LORA, MLA_HEADS * KV_LORA), BF16),
        compiler_params=_params(1),
        name="fold_latent_query",
    )(w_uq_nope, w_uk)


def _premix(x, tab, w, tm, tk, gdtype):
    T = x.shape[0]
    n_tab = tab.shape[0] // tm
    row = lambda n: pl.BlockSpec((tm, n), lambda i: (i, 0))
    outs = [(KV_LORA, F32), (QK_ROPE, F32), (QCAT, BF16), None, (GLA_K_WIDTH, gdtype),
            (GLA_K_WIDTH, gdtype), (GLA_V_WIDTH, gdtype), (GLA_V_WIDTH, BF16), (GLA_K_WIDTH, F32),
            (D_MODEL, BF16), (D_MODEL, BF16)]
    out_specs = [row(o[0]) if o else pl.BlockSpec((tm // tk, QCAT, tk), lambda i: (i, 0, 0)) for o in outs]
    out_shape = [jax.ShapeDtypeStruct((T, o[0]) if o else (T // tk, QCAT, tk), o[1] if o else BF16) for o in outs]
    consts = [w["g_attn"], w["w_in"], w["g_q"], w["g_kv"], w["w_uq"], w["w_a2"], w["b_a"]]
    const_specs = [pl.BlockSpec(c.shape, lambda i, n=len(c.shape): (0,) * n,
                                pipeline_mode=pl.Buffered(1) if c is w["w_in"] else None) for c in consts]
    return pl.pallas_call(
        _premix_kernel,
        grid=(T // tm,),
        in_specs=[row(D_MODEL), pl.BlockSpec((tm, 3 * LANES), lambda i: (i % n_tab, 0))] + const_specs,
        out_specs=[pl.BlockSpec((MLA_HEADS, tm, QCAT), lambda i: (0, i, 0))] + out_specs,
        out_shape=[jax.ShapeDtypeStruct((MLA_HEADS, T, QCAT), BF16)] + out_shape,
        compiler_params=_params(1),
        name="premix",
    )(x, tab, *consts)


def _mla_prompt_kernel(q_ref, k_ref, kt_ref, o_ref, m_ref, l_ref, acc_ref):
    tq = q_ref.shape[1]
    qi = pl.program_id(1)
    q = q_ref[...].reshape(MLA_HEADS * tq, QCAT)
    row = lax.broadcasted_iota(jnp.int32, (MLA_HEADS, tq, tq), 1).reshape(MLA_HEADS * tq, tq)
    col = lax.broadcasted_iota(jnp.int32, (MLA_HEADS * tq, tq), 1)
    m_ref[...] = jnp.full_like(m_ref, -jnp.inf)
    l_ref[...] = jnp.zeros_like(l_ref)
    acc_ref[...] = jnp.zeros_like(acc_ref)
    c = ATTN_SCALE * np.log2(np.e)

    def step(j, n, mask_last):
        keys = pl.ds(pl.multiple_of(j * tq, tq), n * tq)
        parts = [jnp.dot(q, kt_ref[j + t], preferred_element_type=F32) for t in range(n)]
        if mask_last:
            parts[-1] = jnp.where(col <= row, parts[-1], -jnp.inf)
        s = jnp.concatenate(parts, axis=1)
        m = m_ref[...]
        m_new = jnp.maximum(m, jnp.max(s, axis=-1, keepdims=True))
        alpha = jnp.exp2((m - m_new) * c)
        p = jnp.exp2((s - jnp.concatenate([m_new] * (n * tq // LANES), axis=1)) * c)
        p_lanes = p[:, :LANES]
        for t in range(1, n * tq // LANES):
            p_lanes = p_lanes + p[:, t * LANES:(t + 1) * LANES]
        l_ref[...] = alpha * l_ref[...] + p_lanes
        acc_ref[...] = alpha * acc_ref[...] + _dot(p, k_ref[keys, :KV_LORA])
        m_ref[...] = m_new

    def body(jj, carry):
        step(2 * jj, 2, False)
        return carry

    lax.fori_loop(0, qi // 2, body, 0)

    @pl.when(qi % 2 == 0)
    def _():
        step(qi, 1, True)

    @pl.when(qi % 2 == 1)
    def _():
        step(qi - 1, 2, True)
    o = acc_ref[...] / jnp.sum(l_ref[...], axis=-1, keepdims=True)
    for h in range(MLA_HEADS):
        o_ref[:, h * KV_LORA:(h + 1) * KV_LORA] = o[h * tq:(h + 1) * tq].astype(o_ref.dtype)


def _mla_prompt(qcat, kcat, kcat_t, tq):
    _, B, S, _ = qcat.shape
    rows = MLA_HEADS * tq
    return pl.pallas_call(
        _mla_prompt_kernel,
        grid=(B, S // tq),
        in_specs=[pl.BlockSpec((MLA_HEADS, None, tq, QCAT), lambda b, i: (0, b, i, 0)),
                  pl.BlockSpec((None, S, QCAT), lambda b, i: (b, 0, 0)),
                  pl.BlockSpec((S // tq, QCAT, tq), lambda b, i: (b, 0, 0))],
        out_specs=pl.BlockSpec((None, tq, MLA_HEADS * KV_LORA), lambda b, i: (b, i, 0)),
        out_shape=jax.ShapeDtypeStruct((B, S, MLA_HEADS * KV_LORA), BF16),
        scratch_shapes=[pltpu.VMEM((rows, LANES), F32), pltpu.VMEM((rows, LANES), F32), pltpu.VMEM((rows, KV_LORA), F32)],
        compiler_params=_params(2),
        name="mla_prompt",
    )(qcat, kcat, kcat_t)


def _mla_sample_kernel(pt_ref, q_ref, knew_ref, lat_hbm, rope_hbm, o_ref, lat_buf, rope_buf, lat_bf, rope_bf, sems):
    b = pl.program_id(0)
    n_pages, page = lat_buf.shape[1], lat_buf.shape[2]

    def page_copies(sample, slot, i):
        pg = pt_ref[sample, i]
        return (pltpu.make_async_copy(lat_hbm.at[pg], lat_buf.at[slot, i], sems.at[0, slot]),
                pltpu.make_async_copy(rope_hbm.at[pg], rope_buf.at[slot, i], sems.at[1, slot]))

    def start_pages(sample, slot):
        def body(i, carry):
            for cp in page_copies(sample, slot, i):
                cp.start()
            return carry
        lax.fori_loop(0, n_pages, body, 0)

    slot = b % 2

    @pl.when(b == 0)
    def _():
        start_pages(0, 0)

    @pl.when(b + 1 < pl.num_programs(0))
    def _():
        start_pages(b + 1, 1 - slot)

    pltpu.make_async_copy(lat_hbm.at[pl.ds(0, n_pages)], lat_buf.at[slot], sems.at[0, slot]).wait()
    pltpu.make_async_copy(rope_hbm.at[pl.ds(0, n_pages)], rope_buf.at[slot], sems.at[1, slot]).wait()

    lat_bf[...] = lat_buf[slot].reshape(n_pages * page, KV_LORA).astype(BF16)
    for i in range(n_pages):
        rope_bf[:, i * page:(i + 1) * page] = rope_buf[slot, i].astype(BF16)
    q = q_ref[...].reshape(MLA_HEADS, QCAT)
    knew = knew_ref[...].astype(F32)
    s = (_dot_nt(q[:, :KV_LORA], lat_bf[...]) + _dot(q[:, KV_LORA:KV_LORA + QK_ROPE], rope_bf[...])) * ATTN_SCALE
    s_new = jnp.sum(q.astype(F32) * knew, axis=-1, keepdims=True) * ATTN_SCALE
    m = jnp.maximum(jnp.max(s, axis=-1, keepdims=True), s_new)
    p = jnp.exp(s - m)
    p_new = jnp.exp(s_new - m)
    denom = jnp.sum(p, axis=-1, keepdims=True) + p_new
    o = _dot(p, lat_bf[...]) + p_new * knew[:, :KV_LORA]
    o_ref[...] = (o / denom).astype(o_ref.dtype)


def _mla_sample(page_table, qcat, knew, cache_lat, cache_rope_t):
    Bd, n_pages = page_table.shape
    page = cache_lat.shape[1]
    grid_spec = pltpu.PrefetchScalarGridSpec(
        num_scalar_prefetch=1,
        grid=(Bd,),
        in_specs=[pl.BlockSpec((MLA_HEADS, None, 1, QCAT), lambda b, pt: (0, b, 0, 0)),
                  pl.BlockSpec((None, 1, QCAT), lambda b, pt: (b, 0, 0)),
                  pl.BlockSpec(memory_space=pl.ANY), pl.BlockSpec(memory_space=pl.ANY)],
        out_specs=pl.BlockSpec((None, MLA_HEADS, KV_LORA), lambda b, pt: (b, 0, 0)),
        scratch_shapes=[pltpu.VMEM((2, n_pages, page, KV_LORA), F32), pltpu.VMEM((2, n_pages, QK_ROPE, page), F32),
                        pltpu.VMEM((n_pages * page, KV_LORA), BF16), pltpu.VMEM((QK_ROPE, n_pages * page), BF16),
                        pltpu.SemaphoreType.DMA((2, 2))],
    )
    return pl.pallas_call(
        _mla_sample_kernel,
        grid_spec=grid_spec,
        out_shape=jax.ShapeDtypeStruct((Bd, MLA_HEADS, KV_LORA), BF16),
        compiler_params=_params(1),
        name="mla_sample",
    )(page_table, qcat.reshape(MLA_HEADS, Bd, 1, QCAT), knew.reshape(Bd, 1, QCAT), cache_lat, cache_rope_t)


def _gla_prompt_kernel(q_ref, k_ref, v_ref, la_ref, o_ref, state_ref, st_ref):
    C = GLA_CHUNK
    S = q_ref.shape[0]
    row = lax.broadcasted_iota(jnp.int32, (C, 2 * C), 0)
    col = lax.broadcasted_iota(jnp.int32, (C, 2 * C), 1)
    causal = row >= col
    tri = causal[:, :C].astype(BF16)
    pad_k = jnp.zeros((C, GLA_DK), F32)
    pad_v = jnp.zeros((C, GLA_DV), F32)
    st_ref[...] = jnp.zeros_like(st_ref)

    group = min(GLA_GROUP, S // C)

    def chunks(c, carry):
        parts = []
        for g in range(group):
            rows = pl.ds(pl.multiple_of((c * group + g) * C, C), C)
            split = jnp.dot(tri, jnp.concatenate(_split3(la_ref[rows, :]), axis=1), preferred_element_type=F32)
            b_all = (split[:, 2 * GLA_K_WIDTH:] + split[:, GLA_K_WIDTH:2 * GLA_K_WIDTH]) + split[:, :GLA_K_WIDTH]
            for h in range(GLA_HEADS):
                kcols = slice(h * GLA_DK, (h + 1) * GLA_DK)
                vcols = slice(h * GLA_DV, (h + 1) * GLA_DV)
                b = b_all[:, kcols]
                b_last = b[C - 1:C, :]
                q = q_ref[rows, kcols].astype(F32)
                k = k_ref[rows, kcols].astype(F32)
                q_in = (q * jnp.exp(b)).astype(BF16)
                k_in = jnp.concatenate([k * jnp.exp(-b), pad_k], axis=0)
                k_out = jnp.concatenate([k * jnp.exp(b_last - b), pad_k], axis=0)
                a = jnp.where(causal, _dot_nt(q_in, k_in), 0.0)
                v_t = jnp.concatenate([v_ref[rows, vcols].astype(F32), pad_v], axis=0).T.astype(BF16)
                lhs = jnp.concatenate([q_in, a.astype(BF16)], axis=1)
                parts.append((rows, h, vcols, lhs, v_t, jnp.exp(b_last), _dot(v_t, k_out)))
        for rows, h, vcols, lhs, v_t, decay, kv_t in parts:
            st = st_ref[h]
            o_ref[rows, vcols] = _dot_nt(lhs, jnp.concatenate([st.astype(BF16), v_t], axis=1)).astype(o_ref.dtype)
            st_ref[h] = st * decay + kv_t
        return carry

    lax.fori_loop(0, S // C // group, chunks, 0)
    for h in range(GLA_HEADS):
        state_ref[h] = st_ref[h].T


def _gla_prompt(q, k, v, la):
    B, S, _ = q.shape
    seq = lambda n: pl.BlockSpec((None, S, n), lambda b: (b, 0, 0))
    return pl.pallas_call(
        _gla_prompt_kernel,
        grid=(B,),
        in_specs=[seq(GLA_K_WIDTH), seq(GLA_K_WIDTH), seq(GLA_V_WIDTH), seq(GLA_K_WIDTH)],
        out_specs=[seq(GLA_V_WIDTH), pl.BlockSpec((None, GLA_HEADS, GLA_DK, GLA_DV), lambda b: (b, 0, 0, 0))],
        out_shape=[jax.ShapeDtypeStruct((B, S, GLA_V_WIDTH), BF16),
                   jax.ShapeDtypeStruct((B, GLA_HEADS, GLA_DK, GLA_DV), F32)],
        scratch_shapes=[pltpu.VMEM((GLA_HEADS, GLA_DV, GLA_DK), F32)],
        compiler_params=_params(1),
        name="gla_prompt",
    )(q, k, v, la)


def _gla_sample_kernel(q_ref, k_ref, v_ref, la_ref, st_ref, o_ref, sto_ref):
    nb = q_ref.shape[0]
    pad = jnp.zeros((GLA_DK - nb, GLA_DK), F32)
    for h in range(GLA_HEADS):
        kcols = slice(h * GLA_DK, (h + 1) * GLA_DK)
        vcols = slice(h * GLA_DV, (h + 1) * GLA_DV)
        col = lambda ref: jnp.concatenate([ref[:, kcols].astype(F32), pad], axis=0).T
        q_t, k_t, decay_t = col(q_ref), col(k_ref), jnp.exp(col(la_ref))
        for i in range(nb):
            new = decay_t[:, i:i + 1] * st_ref[i, h] + k_t[:, i:i + 1] * v_ref[i:i + 1, vcols].astype(F32)
            sto_ref[i, h] = new
            o_ref[i:i + 1, vcols] = jnp.sum(q_t[:, i:i + 1] * new, axis=0, keepdims=True).astype(o_ref.dtype)


def _gla_sample(q, k, v, la, state, nb=SAMPLES_PER_STEP):
    Bd = q.shape[0]
    rows = lambda n: pl.BlockSpec((nb, n), lambda i: (i, 0))
    st_spec = pl.BlockSpec((nb, GLA_HEADS, GLA_DK, GLA_DV), lambda i: (i, 0, 0, 0))
    return pl.pallas_call(
        _gla_sample_kernel,
        grid=(Bd // nb,),
        in_specs=[rows(GLA_K_WIDTH), rows(GLA_K_WIDTH), rows(GLA_V_WIDTH), rows(GLA_K_WIDTH), st_spec],
        out_specs=[rows(GLA_V_WIDTH), st_spec],
        out_shape=[jax.ShapeDtypeStruct((Bd, GLA_V_WIDTH), BF16), jax.ShapeDtypeStruct(state.shape, F32)],
        compiler_params=_params(1),
        name="gla_sample",
    )(q, k, v, la, state)


def _postmix_kernel(x_ref, olat_ref, og_ref, sg_ref, szm_ref, szg_ref, wuv_ref, w_o_mla_ref, g_gla_ref, w_o_gla_ref,
                    w_out_ref, g_cross_ref, w_mq_ref, h_ref, qx_ref):
    ov = jnp.concatenate(
        [_dot(olat_ref[:, h * KV_LORA:(h + 1) * KV_LORA], wuv_ref[h]).astype(BF16) for h in range(MLA_HEADS)], axis=1)
    o_mla = jnp.dot(ov, w_o_mla_ref[...], preferred_element_type=F32)
    g_gla = g_gla_ref[...]
    og = jnp.concatenate(
        [(_rms(og_ref[:, h * GLA_DV:(h + 1) * GLA_DV].astype(F32), g_gla)
          * sg_ref[:, h * GLA_DV:(h + 1) * GLA_DV].astype(F32)).astype(BF16) for h in range(GLA_HEADS)], axis=1)
    o_gla = jnp.dot(og, w_o_gla_ref[...], preferred_element_type=F32)
    merged = szm_ref[...].astype(F32) * o_mla + szg_ref[...].astype(F32) * o_gla
    h1 = x_ref[...] + _dot(merged, w_out_ref[...])
    h_ref[...] = h1
    qx_ref[...] = _dot(_rms(h1, g_cross_ref[...]), w_mq_ref[...]).astype(qx_ref.dtype)


def _postmix(x, olat, og, sg, szm, szg, w, tm):
    T = x.shape[0]
    row = lambda n: pl.BlockSpec((tm, n), lambda i: (i, 0))
    consts = [w["wuv"], w["w_o_mla"], w["g_gla"], w["w_o_gla"], w["w_out"], w["g_cross"], w["w_mq"]]
    return pl.pallas_call(
        _postmix_kernel,
        grid=(T // tm,),
        in_specs=[row(D_MODEL)] * 6 + [_full(c.shape, buffers=1) for c in consts],
        out_specs=[row(D_MODEL), row(MEM_WIDTH)],
        out_shape=[jax.ShapeDtypeStruct((T, D_MODEL), F32), jax.ShapeDtypeStruct((T, MEM_WIDTH), BF16)],
        compiler_params=_params(1),
        name="postmix",
    )(x, olat, og, sg, szm, szg, *consts)


def _memkv_kernel(mem_ref, g_ref, w_ref, k_ref, v_ref):
    kv = _dot(_rms(mem_ref[...], g_ref[...]), w_ref[...])
    k_ref[...] = kv[:, :MEM_WIDTH]
    v_ref[...] = kv[:, MEM_WIDTH:]


def _memkv(mem, g_mem, w_mkv, tm):
    T = mem.shape[0]
    row = lambda n: pl.BlockSpec((tm, n), lambda i: (i, 0))
    return pl.pallas_call(
        _memkv_kernel,
        grid=(T // tm,),
        in_specs=[row(D_MODEL), _full(g_mem.shape), _full(w_mkv.shape)],
        out_specs=[row(MEM_WIDTH), row(MEM_WIDTH)],
        out_shape=[jax.ShapeDtypeStruct((T, MEM_WIDTH), F32)] * 2,
        compiler_params=_params(1),
        name="memkv",
    )(mem, g_mem, w_mkv)


def _attend_many(problems):
    scores = [_dot_nt(q, k) * MEM_HEAD_DIM ** -0.5 for q, k, _ in problems]
    probs = []
    for s in scores:
        p = jnp.exp(s - jnp.max(s, axis=-1, keepdims=True))
        probs.append(p / jnp.sum(p, axis=-1, keepdims=True))
    return [_dot(p, v) for p, (_, _, v) in zip(probs, problems)]


def _head_cols(h):
    return slice(h * MEM_HEAD_DIM, (h + 1) * MEM_HEAD_DIM)


def _route(o, h1, w_mo_ref, g_ffn_ref, w_router_ref, b_router_ref, h_ref, xn_ref, idx_ref, rank_ref, wt_ref, cnt_ref):
    h2 = h1 + jnp.dot(o, w_mo_ref[...], preferred_element_type=F32)
    h_ref[...] = h2
    xn = _rms(h2, g_ffn_ref[...])
    xn_ref[...] = xn.astype(xn_ref.dtype)
    tb = xn.shape[0]
    logits = _dot_hi_nt(w_router_ref[...], xn) + b_router_ref[...]
    expert = lax.broadcasted_iota(jnp.int32, logits.shape, 0)
    work = logits
    hits, firsts, exps = [], [], []
    top = None
    for _ in range(TOP_K):
        best = jnp.max(work, axis=0, keepdims=True)
        first = jnp.min(jnp.where(work == best, expert, N_EXPERTS), axis=0, keepdims=True)
        hit = expert == first
        top = best if top is None else top
        hits.append(hit)
        firsts.append(first)
        exps.append(jnp.exp(best - top))
        work = jnp.where(hit, -jnp.inf, work)
    denom = (exps[0] + exps[1]) + (exps[2] + exps[3])
    chosen = jnp.zeros(logits.shape, F32)
    for hit in hits:
        chosen = chosen + jnp.where(hit, 1.0, 0.0)
    before = (lax.broadcasted_iota(jnp.int32, (tb, tb), 0) < lax.broadcasted_iota(jnp.int32, (tb, tb), 1)).astype(BF16)
    rank = jnp.dot(chosen.astype(BF16), before, preferred_element_type=F32)
    idx_ref[...] = jnp.concatenate(firsts, axis=0)
    rank_ref[...] = jnp.concatenate(
        [jnp.sum(jnp.where(hit, rank, 0.0), axis=0, keepdims=True) for hit in hits], axis=0).astype(jnp.int32)
    wt_ref[...] = jnp.concatenate([e / denom for e in exps], axis=0)
    cnt_ref[...] = jnp.sum(chosen, axis=1, keepdims=True).astype(jnp.int32)


def _cross_prompt_kernel(qx_ref, h1_ref, mk_ref, mv_ref, *rest):
    outs = _attend_many([(qx_ref[:, _head_cols(h)], mk_ref[:, _head_cols(h)], mv_ref[:, _head_cols(h)])
                         for h in range(MEM_HEADS)])
    _route(jnp.concatenate([o.astype(BF16) for o in outs], axis=1), h1_ref[...], *rest)


def _cross_sample_kernel(qx_ref, mk_ref, mv_ref, o_ref):
    rows = mk_ref.shape[1]
    pad = jnp.zeros((SUBLANES - MEM_HEADS, MEM_HEAD_DIM), qx_ref.dtype)
    head_of_row = lax.broadcasted_iota(jnp.int32, (SUBLANES, rows), 1) % MEM_HEADS
    own_head = head_of_row == lax.broadcasted_iota(jnp.int32, (SUBLANES, rows), 0)
    samples = range(qx_ref.shape[0])
    scores = [_dot_nt(jnp.concatenate([qx_ref[i:i + 1, _head_cols(h)] for h in range(MEM_HEADS)] + [pad], axis=0),
                      mk_ref[i]) * MEM_HEAD_DIM ** -0.5 for i in samples]
    probs = []
    for s in scores:
        s = jnp.where(own_head, s, -jnp.inf)
        p = jnp.exp(s - jnp.max(s, axis=-1, keepdims=True))
        probs.append(p / jnp.sum(p, axis=-1, keepdims=True))
    for i, p in zip(samples, probs):
        o = _dot(p, mv_ref[i])
        o_ref[i:i + 1, :] = jnp.concatenate([o[h:h + 1] for h in range(MEM_HEADS)], axis=1).astype(o_ref.dtype)


def _route_kernel(o_ref, h1_ref, *rest):
    _route(o_ref[...], h1_ref[...], *rest)


def _route_specs(T, tb, w):
    consts = [w["w_mo"], w["g_ffn"], w["w_router_t"], w["b_router"]]
    row = lambda n: pl.BlockSpec((tb, n), lambda i: (i, 0))
    per_choice = pl.BlockSpec((TOP_K, tb), lambda i: (0, i))
    out_specs = [row(D_MODEL), row(D_MODEL), per_choice, per_choice, per_choice,
                 pl.BlockSpec((None, N_EXPERTS, 1), lambda i: (i, 0, 0))]
    out_shape = [jax.ShapeDtypeStruct((T, D_MODEL), F32), jax.ShapeDtypeStruct((T, D_MODEL), BF16),
                 jax.ShapeDtypeStruct((TOP_K, T), jnp.int32), jax.ShapeDtypeStruct((TOP_K, T), jnp.int32),
                 jax.ShapeDtypeStruct((TOP_K, T), F32), jax.ShapeDtypeStruct((T // tb, N_EXPERTS, 1), jnp.int32)]
    return consts, out_specs, out_shape


def _cross_prompt(qx, h1, mem_k, mem_v, w, tb):
    T = qx.shape[0]
    blocks_per_mem = T // mem_k.shape[0] // tb
    consts, out_specs, out_shape = _route_specs(T, tb, w)
    row = lambda n: pl.BlockSpec((tb, n), lambda i: (i, 0))
    mem_spec = pl.BlockSpec((None, N_MEM, MEM_WIDTH), lambda i: (i // blocks_per_mem, 0, 0))
    return pl.pallas_call(
        _cross_prompt_kernel,
        grid=(T // tb,),
        in_specs=[row(MEM_WIDTH), row(D_MODEL), mem_spec, mem_spec] + [_full(c.shape) for c in consts],
        out_specs=out_specs,
        out_shape=out_shape,
        compiler_params=_params(1),
        name="cross_prompt",
    )(qx, h1, mem_k, mem_v, *consts)


def _cross_sample(qx, h1, mem_k, mem_v, w, nb=SAMPLES_PER_STEP):
    T = qx.shape[0]
    mem_spec = pl.BlockSpec((nb, N_MEM * MEM_HEADS, MEM_HEAD_DIM), lambda i: (i, 0, 0))
    rows = pl.BlockSpec((nb, MEM_WIDTH), lambda i: (i, 0))
    o = pl.pallas_call(
        _cross_sample_kernel,
        grid=(T // nb,),
        in_specs=[rows, mem_spec, mem_spec],
        out_specs=rows,
        out_shape=jax.ShapeDtypeStruct((T, MEM_WIDTH), BF16),
        compiler_params=_params(1),
        name="cross_sample",
    )(qx, mem_k, mem_v)
    consts, out_specs, out_shape = _route_specs(T, T, w)
    return pl.pallas_call(
        _route_kernel,
        grid=(1,),
        in_specs=[_full(o.shape), _full(h1.shape)] + [_full(c.shape) for c in consts],
        out_specs=out_specs,
        out_shape=out_shape,
        compiler_params=_params(1),
        name="route_sample",
    )(o, h1, *consts)


SEG_ALIGN = 16
SEL_CHUNK = 256
MOE_BLOCK = 512


def _block_rows(tb):
    worst = TOP_K * tb + N_EXPERTS * (SEG_ALIGN - 1)
    return -(-worst // SEL_CHUNK) * SEL_CHUNK


def _seg_sizes(limit):
    sizes, b = [], SEG_ALIGN
    while b <= limit:
        sizes.append(b)
        b *= 2
    return sizes[::-1]


def _for_segments(n, sizes, make_copy, act):
    for size in sizes:
        @pl.when((n & size) != 0)
        def _():
            act(make_copy(pl.multiple_of(n & (-2 * size), SEG_ALIGN), size))


def _slot_rows(idx, rank, loc_ref, j):
    pos = rank
    for e in range(N_EXPERTS):
        pos = pos + jnp.where(idx == e, loc_ref[j, e], 0)
    return pos


def _dispatch_kernel(loc_ref, pad_ref, goff_ref, total_ref, gap_off_ref, gap_len_ref, x_ref, x_last_ref, idx_ref, rank_ref,
                     xs_hbm, buf, sem, *, tm):
    j = pl.program_id(0)
    last = pl.num_programs(0) - 1
    slot = j % 2
    tb = x_ref.shape[0]
    r_blk = _block_rows(tb)
    pos = _slot_rows(idx_ref[...], rank_ref[...], loc_ref, j)
    x = jnp.where(j == last, x_last_ref[...], x_ref[...])
    for c in range(r_blk // SEL_CHUNK):
        r = lax.broadcasted_iota(jnp.int32, (SEL_CHUNK, tb), 0) + c * SEL_CHUNK
        sel = jnp.where(r == pos[0:1], 1.0, jnp.where(r == pos[1:2], 1.0, jnp.where(
            r == pos[2:3], 1.0, jnp.where(r == pos[3:4], 1.0, 0.0))))
        buf[slot, c * SEL_CHUNK:(c + 1) * SEL_CHUNK, :] = jnp.dot(
            sel.astype(BF16), x, preferred_element_type=F32).astype(BF16)

    sizes = _seg_sizes(tb)

    def start_segments(e, carry):
        src, dst = loc_ref[j, e], goff_ref[j, e]
        _for_segments(pad_ref[j, e], sizes, lambda off, size: pltpu.make_async_copy(
            buf.at[slot, pl.ds(pl.multiple_of(src + off, SEG_ALIGN), size)],
            xs_hbm.at[pl.ds(pl.multiple_of(dst + off, SEG_ALIGN), size)], sem.at[slot]), lambda cp: cp.start())
        return carry

    lax.fori_loop(0, N_EXPERTS, start_segments, 0)

    def wait_block(blk, slot):
        _for_segments(total_ref[blk], _seg_sizes(r_blk), lambda off, size: pltpu.make_async_copy(
            buf.at[slot, pl.ds(0, size)], xs_hbm.at[pl.ds(0, size)], sem.at[slot]), lambda cp: cp.wait())

    @pl.when(j > 0)
    def _():
        wait_block(j - 1, 1 - slot)

    @pl.when(j == last)
    def _():
        wait_block(j, slot)
        buf[1 - slot, 0:tm, :] = jnp.zeros((tm, D_MODEL), BF16)
        gap_sizes = _seg_sizes(tm - SEG_ALIGN)

        def for_gaps(act):
            def body(e, carry):
                dst = gap_off_ref[e]
                _for_segments(gap_len_ref[e], gap_sizes, lambda off, size: pltpu.make_async_copy(
                    buf.at[1 - slot, pl.ds(0, size)], xs_hbm.at[pl.ds(pl.multiple_of(dst + off, SEG_ALIGN), size)],
                    sem.at[1 - slot]), act)
                return carry
            lax.fori_loop(0, N_EXPERTS, body, 0)

        for_gaps(lambda cp: cp.start())
        for_gaps(lambda cp: cp.wait())


def _ffn_kernel(tile_expert_ref, n_valid_ref, x_ref, wgu_f32_ref, bgu_ref, wd_f32_ref, bd_ref, o_ref, wgu_ref, wd_ref):
    t = pl.program_id(0)

    @pl.when(t < n_valid_ref[0])
    def _():
        @pl.when((t == 0) | (tile_expert_ref[t] != tile_expert_ref[jnp.maximum(t - 1, 0)]))
        def _():
            wgu_ref[...] = wgu_f32_ref[...].astype(BF16)
            wd_ref[...] = wd_f32_ref[...].astype(BF16)

        x = x_ref[...]
        bgu = bgu_ref[...]
        gate = jnp.dot(x, wgu_ref[:, :D_FF], preferred_element_type=F32) + bgu[:, :D_FF]
        up = jnp.dot(x, wgu_ref[:, D_FF:], preferred_element_type=F32) + bgu[:, D_FF:]
        gate = jnp.minimum(gate, SWIGLU_LIMIT)
        up = jnp.clip(up, -SWIGLU_LIMIT, SWIGLU_LIMIT)
        hidden = (up + 1.0) * gate * jax.nn.sigmoid(SWIGLU_ALPHA * gate)
        o_ref[...] = (_dot(hidden, wd_ref[...]) + bd_ref[...]).astype(o_ref.dtype)


def _combine_kernel(loc_ref, pad_ref, goff_ref, total_ref, idx0_ref, rank0_ref, wt0_ref, idx_ref, rank_ref, wt_ref,
                    h_ref, h_last_ref, g_final_ref, o_hbm, y_ref, y_last_ref, buf, sel_ref, sem):
    j = pl.program_id(0)
    last = pl.num_programs(0) - 1
    slot = j % 2
    tb = h_ref.shape[0]
    r_blk = buf.shape[1]
    sizes = _seg_sizes(tb)

    def fetch(blk, slot):
        def body(e, carry):
            dst, src = loc_ref[blk, e], goff_ref[blk, e]
            _for_segments(pad_ref[blk, e], sizes, lambda off, size: pltpu.make_async_copy(
                o_hbm.at[pl.ds(pl.multiple_of(src + off, SEG_ALIGN), size)],
                buf.at[slot, pl.ds(pl.multiple_of(dst + off, SEG_ALIGN), size)], sem.at[slot]), lambda cp: cp.start())
            return carry
        lax.fori_loop(0, N_EXPERTS, body, 0)

    def build(blk, slot, idx, rank, wt):
        pos = _slot_rows(idx, rank, loc_ref, blk)
        for c in range(r_blk // SEL_CHUNK):
            r = lax.broadcasted_iota(jnp.int32, (SEL_CHUNK, tb), 0) + c * SEL_CHUNK
            sel = jnp.zeros((SEL_CHUNK, tb), F32)
            for k in range(TOP_K):
                sel = jnp.where(r == pos[k:k + 1], wt[k:k + 1], sel)
            sel_ref[slot, c * SEL_CHUNK:(c + 1) * SEL_CHUNK, :] = sel.astype(BF16)

    @pl.when(j == 0)
    def _():
        buf[...] = jnp.zeros_like(buf)
        fetch(0, 0)
        build(0, 0, idx0_ref[...], rank0_ref[...], wt0_ref[...])

    @pl.when(j < last)
    def _():
        fetch(j + 1, 1 - slot)

    _for_segments(total_ref[j], _seg_sizes(r_blk), lambda off, size: pltpu.make_async_copy(
        o_hbm.at[pl.ds(0, size)], buf.at[slot, pl.ds(0, size)], sem.at[slot]), lambda cp: cp.wait())
    moe = _dot_tn(sel_ref[slot], buf[slot])
    y = _rms(jnp.where(j == last, h_last_ref[...], h_ref[...]) + moe, g_final_ref[...])
    build(jnp.minimum(j + 1, last), 1 - slot, idx_ref[...], rank_ref[...], wt_ref[...])

    @pl.when(j < last)
    def _():
        y_ref[...] = y

    @pl.when(j == last)
    def _():
        y_last_ref[...] = y


def _moe(xn, xn_last, idx, rank, wt, cnt, h2, h2_last, w, tb, tm):
    T = xn.shape[0]
    nb = T // tb + 1
    r_blk = _block_rows(tb)
    i32 = jnp.int32
    pad = (cnt + (SEG_ALIGN - 1)) // SEG_ALIGN * SEG_ALIGN
    loc = jnp.cumsum(pad, axis=1) - pad
    seg = jnp.sum(pad, axis=0)
    region = (seg + (tm - 1)) // tm * tm
    region_end = jnp.cumsum(region)
    region_start = region_end - region
    goff = region_start[None, :] + jnp.cumsum(pad, axis=0) - pad
    n_tiles = -(-(nb * (TOP_K * tb + N_EXPERTS * (SEG_ALIGN - 1)) + N_EXPERTS * (tm - 1)) // tm)
    n_valid = (region_end[-1] // tm).astype(i32).reshape(1)
    tile = jnp.minimum(jnp.arange(n_tiles, dtype=i32), n_valid - 1)
    tile_expert = jnp.minimum(jnp.sum((region_end[None, :] <= (tile * tm)[:, None]).astype(i32), axis=1), N_EXPERTS - 1)
    plan = [a.astype(i32) for a in (loc, pad, goff, jnp.sum(pad, axis=1))]
    gaps = [(region_start + seg).astype(i32), (region - seg).astype(i32)]

    choice = pl.BlockSpec((TOP_K, tb), lambda j, *_: (0, j))
    body_rows = pl.BlockSpec((tb, D_MODEL), lambda j, *_: (jnp.minimum(j, nb - 2), 0))
    last_rows = pl.BlockSpec((tb, D_MODEL), lambda j, *_: (0, 0))
    xs = pl.pallas_call(
        functools.partial(_dispatch_kernel, tm=tm),
        grid_spec=pltpu.PrefetchScalarGridSpec(
            num_scalar_prefetch=6, grid=(nb,),
            in_specs=[body_rows, last_rows, choice, choice],
            out_specs=pl.BlockSpec(memory_space=pl.ANY),
            scratch_shapes=[pltpu.VMEM((2, r_blk, D_MODEL), BF16), pltpu.SemaphoreType.DMA((2,))]),
        out_shape=jax.ShapeDtypeStruct((n_tiles * tm, D_MODEL), BF16),
        compiler_params=_params(1),
        name="moe_dispatch",
    )(*plan, *gaps, xn, xn_last, idx, rank)

    rows = pl.BlockSpec((tm, D_MODEL), lambda t, te, nv: (jnp.minimum(t, nv[0] - 1), 0))
    per_e = lambda a, b: pl.BlockSpec((None, a, b), lambda t, te, nv: (te[t], 0, 0))
    out = pl.pallas_call(
        _ffn_kernel,
        grid_spec=pltpu.PrefetchScalarGridSpec(
            num_scalar_prefetch=2, grid=(n_tiles,),
            in_specs=[rows, per_e(D_MODEL, 2 * D_FF), per_e(1, 2 * D_FF), per_e(D_FF, D_MODEL), per_e(1, D_MODEL)],
            out_specs=rows,
            scratch_shapes=[pltpu.VMEM((D_MODEL, 2 * D_FF), BF16), pltpu.VMEM((D_FF, D_MODEL), BF16)]),
        out_shape=jax.ShapeDtypeStruct((n_tiles * tm, D_MODEL), BF16),
        compiler_params=_params(1),
        name="moe_ffn",
    )(tile_expert, n_valid, xs, w["w_gate_up"], w["b_gate_up"], w["w_down"], w["b_down"])

    first = pl.BlockSpec((TOP_K, tb), lambda j, *_: (0, 0))
    ahead = pl.BlockSpec((TOP_K, tb), lambda j, *_: (0, jnp.minimum(j + 1, nb - 1)))
    return pl.pallas_call(
        _combine_kernel,
        grid_spec=pltpu.PrefetchScalarGridSpec(
            num_scalar_prefetch=4, grid=(nb,),
            in_specs=[first, first, first, ahead, ahead, ahead, body_rows, last_rows,
                      pl.BlockSpec((1, D_MODEL), lambda j, *_: (0, 0)), pl.BlockSpec(memory_space=pl.ANY)],
            out_specs=[body_rows, last_rows],
            scratch_shapes=[pltpu.VMEM((2, r_blk, D_MODEL), BF16), pltpu.VMEM((2, r_blk, tb), BF16),
                            pltpu.SemaphoreType.DMA((2,))]),
        out_shape=[jax.ShapeDtypeStruct((T, D_MODEL), F32), jax.ShapeDtypeStruct((tb, D_MODEL), F32)],
        compiler_params=_params(1),
        name="moe_combine",
    )(*plan, idx, rank, wt, idx, rank, wt, h2, h2_last, w["g_final"], out)


def _rope_table(pos):
    half = QK_ROPE // 2
    inv = ROPE_THETA ** (-jnp.arange(half, dtype=F32) * 2.0 / QK_ROPE)
    ang = pos[:, None] * inv[None, :]
    cos, sin, zero = jnp.cos(ang), jnp.sin(ang), jnp.zeros_like(ang)
    return jnp.concatenate([cos, cos, cos, cos, -sin, zero, -sin, zero, zero, sin, zero, sin], axis=1)


def _prepare_weights(g_attn, w_in, g_q, g_kv, w_uq, w_uk, w_uv, w_a2, b_a, g_gla, w_o_mla, w_o_gla, w_out,
                     g_mem, w_mk, w_mv, g_cross, w_mq, w_mo, g_ffn, w_router, b_router, w_gate_up, b_gate_up,
                     w_down, b_down, g_final):
    splits = np.cumsum((0,) + IN_SPLITS)
    parts = [w_in[0][:, splits[i]:splits[i + 1]] for i in range(len(IN_SPLITS))]
    parts = [jnp.pad(p, ((0, 0), (0, wd - p.shape[1]))) for p, wd in zip(parts, _PACKED_WIDTHS)]
    wq = w_uq[0].reshape(Q_LORA, MLA_HEADS, QK_NOPE + QK_ROPE)
    wq_rope = jnp.pad(wq[:, :, QK_NOPE:], ((0, 0), (0, 0), (0, LANES - QK_ROPE)))
    row = lambda v: v.reshape(1, -1)
    return dict(
        g_attn=row(g_attn[0]), w_in=jnp.concatenate(parts, axis=1).astype(BF16), g_q=row(g_q[0]), g_kv=row(g_kv[0]),
        w_uq=jnp.concatenate([_fold_latent_query(jnp.transpose(wq[:, :, :QK_NOPE], (1, 0, 2)),
                                                 jnp.transpose(w_uk[0], (1, 0, 2))),
                              wq_rope.reshape(Q_LORA, -1).astype(BF16)], axis=1),
        wuv=jnp.transpose(w_uv[0], (1, 0, 2)).astype(BF16),
        w_a2=jnp.pad(w_a2[0], ((0, LANES - GLA_RANK), (0, 0))).astype(BF16), b_a=row(b_a[0]),
        g_gla=row(g_gla[0]), w_o_mla=w_o_mla[0].astype(BF16), w_o_gla=w_o_gla[0].astype(BF16),
        w_out=w_out[0].astype(BF16), g_mem=row(g_mem[0]),
        w_mkv=jnp.concatenate([w_mk[0], w_mv[0]], axis=1).astype(BF16),
        g_cross=row(g_cross[0]), w_mq=w_mq[0].astype(BF16), w_mo=w_mo[0].astype(BF16), g_ffn=row(g_ffn[0]),
        w_router_t=w_router[0].T, b_router=b_router[0].reshape(-1, 1),
        w_gate_up=w_gate_up[0], b_gate_up=b_gate_up[0].reshape(N_EXPERTS, 1, 2 * D_FF),
        w_down=w_down[0], b_down=b_down[0].reshape(N_EXPERTS, 1, D_MODEL), g_final=row(g_final),
    )


def kernel(x_prompt, x_sample, cache_kv_latent, cache_k_rope, state_gla, cache_mem_k, cache_mem_v, page_table, mem_prompt, g_attn, w_in, g_q, g_kv, w_uq, w_uk, w_uv, w_a2, b_a, g_gla, w_o_mla, w_o_gla, w_out, g_mem, w_mk, w_mv, g_cross, w_mq, w_mo, g_ffn, w_router, b_router, w_gate_up, b_gate_up, w_down, b_down, g_final):
    w = _prepare_weights(g_attn, w_in, g_q, g_kv, w_uq, w_uk, w_uv, w_a2, b_a, g_gla, w_o_mla, w_o_gla, w_out,
                         g_mem, w_mk, w_mv, g_cross, w_mq, w_mo, g_ffn, w_router, b_router, w_gate_up, b_gate_up,
                         w_down, b_down, g_final)
    B, S, D = x_prompt.shape
    Bd = x_sample.shape[0]
    past_len = page_table.shape[1] * cache_kv_latent.shape[2]

    tm = min(MLA_BLOCK, S)
    xp = x_prompt.reshape(B * S, D)
    (qcat, ckv, krope, kcat, kcat_t, gq, gk, gv, sg, la, szm, szg) = _premix(
        xp, _rope_table(jnp.arange(S, dtype=F32)), w, min(PREMIX_ROWS, S), tm, BF16)
    olat = _mla_prompt(qcat.reshape(MLA_HEADS, B, S, QCAT), kcat.reshape(B, S, QCAT), kcat_t, tm)
    og, state_p = _gla_prompt(gq.reshape(B, S, -1), gk.reshape(B, S, -1), gv.reshape(B, S, -1), la.reshape(B, S, -1))
    h1, qx = _postmix(xp, olat.reshape(B * S, -1), og.reshape(B * S, -1), sg, szm, szg, w, min(POSTMIX_ROWS, S))
    mem_k, mem_v = _memkv(mem_prompt.reshape(B * N_MEM, D), w["g_mem"], w["w_mkv"], min(MEMKV_ROWS, B * N_MEM))
    tb = min(MOE_BLOCK, S)
    h2, xn2, idx, rank, wt, cnt = _cross_prompt(qx, h1, mem_k.reshape(B, N_MEM, MEM_WIDTH),
                                                mem_v.reshape(B, N_MEM, MEM_WIDTH), w, tb)

    xs = x_sample.reshape(Bd, D)
    (qcat_s, ckv_s, krope_s, kcat_s, _, gq_s, gk_s, gv_s, sg_s, la_s, szm_s, szg_s) = _premix(
        xs, _rope_table(jnp.full((Bd,), past_len, F32)), w, Bd, Bd, F32)
    olat_s = _mla_sample(page_table, qcat_s, kcat_s, cache_kv_latent[0], jnp.swapaxes(cache_k_rope[0], 1, 2))
    og_s, state_s = _gla_sample(gq_s, gk_s, gv_s, la_s, state_gla[0])
    h1_s, qx_s = _postmix(xs, olat_s.reshape(Bd, -1), og_s, sg_s, szm_s, szg_s, w, Bd)
    h2_s, xn2_s, idx_s, rank_s, wt_s, cnt_s = _cross_sample(
        qx_s, h1_s, cache_mem_k.reshape(Bd, N_MEM * MEM_HEADS, MEM_HEAD_DIM),
        cache_mem_v.reshape(Bd, N_MEM * MEM_HEADS, MEM_HEAD_DIM), w)

    fill = lambda a, v: jnp.pad(a, ((0, 0), (0, tb - Bd)), constant_values=v)
    rows = lambda a: jnp.pad(a, ((0, tb - Bd), (0, 0)))
    y_prompt, y_last = _moe(
        xn2, rows(xn2_s), jnp.concatenate([idx, fill(idx_s, N_EXPERTS)], axis=1),
        jnp.concatenate([rank, fill(rank_s, -1)], axis=1), jnp.concatenate([wt, fill(wt_s, 0.0)], axis=1),
        jnp.concatenate([cnt.reshape(-1, N_EXPERTS), cnt_s.reshape(-1, N_EXPERTS)], axis=0), h2, rows(h2_s), w, tb, tb)
    y_sample = y_last[:Bd]

    return (y_prompt.reshape(B, S, D), y_sample.reshape(Bd, 1, D),
            ckv.reshape(1, B, S, KV_LORA), krope.reshape(1, B, S, QK_ROPE), state_p[None],
            mem_k.reshape(1, B, N_MEM, MEM_HEADS, MEM_HEAD_DIM), mem_v.reshape(1, B, N_MEM, MEM_HEADS, MEM_HEAD_DIM),
            ckv_s.reshape(1, Bd, 1, KV_LORA), krope_s.reshape(1, Bd, 1, QK_ROPE), state_s[None])
```

```python
import functools

import jax
import jax.numpy as jnp
import numpy as np
from jax import lax
from jax.experimental import pallas as pl
from jax.experimental.pallas import tpu as pltpu

F32 = jnp.float32
BF16 = jnp.bfloat16

D_MODEL = 1024
MLA_HEADS = 8
Q_LORA = 256
KV_LORA = 128
QK_NOPE = 128
QK_ROPE = 64
ROPE_THETA = 10000.0
ATTN_SCALE = (QK_NOPE + QK_ROPE) ** -0.5
GLA_HEADS = 4
GLA_DK = 128
GLA_DV = 256
GLA_RANK = 16
GLA_TAU = 16.0
GLA_CHUNK = 64
GLA_GROUP = 4
N_MEM = 256
MEM_HEADS = 4
MEM_HEAD_DIM = 128
N_EXPERTS = 32
TOP_K = 4
D_FF = 1024
SWIGLU_LIMIT = 7.0
SWIGLU_ALPHA = 1.702
EPS = 1e-6

GLA_K_WIDTH = GLA_HEADS * GLA_DK
GLA_V_WIDTH = GLA_HEADS * GLA_DV
MEM_WIDTH = MEM_HEADS * MEM_HEAD_DIM
IN_SPLITS = (Q_LORA, KV_LORA, QK_ROPE, GLA_K_WIDTH, GLA_K_WIDTH, GLA_V_WIDTH, GLA_V_WIDTH, GLA_RANK, D_MODEL, D_MODEL)

LANES = 128
SUBLANES = 8
_PACKED_WIDTHS = (Q_LORA, KV_LORA, LANES, GLA_K_WIDTH, GLA_K_WIDTH, GLA_V_WIDTH, GLA_V_WIDTH, LANES, D_MODEL, D_MODEL)
_OFF = tuple(int(v) for v in np.cumsum((0,) + _PACKED_WIDTHS))
QCAT = 2 * LANES

VMEM_LIMIT = 56 << 20
PREMIX_ROWS = 512
POSTMIX_ROWS = 512
MLA_BLOCK = 256
MEMKV_ROWS = 512
SAMPLES_PER_STEP = 8


def _params(n_axes):
    return pltpu.CompilerParams(dimension_semantics=("arbitrary",) * n_axes, vmem_limit_bytes=VMEM_LIMIT)


def _rms(x, g):
    var = jnp.mean(x * x, axis=-1, keepdims=True)
    return x * lax.rsqrt(var + EPS) * g


def _dot(a, b):
    return jnp.dot(a.astype(BF16), b.astype(BF16), preferred_element_type=F32)


def _dot_nt(a, b):
    return lax.dot_general(a.astype(BF16), b.astype(BF16), (((1,), (1,)), ((), ())), preferred_element_type=F32)


def _dot_tn(a, b):
    return lax.dot_general(a.astype(BF16), b.astype(BF16), (((0,), (0,)), ((), ())), preferred_element_type=F32)


def _split3(x):
    hi = x.astype(BF16)
    r1 = x - hi.astype(F32)
    mid = r1.astype(BF16)
    lo = (r1 - mid.astype(F32)).astype(BF16)
    return hi, mid, lo


def _dot_f32_nt(a, b):
    a0, a1, a2 = _split3(a)
    b0, b1, b2 = _split3(b)
    d = lambda u, v: lax.dot_general(u, v, (((1,), (1,)), ((), ())), preferred_element_type=F32)
    return ((d(a1, b1) + d(a0, b2) + d(a2, b0)) + (d(a0, b1) + d(a1, b0))) + d(a0, b0)


def _dot_hi_nt(a, b):
    a0 = a.astype(BF16)
    a1 = (a - a0.astype(F32)).astype(BF16)
    b0 = b.astype(BF16)
    b1 = (b - b0.astype(F32)).astype(BF16)
    d = lambda u, v: lax.dot_general(u, v, (((1,), (1,)), ((), ())), preferred_element_type=F32)
    return (d(a0, b1) + d(a1, b0)) + d(a0, b0)


def _full(shape, buffers=None):
    n = len(shape)
    return pl.BlockSpec(shape, lambda *_: (0,) * n, pipeline_mode=pl.Buffered(buffers) if buffers else None)


def _premix_kernel(x_ref, tab_ref, g_attn_ref, w_in_ref, g_q_ref, g_kv_ref, w_uq_ref, w_a2_ref, b_a_ref,
                   qcat_ref, ckv_ref, krope_ref, kcat_ref, kcat_t_ref, gq_ref, gk_ref, gv_ref, sg_ref, la_ref, szm_ref, szg_ref):
    xn = _rms(x_ref[...], g_attn_ref[...]).astype(BF16)

    def proj(lo, hi):
        y = jnp.dot(xn, w_in_ref[:, _OFF[lo]:_OFF[hi]], preferred_element_type=F32)
        return [y[:, _OFF[i] - _OFF[lo]:_OFF[i + 1] - _OFF[lo]] for i in range(lo, hi)]

    tab = tab_ref[...]
    cos, sin_lo, sin_hi = tab[:, :LANES], tab[:, LANES:2 * LANES], tab[:, 2 * LANES:]

    def rope(t):
        return t * cos + pltpu.roll(t, LANES - QK_ROPE // 2, 1) * sin_lo + pltpu.roll(t, QK_ROPE // 2, 1) * sin_hi

    cq, ckv, kr = proj(0, 3)
    c_q = _rms(cq, g_q_ref[...]).astype(BF16)
    ckv = _rms(ckv, g_kv_ref[...])
    kr = rope(kr)
    ckv_ref[...] = ckv
    krope_ref[...] = kr[:, :QK_ROPE]
    kcat = jnp.concatenate([ckv, kr], axis=1)
    kcat_ref[...] = kcat.astype(BF16)
    tk = kcat_t_ref.shape[2]
    for t in range(kcat_t_ref.shape[0]):
        kcat_t_ref[t] = kcat[t * tk:(t + 1) * tk].T.astype(BF16)
    q_all = jnp.dot(c_q, w_uq_ref[...], preferred_element_type=F32)
    for h in range(MLA_HEADS):
        qcat_ref[h, :, :LANES] = q_all[:, h * LANES:(h + 1) * LANES].astype(qcat_ref.dtype)
        qcat_ref[h, :, LANES:] = rope(q_all[:, (MLA_HEADS + h) * LANES:(MLA_HEADS + h + 1) * LANES]).astype(qcat_ref.dtype)
    gq, gk = proj(3, 5)
    gq_ref[...] = (gq * GLA_DK ** -0.5).astype(gq_ref.dtype)
    gk_ref[...] = gk.astype(gk_ref.dtype)
    gv_ref[...] = proj(5, 6)[0].astype(gv_ref.dtype)
    gg, ga = proj(6, 8)
    sg_ref[...] = (gg * jax.nn.sigmoid(gg)).astype(sg_ref.dtype)
    z = _dot(ga, w_a2_ref[...]) + b_a_ref[...]
    la_ref[...] = -(jnp.maximum(-z, 0.0) + jnp.log1p(jnp.exp(-jnp.abs(z)))) * (1.0 / GLA_TAU)
    szm_ref[...] = jax.nn.sigmoid(proj(8, 9)[0]).astype(szm_ref.dtype)
    szg_ref[...] = jax.nn.sigmoid(proj(9, 10)[0]).astype(szg_ref.dtype)


def _fold_kernel(wq_ref, wk_ref, o_ref):
    o_ref[...] = _dot_f32_nt(wq_ref[...], wk_ref[...]).astype(o_ref.dtype)


def _fold_latent_query(w_uq_nope, w_uk):
    return pl.pallas_call(
        _fold_kernel,
        grid=(MLA_HEADS,),
        in_specs=[pl.BlockSpec((None, Q_LORA, QK_NOPE), lambda h: (h, 0, 0)),
                  pl.BlockSpec((None, KV_LORA, QK_NOPE), lambda h: (h, 0, 0))],
        out_specs=pl.BlockSpec((Q_LORA, KV_LORA), lambda h: (0, h)),
        out_shape=jax.ShapeDtypeStruct((Q_LORA, MLA_HEADS * KV_LORA), BF16),
        compiler_params=_params(1),
        name="fold_latent_query",
    )(w_uq_nope, w_uk)


def _premix(x, tab, w, tm, tk, gdtype):
    T = x.shape[0]
    n_tab = tab.shape[0] // tm
    row = lambda n: pl.BlockSpec((tm, n), lambda i: (i, 0))
    outs = [(KV_LORA, F32), (QK_ROPE, F32), (QCAT, BF16), None, (GLA_K_WIDTH, gdtype),
            (GLA_K_WIDTH, gdtype), (GLA_V_WIDTH, gdtype), (GLA_V_WIDTH, BF16), (GLA_K_WIDTH, F32),
            (D_MODEL, BF16), (D_MODEL, BF16)]
    out_specs = [row(o[0]) if o else pl.BlockSpec((tm // tk, QCAT, tk), lambda i: (i, 0, 0)) for o in outs]
    out_shape = [jax.ShapeDtypeStruct((T, o[0]) if o else (T // tk, QCAT, tk), o[1] if o else BF16) for o in outs]
    consts = [w["g_attn"], w["w_in"], w["g_q"], w["g_kv"], w["w_uq"], w["w_a2"], w["b_a"]]
    const_specs = [pl.BlockSpec(c.shape, lambda i, n=len(c.shape): (0,) * n,
                                pipeline_mode=pl.Buffered(1) if c is w["w_in"] else None) for c in consts]
    return pl.pallas_call(
        _premix_kernel,
        grid=(T // tm,),
        in_specs=[row(D_MODEL), pl.BlockSpec((tm, 3 * LANES), lambda i: (i % n_tab, 0))] + const_specs,
        out_specs=[pl.BlockSpec((MLA_HEADS, tm, QCAT), lambda i: (0, i, 0))] + out_specs,
        out_shape=[jax.ShapeDtypeStruct((MLA_HEADS, T, QCAT), BF16)] + out_shape,
        compiler_params=_params(1),
        name="premix",
    )(x, tab, *consts)


def _mla_prompt_kernel(q_ref, k_ref, kt_ref, o_ref, m_ref, l_ref, acc_ref):
    tq = q_ref.shape[1]
    qi = pl.program_id(1)
    q = q_ref[...].reshape(MLA_HEADS * tq, QCAT)
    row = lax.broadcasted_iota(jnp.int32, (MLA_HEADS, tq, tq), 1).reshape(MLA_HEADS * tq, tq)
    col = lax.broadcasted_iota(jnp.int32, (MLA_HEADS * tq, tq), 1)
    m_ref[...] = jnp.full_like(m_ref, -jnp.inf)
    l_ref[...] = jnp.zeros_like(l_ref)
    acc_ref[...] = jnp.zeros_like(acc_ref)
    c = ATTN_SCALE * np.log2(np.e)

    def step(j, n, mask_last):
        keys = pl.ds(pl.multiple_of(j * tq, tq), n * tq)
        parts = [jnp.dot(q, kt_ref[j + t], preferred_element_type=F32) for t in range(n)]
        if mask_last:
            parts[-1] = jnp.where(col <= row, parts[-1], -jnp.inf)
        s = jnp.concatenate(parts, axis=1)
        m = m_ref[...]
        m_new = jnp.maximum(m, jnp.max(s, axis=-1, keepdims=True))
        alpha = jnp.exp2((m - m_new) * c)
        p = jnp.exp2((s - jnp.concatenate([m_new] * (n * tq // LANES), axis=1)) * c)
        p_lanes = p[:, :LANES]
        for t in range(1, n * tq // LANES):
            p_lanes = p_lanes + p[:, t * LANES:(t + 1) * LANES]
        l_ref[...] = alpha * l_ref[...] + p_lanes
        acc_ref[...] = alpha * acc_ref[...] + _dot(p, k_ref[keys, :KV_LORA])
        m_ref[...] = m_new

    def body(jj, carry):
        step(2 * jj, 2, False)
        return carry

    lax.fori_loop(0, qi // 2, body, 0)

    @pl.when(qi % 2 == 0)
    def _():
        step(qi, 1, True)

    @pl.when(qi % 2 == 1)
    def _():
        step(qi - 1, 2, True)
    o = acc_ref[...] / jnp.sum(l_ref[...], axis=-1, keepdims=True)
    for h in range(MLA_HEADS):
        o_ref[:, h * KV_LORA:(h + 1) * KV_LORA] = o[h * tq:(h + 1) * tq].astype(o_ref.dtype)


def _mla_prompt(qcat, kcat, kcat_t, tq):
    _, B, S, _ = qcat.shape
    rows = MLA_HEADS * tq
    return pl.pallas_call(
        _mla_prompt_kernel,
        grid=(B, S // tq),
        in_specs=[pl.BlockSpec((MLA_HEADS, None, tq, QCAT), lambda b, i: (0, b, i, 0)),
                  pl.BlockSpec((None, S, QCAT), lambda b, i: (b, 0, 0)),
                  pl.BlockSpec((S // tq, QCAT, tq), lambda b, i: (b, 0, 0))],
        out_specs=pl.BlockSpec((None, tq, MLA_HEADS * KV_LORA), lambda b, i: (b, i, 0)),
        out_shape=jax.ShapeDtypeStruct((B, S, MLA_HEADS * KV_LORA), BF16),
        scratch_shapes=[pltpu.VMEM((rows, LANES), F32), pltpu.VMEM((rows, LANES), F32), pltpu.VMEM((rows, KV_LORA), F32)],
        compiler_params=_params(2),
        name="mla_prompt",
    )(qcat, kcat, kcat_t)


def _mla_sample_kernel(pt_ref, q_ref, knew_ref, lat_hbm, rope_hbm, o_ref, lat_buf, rope_buf, lat_bf, rope_bf, sems):
    b = pl.program_id(0)
    n_pages, page = lat_buf.shape[1], lat_buf.shape[2]

    def page_copies(sample, slot, i):
        pg = pt_ref[sample, i]
        return (pltpu.make_async_copy(lat_hbm.at[pg], lat_buf.at[slot, i], sems.at[0, slot]),
                pltpu.make_async_copy(rope_hbm.at[pg], rope_buf.at[slot, i], sems.at[1, slot]))

    def start_pages(sample, slot):
        def body(i, carry):
            for queue, cp in enumerate(page_copies(sample, slot, i)):
                cp.start(priority=queue)
            return carry
        lax.fori_loop(0, n_pages, body, 0)

    slot = b % 2

    @pl.when(b == 0)
    def _():
        start_pages(0, 0)

    @pl.when(b + 1 < pl.num_programs(0))
    def _():
        start_pages(b + 1, 1 - slot)

    pltpu.make_async_copy(lat_hbm.at[pl.ds(0, n_pages)], lat_buf.at[slot], sems.at[0, slot]).wait()
    pltpu.make_async_copy(rope_hbm.at[pl.ds(0, n_pages)], rope_buf.at[slot], sems.at[1, slot]).wait()

    lat_bf[...] = lat_buf[slot].reshape(n_pages * page, KV_LORA).astype(BF16)
    for i in range(n_pages):
        rope_bf[:, i * page:(i + 1) * page] = rope_buf[slot, i].astype(BF16)
    q = q_ref[...].reshape(MLA_HEADS, QCAT)
    knew = knew_ref[...].astype(F32)
    s = (_dot_nt(q[:, :KV_LORA], lat_bf[...]) + _dot(q[:, KV_LORA:KV_LORA + QK_ROPE], rope_bf[...])) * ATTN_SCALE
    s_new = jnp.sum(q.astype(F32) * knew, axis=-1, keepdims=True) * ATTN_SCALE
    m = jnp.maximum(jnp.max(s, axis=-1, keepdims=True), s_new)
    p = jnp.exp(s - m)
    p_new = jnp.exp(s_new - m)
    denom = jnp.sum(p, axis=-1, keepdims=True) + p_new
    o = _dot(p, lat_bf[...]) + p_new * knew[:, :KV_LORA]
    o_ref[...] = (o / denom).astype(o_ref.dtype)


def _mla_sample(page_table, qcat, knew, cache_lat, cache_rope_t):
    Bd, n_pages = page_table.shape
    page = cache_lat.shape[1]
    grid_spec = pltpu.PrefetchScalarGridSpec(
        num_scalar_prefetch=1,
        grid=(Bd,),
        in_specs=[pl.BlockSpec((MLA_HEADS, None, 1, QCAT), lambda b, pt: (0, b, 0, 0)),
                  pl.BlockSpec((None, 1, QCAT), lambda b, pt: (b, 0, 0)),
                  pl.BlockSpec(memory_space=pl.ANY), pl.BlockSpec(memory_space=pl.ANY)],
        out_specs=pl.BlockSpec((None, MLA_HEADS, KV_LORA), lambda b, pt: (b, 0, 0)),
        scratch_shapes=[pltpu.VMEM((2, n_pages, page, KV_LORA), F32), pltpu.VMEM((2, n_pages, QK_ROPE, page), F32),
                        pltpu.VMEM((n_pages * page, KV_LORA), BF16), pltpu.VMEM((QK_ROPE, n_pages * page), BF16),
                        pltpu.SemaphoreType.DMA((2, 2))],
    )
    return pl.pallas_call(
        _mla_sample_kernel,
        grid_spec=grid_spec,
        out_shape=jax.ShapeDtypeStruct((Bd, MLA_HEADS, KV_LORA), BF16),
        compiler_params=_params(1),
        name="mla_sample",
    )(page_table, qcat.reshape(MLA_HEADS, Bd, 1, QCAT), knew.reshape(Bd, 1, QCAT), cache_lat, cache_rope_t)


def _gla_prompt_kernel(q_ref, k_ref, v_ref, la_ref, o_ref, state_ref, st_ref):
    C = GLA_CHUNK
    S = q_ref.shape[0]
    row = lax.broadcasted_iota(jnp.int32, (C, 2 * C), 0)
    col = lax.broadcasted_iota(jnp.int32, (C, 2 * C), 1)
    causal = row >= col
    tri = causal[:, :C].astype(BF16)
    pad_k = jnp.zeros((C, GLA_DK), F32)
    pad_v = jnp.zeros((C, GLA_DV), F32)
    st_ref[...] = jnp.zeros_like(st_ref)

    group = min(GLA_GROUP, S // C)

    def chunks(c, carry):
        parts = []
        for g in range(group):
            rows = pl.ds(pl.multiple_of((c * group + g) * C, C), C)
            split = jnp.dot(tri, jnp.concatenate(_split3(la_ref[rows, :]), axis=1), preferred_element_type=F32)
            b_all = (split[:, 2 * GLA_K_WIDTH:] + split[:, GLA_K_WIDTH:2 * GLA_K_WIDTH]) + split[:, :GLA_K_WIDTH]
            for h in range(GLA_HEADS):
                kcols = slice(h * GLA_DK, (h + 1) * GLA_DK)
                vcols = slice(h * GLA_DV, (h + 1) * GLA_DV)
                b = b_all[:, kcols]
                b_last = b[C - 1:C, :]
                q = q_ref[rows, kcols].astype(F32)
                k = k_ref[rows, kcols].astype(F32)
                q_in = (q * jnp.exp(b)).astype(BF16)
                k_in = jnp.concatenate([k * jnp.exp(-b), pad_k], axis=0)
                k_out = jnp.concatenate([k * jnp.exp(b_last - b), pad_k], axis=0)
                a = jnp.where(causal, _dot_nt(q_in, k_in), 0.0)
                v_t = jnp.concatenate([v_ref[rows, vcols].astype(F32), pad_v], axis=0).T.astype(BF16)
                lhs = jnp.concatenate([q_in, a.astype(BF16)], axis=1)
                parts.append((rows, h, vcols, lhs, v_t, jnp.exp(b_last), _dot(v_t, k_out)))
        for rows, h, vcols, lhs, v_t, decay, kv_t in parts:
            st = st_ref[h]
            o_ref[rows, vcols] = _dot_nt(lhs, jnp.concatenate([st.astype(BF16), v_t], axis=1)).astype(o_ref.dtype)
            st_ref[h] = st * decay + kv_t
        return carry

    lax.fori_loop(0, S // C // group, chunks, 0)
    for h in range(GLA_HEADS):
        state_ref[h] = st_ref[h].T


def _gla_prompt(q, k, v, la):
    B, S, _ = q.shape
    seq = lambda n: pl.BlockSpec((None, S, n), lambda b: (b, 0, 0))
    return pl.pallas_call(
        _gla_prompt_kernel,
        grid=(B,),
        in_specs=[seq(GLA_K_WIDTH), seq(GLA_K_WIDTH), seq(GLA_V_WIDTH), seq(GLA_K_WIDTH)],
        out_specs=[seq(GLA_V_WIDTH), pl.BlockSpec((None, GLA_HEADS, GLA_DK, GLA_DV), lambda b: (b, 0, 0, 0))],
        out_shape=[jax.ShapeDtypeStruct((B, S, GLA_V_WIDTH), BF16),
                   jax.ShapeDtypeStruct((B, GLA_HEADS, GLA_DK, GLA_DV), F32)],
        scratch_shapes=[pltpu.VMEM((GLA_HEADS, GLA_DV, GLA_DK), F32)],
        compiler_params=_params(1),
        name="gla_prompt",
    )(q, k, v, la)


def _gla_sample_kernel(q_ref, k_ref, v_ref, la_ref, st_ref, o_ref, sto_ref):
    nb = q_ref.shape[0]
    pad = jnp.zeros((GLA_DK - nb, GLA_DK), F32)
    for h in range(GLA_HEADS):
        kcols = slice(h * GLA_DK, (h + 1) * GLA_DK)
        vcols = slice(h * GLA_DV, (h + 1) * GLA_DV)
        col = lambda ref: jnp.concatenate([ref[:, kcols].astype(F32), pad], axis=0).T
        q_t, k_t, decay_t = col(q_ref), col(k_ref), jnp.exp(col(la_ref))
        for i in range(nb):
            new = decay_t[:, i:i + 1] * st_ref[i, h] + k_t[:, i:i + 1] * v_ref[i:i + 1, vcols].astype(F32)
            sto_ref[i, h] = new
            o_ref[i:i + 1, vcols] = jnp.sum(q_t[:, i:i + 1] * new, axis=0, keepdims=True).astype(o_ref.dtype)


def _gla_sample(q, k, v, la, state, nb=SAMPLES_PER_STEP):
    Bd = q.shape[0]
    rows = lambda n: pl.BlockSpec((nb, n), lambda i: (i, 0))
    st_spec = pl.BlockSpec((nb, GLA_HEADS, GLA_DK, GLA_DV), lambda i: (i, 0, 0, 0))
    return pl.pallas_call(
        _gla_sample_kernel,
        grid=(Bd // nb,),
        in_specs=[rows(GLA_K_WIDTH), rows(GLA_K_WIDTH), rows(GLA_V_WIDTH), rows(GLA_K_WIDTH), st_spec],
        out_specs=[rows(GLA_V_WIDTH), st_spec],
        out_shape=[jax.ShapeDtypeStruct((Bd, GLA_V_WIDTH), BF16), jax.ShapeDtypeStruct(state.shape, F32)],
        compiler_params=_params(1),
        name="gla_sample",
    )(q, k, v, la, state)


def _postmix_kernel(x_ref, olat_ref, og_ref, sg_ref, szm_ref, szg_ref, wuv_ref, w_o_mla_ref, g_gla_ref, w_o_gla_ref,
                    w_out_ref, g_cross_ref, w_mq_ref, h_ref, qx_ref):
    ov = jnp.concatenate(
        [_dot(olat_ref[:, h * KV_LORA:(h + 1) * KV_LORA], wuv_ref[h]).astype(BF16) for h in range(MLA_HEADS)], axis=1)
    o_mla = jnp.dot(ov, w_o_mla_ref[...], preferred_element_type=F32)
    g_gla = g_gla_ref[...]
    og = jnp.concatenate(
        [(_rms(og_ref[:, h * GLA_DV:(h + 1) * GLA_DV].astype(F32), g_gla)
          * sg_ref[:, h * GLA_DV:(h + 1) * GLA_DV].astype(F32)).astype(BF16) for h in range(GLA_HEADS)], axis=1)
    o_gla = jnp.dot(og, w_o_gla_ref[...], preferred_element_type=F32)
    merged = szm_ref[...].astype(F32) * o_mla + szg_ref[...].astype(F32) * o_gla
    h1 = x_ref[...] + _dot(merged, w_out_ref[...])
    h_ref[...] = h1
    qx_ref[...] = _dot(_rms(h1, g_cross_ref[...]), w_mq_ref[...]).astype(qx_ref.dtype)


def _postmix(x, olat, og, sg, szm, szg, w, tm):
    T = x.shape[0]
    row = lambda n: pl.BlockSpec((tm, n), lambda i: (i, 0))
    consts = [w["wuv"], w["w_o_mla"], w["g_gla"], w["w_o_gla"], w["w_out"], w["g_cross"], w["w_mq"]]
    return pl.pallas_call(
        _postmix_kernel,
        grid=(T // tm,),
        in_specs=[row(D_MODEL)] * 6 + [_full(c.shape, buffers=1) for c in consts],
        out_specs=[row(D_MODEL), row(MEM_WIDTH)],
        out_shape=[jax.ShapeDtypeStruct((T, D_MODEL), F32), jax.ShapeDtypeStruct((T, MEM_WIDTH), BF16)],
        compiler_params=_params(1),
        name="postmix",
    )(x, olat, og, sg, szm, szg, *consts)


def _memkv_kernel(mem_ref, g_ref, w_ref, k_ref, v_ref):
    kv = _dot(_rms(mem_ref[...], g_ref[...]), w_ref[...])
    k_ref[...] = kv[:, :MEM_WIDTH]
    v_ref[...] = kv[:, MEM_WIDTH:]


def _memkv(mem, g_mem, w_mkv, tm):
    T = mem.shape[0]
    row = lambda n: pl.BlockSpec((tm, n), lambda i: (i, 0))
    return pl.pallas_call(
        _memkv_kernel,
        grid=(T // tm,),
        in_specs=[row(D_MODEL), _full(g_mem.shape), _full(w_mkv.shape)],
        out_specs=[row(MEM_WIDTH), row(MEM_WIDTH)],
        out_shape=[jax.ShapeDtypeStruct((T, MEM_WIDTH), F32)] * 2,
        compiler_params=_params(1),
        name="memkv",
    )(mem, g_mem, w_mkv)


def _attend_many(problems):
    scores = [_dot_nt(q, k) * MEM_HEAD_DIM ** -0.5 for q, k, _ in problems]
    probs = []
    for s in scores:
        p = jnp.exp(s - jnp.max(s, axis=-1, keepdims=True))
        probs.append(p / jnp.sum(p, axis=-1, keepdims=True))
    return [_dot(p, v) for p, (_, _, v) in zip(probs, problems)]


def _head_cols(h):
    return slice(h * MEM_HEAD_DIM, (h + 1) * MEM_HEAD_DIM)


def _route(o, h1, w_mo_ref, g_ffn_ref, w_router_ref, b_router_ref, h_ref, xn_ref, idx_ref, rank_ref, wt_ref, cnt_ref):
    h2 = h1 + jnp.dot(o, w_mo_ref[...], preferred_element_type=F32)
    h_ref[...] = h2
    xn = _rms(h2, g_ffn_ref[...])
    xn_ref[...] = xn.astype(xn_ref.dtype)
    tb = xn.shape[0]
    logits = _dot_hi_nt(w_router_ref[...], xn) + b_router_ref[...]
    expert = lax.broadcasted_iota(jnp.int32, logits.shape, 0)
    work = logits
    hits, firsts, exps = [], [], []
    top = None
    for _ in range(TOP_K):
        best = jnp.max(work, axis=0, keepdims=True)
        first = jnp.min(jnp.where(work == best, expert, N_EXPERTS), axis=0, keepdims=True)
        hit = expert == first
        top = best if top is None else top
        hits.append(hit)
        firsts.append(first)
        exps.append(jnp.exp(best - top))
        work = jnp.where(hit, -jnp.inf, work)
    denom = (exps[0] + exps[1]) + (exps[2] + exps[3])
    chosen = jnp.zeros(logits.shape, F32)
    for hit in hits:
        chosen = chosen + jnp.where(hit, 1.0, 0.0)
    before = (lax.broadcasted_iota(jnp.int32, (tb, tb), 0) < lax.broadcasted_iota(jnp.int32, (tb, tb), 1)).astype(BF16)
    rank = jnp.dot(chosen.astype(BF16), before, preferred_element_type=F32)
    idx_ref[...] = jnp.concatenate(firsts, axis=0)
    rank_ref[...] = jnp.concatenate(
        [jnp.sum(jnp.where(hit, rank, 0.0), axis=0, keepdims=True) for hit in hits], axis=0).astype(jnp.int32)
    wt_ref[...] = jnp.concatenate([e / denom for e in exps], axis=0)
    cnt_ref[...] = jnp.sum(chosen, axis=1, keepdims=True).astype(jnp.int32)


def _cross_prompt_kernel(qx_ref, h1_ref, mk_ref, mv_ref, *rest):
    outs = _attend_many([(qx_ref[:, _head_cols(h)], mk_ref[:, _head_cols(h)], mv_ref[:, _head_cols(h)])
                         for h in range(MEM_HEADS)])
    _route(jnp.concatenate([o.astype(BF16) for o in outs], axis=1), h1_ref[...], *rest)


def _cross_sample_kernel(qx_ref, mk_ref, mv_ref, o_ref):
    rows = mk_ref.shape[1]
    pad = jnp.zeros((SUBLANES - MEM_HEADS, MEM_HEAD_DIM), qx_ref.dtype)
    head_of_row = lax.broadcasted_iota(jnp.int32, (SUBLANES, rows), 1) % MEM_HEADS
    own_head = head_of_row == lax.broadcasted_iota(jnp.int32, (SUBLANES, rows), 0)
    samples = range(qx_ref.shape[0])
    scores = [_dot_nt(jnp.concatenate([qx_ref[i:i + 1, _head_cols(h)] for h in range(MEM_HEADS)] + [pad], axis=0),
                      mk_ref[i]) * MEM_HEAD_DIM ** -0.5 for i in samples]
    probs = []
    for s in scores:
        s = jnp.where(own_head, s, -jnp.inf)
        p = jnp.exp(s - jnp.max(s, axis=-1, keepdims=True))
        probs.append(p / jnp.sum(p, axis=-1, keepdims=True))
    for i, p in zip(samples, probs):
        o = _dot(p, mv_ref[i])
        o_ref[i:i + 1, :] = jnp.concatenate([o[h:h + 1] for h in range(MEM_HEADS)], axis=1).astype(o_ref.dtype)


def _route_kernel(o_ref, h1_ref, *rest):
    _route(o_ref[...], h1_ref[...], *rest)


def _route_specs(T, tb, w):
    consts = [w["w_mo"], w["g_ffn"], w["w_router_t"], w["b_router"]]
    row = lambda n: pl.BlockSpec((tb, n), lambda i: (i, 0))
    per_choice = pl.BlockSpec((TOP_K, tb), lambda i: (0, i))
    out_specs = [row(D_MODEL), row(D_MODEL), per_choice, per_choice, per_choice,
                 pl.BlockSpec((None, N_EXPERTS, 1), lambda i: (i, 0, 0))]
    out_shape = [jax.ShapeDtypeStruct((T, D_MODEL), F32), jax.ShapeDtypeStruct((T, D_MODEL), BF16),
                 jax.ShapeDtypeStruct((TOP_K, T), jnp.int32), jax.ShapeDtypeStruct((TOP_K, T), jnp.int32),
                 jax.ShapeDtypeStruct((TOP_K, T), F32), jax.ShapeDtypeStruct((T // tb, N_EXPERTS, 1), jnp.int32)]
    return consts, out_specs, out_shape


def _cross_prompt(qx, h1, mem_k, mem_v, w, tb):
    T = qx.shape[0]
    blocks_per_mem = T // mem_k.shape[0] // tb
    consts, out_specs, out_shape = _route_specs(T, tb, w)
    row = lambda n: pl.BlockSpec((tb, n), lambda i: (i, 0))
    mem_spec = pl.BlockSpec((None, N_MEM, MEM_WIDTH), lambda i: (i // blocks_per_mem, 0, 0))
    return pl.pallas_call(
        _cross_prompt_kernel,
        grid=(T // tb,),
        in_specs=[row(MEM_WIDTH), row(D_MODEL), mem_spec, mem_spec] + [_full(c.shape) for c in consts],
        out_specs=out_specs,
        out_shape=out_shape,
        compiler_params=_params(1),
        name="cross_prompt",
    )(qx, h1, mem_k, mem_v, *consts)


def _cross_sample(qx, h1, mem_k, mem_v, w, nb=SAMPLES_PER_STEP):
    T = qx.shape[0]
    mem_spec = pl.BlockSpec((nb, N_MEM * MEM_HEADS, MEM_HEAD_DIM), lambda i: (i, 0, 0))
    rows = pl.BlockSpec((nb, MEM_WIDTH), lambda i: (i, 0))
    o = pl.pallas_call(
        _cross_sample_kernel,
        grid=(T // nb,),
        in_specs=[rows, mem_spec, mem_spec],
        out_specs=rows,
        out_shape=jax.ShapeDtypeStruct((T, MEM_WIDTH), BF16),
        compiler_params=_params(1),
        name="cross_sample",
    )(qx, mem_k, mem_v)
    consts, out_specs, out_shape = _route_specs(T, T, w)
    return pl.pallas_call(
        _route_kernel,
        grid=(1,),
        in_specs=[_full(o.shape), _full(h1.shape)] + [_full(c.shape) for c in consts],
        out_specs=out_specs,
        out_shape=out_shape,
        compiler_params=_params(1),
        name="route_sample",
    )(o, h1, *consts)


SEG_ALIGN = 16
SEL_CHUNK = 256
MOE_BLOCK = 512


def _block_rows(tb):
    worst = TOP_K * tb + N_EXPERTS * (SEG_ALIGN - 1)
    return -(-worst // SEL_CHUNK) * SEL_CHUNK


def _seg_sizes(limit):
    sizes, b = [], SEG_ALIGN
    while b <= limit:
        sizes.append(b)
        b *= 2
    return sizes[::-1]


def _for_segments(n, sizes, make_copy, act):
    for size in sizes:
        @pl.when((n & size) != 0)
        def _():
            act(make_copy(pl.multiple_of(n & (-2 * size), SEG_ALIGN), size))


def _slot_rows(idx, rank, loc_ref, j):
    pos = rank
    for e in range(N_EXPERTS):
        pos = pos + jnp.where(idx == e, loc_ref[j, e], 0)
    return pos


def _dispatch_kernel(loc_ref, pad_ref, goff_ref, total_ref, gap_off_ref, gap_len_ref, x_ref, x_last_ref, idx_ref, rank_ref,
                     xs_hbm, buf, sem, *, tm):
    j = pl.program_id(0)
    last = pl.num_programs(0) - 1
    slot = j % 2
    tb = x_ref.shape[0]
    r_blk = _block_rows(tb)
    pos = _slot_rows(idx_ref[...], rank_ref[...], loc_ref, j)
    x = jnp.where(j == last, x_last_ref[...], x_ref[...])
    for c in range(r_blk // SEL_CHUNK):
        r = lax.broadcasted_iota(jnp.int32, (SEL_CHUNK, tb), 0) + c * SEL_CHUNK
        sel = jnp.where(r == pos[0:1], 1.0, jnp.where(r == pos[1:2], 1.0, jnp.where(
            r == pos[2:3], 1.0, jnp.where(r == pos[3:4], 1.0, 0.0))))
        buf[slot, c * SEL_CHUNK:(c + 1) * SEL_CHUNK, :] = jnp.dot(
            sel.astype(BF16), x, preferred_element_type=F32).astype(BF16)

    sizes = _seg_sizes(tb)

    def start_segments(e, carry):
        src, dst = loc_ref[j, e], goff_ref[j, e]
        _for_segments(pad_ref[j, e], sizes, lambda off, size: pltpu.make_async_copy(
            buf.at[slot, pl.ds(pl.multiple_of(src + off, SEG_ALIGN), size)],
            xs_hbm.at[pl.ds(pl.multiple_of(dst + off, SEG_ALIGN), size)], sem.at[slot]), lambda cp: cp.start())
        return carry

    lax.fori_loop(0, N_EXPERTS, start_segments, 0)

    def wait_block(blk, slot):
        _for_segments(total_ref[blk], _seg_sizes(r_blk), lambda off, size: pltpu.make_async_copy(
            buf.at[slot, pl.ds(0, size)], xs_hbm.at[pl.ds(0, size)], sem.at[slot]), lambda cp: cp.wait())

    @pl.when(j > 0)
    def _():
        wait_block(j - 1, 1 - slot)

    @pl.when(j == last)
    def _():
        wait_block(j, slot)
        buf[1 - slot, 0:tm, :] = jnp.zeros((tm, D_MODEL), BF16)
        gap_sizes = _seg_sizes(tm - SEG_ALIGN)

        def for_gaps(act):
            def body(e, carry):
                dst = gap_off_ref[e]
                _for_segments(gap_len_ref[e], gap_sizes, lambda off, size: pltpu.make_async_copy(
                    buf.at[1 - slot, pl.ds(0, size)], xs_hbm.at[pl.ds(pl.multiple_of(dst + off, SEG_ALIGN), size)],
                    sem.at[1 - slot]), act)
                return carry
            lax.fori_loop(0, N_EXPERTS, body, 0)

        for_gaps(lambda cp: cp.start())
        for_gaps(lambda cp: cp.wait())


def _ffn_kernel(tile_expert_ref, n_valid_ref, x_ref, wgu_f32_ref, bgu_ref, wd_f32_ref, bd_ref, o_ref, wgu_ref, wd_ref):
    t = pl.program_id(0)

    @pl.when(t < n_valid_ref[0])
    def _():
        @pl.when((t == 0) | (tile_expert_ref[t] != tile_expert_ref[jnp.maximum(t - 1, 0)]))
        def _():
            wgu_ref[...] = wgu_f32_ref[...].astype(BF16)
            wd_ref[...] = wd_f32_ref[...].astype(BF16)

        x = x_ref[...]
        bgu = bgu_ref[...]
        gate = jnp.dot(x, wgu_ref[:, :D_FF], preferred_element_type=F32) + bgu[:, :D_FF]
        up = jnp.dot(x, wgu_ref[:, D_FF:], preferred_element_type=F32) + bgu[:, D_FF:]
        gate = jnp.minimum(gate, SWIGLU_LIMIT)
        up = jnp.clip(up, -SWIGLU_LIMIT, SWIGLU_LIMIT)
        hidden = (up + 1.0) * gate * jax.nn.sigmoid(SWIGLU_ALPHA * gate)
        o_ref[...] = (_dot(hidden, wd_ref[...]) + bd_ref[...]).astype(o_ref.dtype)


def _combine_kernel(loc_ref, pad_ref, goff_ref, total_ref, idx0_ref, rank0_ref, wt0_ref, idx_ref, rank_ref, wt_ref,
                    h_ref, h_last_ref, g_final_ref, o_hbm, y_ref, y_last_ref, buf, sel_ref, sem):
    j = pl.program_id(0)
    last = pl.num_programs(0) - 1
    slot = j % 2
    tb = h_ref.shape[0]
    r_blk = buf.shape[1]
    sizes = _seg_sizes(tb)

    def fetch(blk, slot):
        def body(e, carry):
            dst, src = loc_ref[blk, e], goff_ref[blk, e]
            _for_segments(pad_ref[blk, e], sizes, lambda off, size: pltpu.make_async_copy(
                o_hbm.at[pl.ds(pl.multiple_of(src + off, SEG_ALIGN), size)],
                buf.at[slot, pl.ds(pl.multiple_of(dst + off, SEG_ALIGN), size)], sem.at[slot]), lambda cp: cp.start())
            return carry
        lax.fori_loop(0, N_EXPERTS, body, 0)

    def build(blk, slot, idx, rank, wt):
        pos = _slot_rows(idx, rank, loc_ref, blk)
        for c in range(r_blk // SEL_CHUNK):
            r = lax.broadcasted_iota(jnp.int32, (SEL_CHUNK, tb), 0) + c * SEL_CHUNK
            sel = jnp.zeros((SEL_CHUNK, tb), F32)
            for k in range(TOP_K):
                sel = jnp.where(r == pos[k:k + 1], wt[k:k + 1], sel)
            sel_ref[slot, c * SEL_CHUNK:(c + 1) * SEL_CHUNK, :] = sel.astype(BF16)

    @pl.when(j == 0)
    def _():
        buf[...] = jnp.zeros_like(buf)
        fetch(0, 0)
        build(0, 0, idx0_ref[...], rank0_ref[...], wt0_ref[...])

    @pl.when(j < last)
    def _():
        fetch(j + 1, 1 - slot)

    _for_segments(total_ref[j], _seg_sizes(r_blk), lambda off, size: pltpu.make_async_copy(
        o_hbm.at[pl.ds(0, size)], buf.at[slot, pl.ds(0, size)], sem.at[slot]), lambda cp: cp.wait())
    moe = _dot_tn(sel_ref[slot], buf[slot])
    y = _rms(jnp.where(j == last, h_last_ref[...], h_ref[...]) + moe, g_final_ref[...])
    build(jnp.minimum(j + 1, last), 1 - slot, idx_ref[...], rank_ref[...], wt_ref[...])

    @pl.when(j < last)
    def _():
        y_ref[...] = y

    @pl.when(j == last)
    def _():
        y_last_ref[...] = y


def _moe(xn, xn_last, idx, rank, wt, cnt, h2, h2_last, w, tb, tm):
    T = xn.shape[0]
    nb = T // tb + 1
    r_blk = _block_rows(tb)
    i32 = jnp.int32
    pad = (cnt + (SEG_ALIGN - 1)) // SEG_ALIGN * SEG_ALIGN
    loc = jnp.cumsum(pad, axis=1) - pad
    seg = jnp.sum(pad, axis=0)
    region = (seg + (tm - 1)) // tm * tm
    region_end = jnp.cumsum(region)
    region_start = region_end - region
    goff = region_start[None, :] + jnp.cumsum(pad, axis=0) - pad
    n_tiles = -(-(nb * (TOP_K * tb + N_EXPERTS * (SEG_ALIGN - 1)) + N_EXPERTS * (tm - 1)) // tm)
    n_valid = (region_end[-1] // tm).astype(i32).reshape(1)
    tile = jnp.minimum(jnp.arange(n_tiles, dtype=i32), n_valid - 1)
    tile_expert = jnp.minimum(jnp.sum((region_end[None, :] <= (tile * tm)[:, None]).astype(i32), axis=1), N_EXPERTS - 1)
    plan = [a.astype(i32) for a in (loc, pad, goff, jnp.sum(pad, axis=1))]
    gaps = [(region_start + seg).astype(i32), (region - seg).astype(i32)]

    choice = pl.BlockSpec((TOP_K, tb), lambda j, *_: (0, j))
    body_rows = pl.BlockSpec((tb, D_MODEL), lambda j, *_: (jnp.minimum(j, nb - 2), 0))
    last_rows = pl.BlockSpec((tb, D_MODEL), lambda j, *_: (0, 0))
    xs = pl.pallas_call(
        functools.partial(_dispatch_kernel, tm=tm),
        grid_spec=pltpu.PrefetchScalarGridSpec(
            num_scalar_prefetch=6, grid=(nb,),
            in_specs=[body_rows, last_rows, choice, choice],
            out_specs=pl.BlockSpec(memory_space=pl.ANY),
            scratch_shapes=[pltpu.VMEM((2, r_blk, D_MODEL), BF16), pltpu.SemaphoreType.DMA((2,))]),
        out_shape=jax.ShapeDtypeStruct((n_tiles * tm, D_MODEL), BF16),
        compiler_params=_params(1),
        name="moe_dispatch",
    )(*plan, *gaps, xn, xn_last, idx, rank)

    rows = pl.BlockSpec((tm, D_MODEL), lambda t, te, nv: (jnp.minimum(t, nv[0] - 1), 0))
    per_e = lambda a, b: pl.BlockSpec((None, a, b), lambda t, te, nv: (te[t], 0, 0))
    out = pl.pallas_call(
        _ffn_kernel,
        grid_spec=pltpu.PrefetchScalarGridSpec(
            num_scalar_prefetch=2, grid=(n_tiles,),
            in_specs=[rows, per_e(D_MODEL, 2 * D_FF), per_e(1, 2 * D_FF), per_e(D_FF, D_MODEL), per_e(1, D_MODEL)],
            out_specs=rows,
            scratch_shapes=[pltpu.VMEM((D_MODEL, 2 * D_FF), BF16), pltpu.VMEM((D_FF, D_MODEL), BF16)]),
        out_shape=jax.ShapeDtypeStruct((n_tiles * tm, D_MODEL), BF16),
        compiler_params=_params(1),
        name="moe_ffn",
    )(tile_expert, n_valid, xs, w["w_gate_up"], w["b_gate_up"], w["w_down"], w["b_down"])

    first = pl.BlockSpec((TOP_K, tb), lambda j, *_: (0, 0))
    ahead = pl.BlockSpec((TOP_K, tb), lambda j, *_: (0, jnp.minimum(j + 1, nb - 1)))
    return pl.pallas_call(
        _combine_kernel,
        grid_spec=pltpu.PrefetchScalarGridSpec(
            num_scalar_prefetch=4, grid=(nb,),
            in_specs=[first, first, first, ahead, ahead, ahead, body_rows, last_rows,
                      pl.BlockSpec((1, D_MODEL), lambda j, *_: (0, 0)), pl.BlockSpec(memory_space=pl.ANY)],
            out_specs=[body_rows, last_rows],
            scratch_shapes=[pltpu.VMEM((2, r_blk, D_MODEL), BF16), pltpu.VMEM((2, r_blk, tb), BF16),
                            pltpu.SemaphoreType.DMA((2,))]),
        out_shape=[jax.ShapeDtypeStruct((T, D_MODEL), F32), jax.ShapeDtypeStruct((tb, D_MODEL), F32)],
        compiler_params=_params(1),
        name="moe_combine",
    )(*plan, idx, rank, wt, idx, rank, wt, h2, h2_last, w["g_final"], out)


def _rope_table(pos):
    half = QK_ROPE // 2
    inv = ROPE_THETA ** (-jnp.arange(half, dtype=F32) * 2.0 / QK_ROPE)
    ang = pos[:, None] * inv[None, :]
    cos, sin, zero = jnp.cos(ang), jnp.sin(ang), jnp.zeros_like(ang)
    return jnp.concatenate([cos, cos, cos, cos, -sin, zero, -sin, zero, zero, sin, zero, sin], axis=1)


def _prepare_weights(g_attn, w_in, g_q, g_kv, w_uq, w_uk, w_uv, w_a2, b_a, g_gla, w_o_mla, w_o_gla, w_out,
                     g_mem, w_mk, w_mv, g_cross, w_mq, w_mo, g_ffn, w_router, b_router, w_gate_up, b_gate_up,
                     w_down, b_down, g_final):
    splits = np.cumsum((0,) + IN_SPLITS)
    parts = [w_in[0][:, splits[i]:splits[i + 1]] for i in range(len(IN_SPLITS))]
    parts = [jnp.pad(p, ((0, 0), (0, wd - p.shape[1]))) for p, wd in zip(parts, _PACKED_WIDTHS)]
    wq = w_uq[0].reshape(Q_LORA, MLA_HEADS, QK_NOPE + QK_ROPE)
    wq_rope = jnp.pad(wq[:, :, QK_NOPE:], ((0, 0), (0, 0), (0, LANES - QK_ROPE)))
    row = lambda v: v.reshape(1, -1)
    return dict(
        g_attn=row(g_attn[0]), w_in=jnp.concatenate(parts, axis=1).astype(BF16), g_q=row(g_q[0]), g_kv=row(g_kv[0]),
        w_uq=jnp.concatenate([_fold_latent_query(jnp.transpose(wq[:, :, :QK_NOPE], (1, 0, 2)),
                                                 jnp.transpose(w_uk[0], (1, 0, 2))),
                              wq_rope.reshape(Q_LORA, -1).astype(BF16)], axis=1),
        wuv=jnp.transpose(w_uv[0], (1, 0, 2)).astype(BF16),
        w_a2=jnp.pad(w_a2[0], ((0, LANES - GLA_RANK), (0, 0))).astype(BF16), b_a=row(b_a[0]),
        g_gla=row(g_gla[0]), w_o_mla=w_o_mla[0].astype(BF16), w_o_gla=w_o_gla[0].astype(BF16),
        w_out=w_out[0].astype(BF16), g_mem=row(g_mem[0]),
        w_mkv=jnp.concatenate([w_mk[0], w_mv[0]], axis=1).astype(BF16),
        g_cross=row(g_cross[0]), w_mq=w_mq[0].astype(BF16), w_mo=w_mo[0].astype(BF16), g_ffn=row(g_ffn[0]),
        w_router_t=w_router[0].T, b_router=b_router[0].reshape(-1, 1),
        w_gate_up=w_gate_up[0], b_gate_up=b_gate_up[0].reshape(N_EXPERTS, 1, 2 * D_FF),
        w_down=w_down[0], b_down=b_down[0].reshape(N_EXPERTS, 1, D_MODEL), g_final=row(g_final),
    )


def kernel(x_prompt, x_sample, cache_kv_latent, cache_k_rope, state_gla, cache_mem_k, cache_mem_v, page_table, mem_prompt, g_attn, w_in, g_q, g_kv, w_uq, w_uk, w_uv, w_a2, b_a, g_gla, w_o_mla, w_o_gla, w_out, g_mem, w_mk, w_mv, g_cross, w_mq, w_mo, g_ffn, w_router, b_router, w_gate_up, b_gate_up, w_down, b_down, g_final):
    w = _prepare_weights(g_attn, w_in, g_q, g_kv, w_uq, w_uk, w_uv, w_a2, b_a, g_gla, w_o_mla, w_o_gla, w_out,
                         g_mem, w_mk, w_mv, g_cross, w_mq, w_mo, g_ffn, w_router, b_router, w_gate_up, b_gate_up,
                         w_down, b_down, g_final)
    B, S, D = x_prompt.shape
    Bd = x_sample.shape[0]
    past_len = page_table.shape[1] * cache_kv_latent.shape[2]

    tm = min(MLA_BLOCK, S)
    xp = x_prompt.reshape(B * S, D)
    (qcat, ckv, krope, kcat, kcat_t, gq, gk, gv, sg, la, szm, szg) = _premix(
        xp, _rope_table(jnp.arange(S, dtype=F32)), w, min(PREMIX_ROWS, S), tm, BF16)
    olat = _mla_prompt(qcat.reshape(MLA_HEADS, B, S, QCAT), kcat.reshape(B, S, QCAT), kcat_t, tm)
    og, state_p = _gla_prompt(gq.reshape(B, S, -1), gk.reshape(B, S, -1), gv.reshape(B, S, -1), la.reshape(B, S, -1))
    h1, qx = _postmix(xp, olat.reshape(B * S, -1), og.reshape(B * S, -1), sg, szm, szg, w, min(POSTMIX_ROWS, S))
    mem_k, mem_v = _memkv(mem_prompt.reshape(B * N_MEM, D), w["g_mem"], w["w_mkv"], min(MEMKV_ROWS, B * N_MEM))
    tb = min(MOE_BLOCK, S)
    h2, xn2, idx, rank, wt, cnt = _cross_prompt(qx, h1, mem_k.reshape(B, N_MEM, MEM_WIDTH),
                                                mem_v.reshape(B, N_MEM, MEM_WIDTH), w, tb)

    xs = x_sample.reshape(Bd, D)
    (qcat_s, ckv_s, krope_s, kcat_s, _, gq_s, gk_s, gv_s, sg_s, la_s, szm_s, szg_s) = _premix(
        xs, _rope_table(jnp.full((Bd,), past_len, F32)), w, Bd, Bd, F32)
    olat_s = _mla_sample(page_table, qcat_s, kcat_s, cache_kv_latent[0], jnp.swapaxes(cache_k_rope[0], 1, 2))
    og_s, state_s = _gla_sample(gq_s, gk_s, gv_s, la_s, state_gla[0])
    h1_s, qx_s = _postmix(xs, olat_s.reshape(Bd, -1), og_s, sg_s, szm_s, szg_s, w, Bd)
    h2_s, xn2_s, idx_s, rank_s, wt_s, cnt_s = _cross_sample(
        qx_s, h1_s, cache_mem_k.reshape(Bd, N_MEM * MEM_HEADS, MEM_HEAD_DIM),
        cache_mem_v.reshape(Bd, N_MEM * MEM_HEADS, MEM_HEAD_DIM), w)

    fill = lambda a, v: jnp.pad(a, ((0, 0), (0, tb - Bd)), constant_values=v)
    rows = lambda a: jnp.pad(a, ((0, tb - Bd), (0, 0)))
    y_prompt, y_last = _moe(
        xn2, rows(xn2_s), jnp.concatenate([idx, fill(idx_s, N_EXPERTS)], axis=1),
        jnp.concatenate([rank, fill(rank_s, -1)], axis=1), jnp.concatenate([wt, fill(wt_s, 0.0)], axis=1),
        jnp.concatenate([cnt.reshape(-1, N_EXPERTS), cnt_s.reshape(-1, N_EXPERTS)], axis=0), h2, rows(h2_s), w, tb, tb)
    y_sample = y_last[:Bd]

    return (y_prompt.reshape(B, S, D), y_sample.reshape(Bd, 1, D),
            ckv.reshape(1, B, S, KV_LORA), krope.reshape(1, B, S, QK_ROPE), state_p[None],
            mem_k.reshape(1, B, N_MEM, MEM_HEADS, MEM_HEAD_DIM), mem_v.reshape(1, B, N_MEM, MEM_HEADS, MEM_HEAD_DIM),
            ckv_s.reshape(1, Bd, 1, KV_LORA), krope_s.reshape(1, Bd, 1, QK_ROPE), state_s[None])
```
